```python
import jax
import jax.numpy as jnp
from jax import lax
import numpy as np

D_MODEL = 2048
BATCH = 2
SEQ = 16384
DEPTH = 4

CTX_LEN = 256
GRID_W = 64
HEAD_DIM = 64
NA_HEADS = 8
NA_WIN_ROWS = 8
NA_WIN_COLS = 16
SC_WIDTH = 1024
SC_CONV_WIDTH = 3
WA_Q_HEADS = 8
WA_KV_HEADS = 2
WA_GROUP = WA_Q_HEADS // WA_KV_HEADS
WA_WINDOW = 128
WA_BLOCK = 128
ROPE_BASE = 10000.0
N_BRANCHES = 3
N_EXPERTS = 32
TOP_K = 4
D_EXPERT = 512
SWIGLU_LIMIT = 7.0
SWIGLU_ALPHA = 1.702
MOE_BLOCK = 256
N_ADA = 6
RMS_EPS = 1e-6
NEG_INF = -1e30

NA_DIM = NA_HEADS * HEAD_DIM
WA_Q_DIM = WA_Q_HEADS * HEAD_DIM
WA_KV_DIM = WA_KV_HEADS * HEAD_DIM
KV_SIZES = (NA_DIM, NA_DIM, WA_KV_DIM, WA_KV_DIM)
REST_SIZES = (NA_DIM, WA_Q_DIM, SC_WIDTH, SC_WIDTH, SC_WIDTH, N_BRANCHES * D_MODEL)
N_KV_COLS = sum(KV_SIZES)
N_IN_COLS = N_KV_COLS + sum(REST_SIZES)

kernel_name = 'hybrid_natten_shortconv_swa_moe_dit'


def rms_norm(x, g):
    xf = x.astype(jnp.float32)
    y = xf * lax.rsqrt(jnp.mean(xf * xf, axis=-1, keepdims=True) + RMS_EPS)
    return (y * g.astype(jnp.float32)).astype(x.dtype)


def modulate(x, shift, scale):
    return x * (1.0 + scale) + shift


def split_cols(p, sizes):
    cuts = [int(v) for v in np.cumsum(sizes)[:-1]]
    return jnp.split(p, cuts, axis=-1)


def rope_2d(x):
    n = x.shape[1]
    t = jnp.arange(n)
    quarter = HEAD_DIM // 4
    inv = ROPE_BASE ** (-jnp.arange(quarter, dtype=jnp.float32) / quarter)
    bshape = (1, n) + (1,) * (x.ndim - 3) + (quarter,)
    xf = x.astype(jnp.float32)
    parts = []
    for axis_i, pos in enumerate((t // GRID_W, t % GRID_W)):
        ang = (pos.astype(jnp.float32)[:, None] * inv[None, :]).reshape(bshape)
        cos, sin = jnp.cos(ang), jnp.sin(ang)
        seg = xf[..., axis_i * 2 * quarter:(axis_i + 1) * 2 * quarter]
        x1, x2 = seg[..., :quarter], seg[..., quarter:]
        parts += [x1 * cos - x2 * sin, x2 * cos + x1 * sin]
    return jnp.concatenate(parts, axis=-1).astype(x.dtype)


def neighbourhood_attention(q, k, v, k_ctx, v_ctx, rpb):
    b, s, h, dh = q.shape
    rows = s // GRID_W
    win_r = min(NA_WIN_ROWS, rows)
    scale = HEAD_DIM ** -0.5
    kg = k.reshape(b, rows, GRID_W, h, dh)
    vg = v.reshape(b, rows, GRID_W, h, dh)
    q_rows = q.reshape(b, rows, GRID_W, h, dh).transpose(1, 0, 2, 3, 4)
    col = np.arange(GRID_W)
    c0 = np.clip(col - NA_WIN_COLS // 2, 0, GRID_W - NA_WIN_COLS)
    col_idx = c0[:, None] + np.arange(NA_WIN_COLS)[None, :]
    col_off = col_idx - col[:, None] + NA_WIN_COLS - 1
    rpb_cols = rpb[:, :, col_off]
    n_nb = win_r * NA_WIN_COLS

    def one_row(args):
        r, q_row = args
        r0 = jnp.clip(r - win_r // 2, 0, rows - win_r)
        k_nb = lax.dynamic_slice_in_dim(kg, r0, win_r, axis=1)[:, :, col_idx]
        v_nb = lax.dynamic_slice_in_dim(vg, r0, win_r, axis=1)[:, :, col_idx]
        row_off = r0 + jnp.arange(win_r) - r + NA_WIN_ROWS - 1
        bias = jnp.take(rpb_cols, row_off, axis=1).transpose(0, 2, 1, 3)
        s_nb = jnp.einsum('bqhd,brqjhd->bhqrj', q_row, k_nb, preferred_element_type=jnp.float32) * scale
        s_nb = (s_nb + bias[None].astype(jnp.float32)).reshape(b, h, GRID_W, n_nb)
        s_ctx = jnp.einsum('bqhd,blhd->bhql', q_row, k_ctx, preferred_element_type=jnp.float32) * scale
        p = jax.nn.softmax(jnp.concatenate([s_nb, s_ctx], axis=-1), axis=-1).astype(v.dtype)
        p_nb = p[..., :n_nb].reshape(b, h, GRID_W, win_r, NA_WIN_COLS)
        return (jnp.einsum('bhqrj,brqjhd->bqhd', p_nb, v_nb)
                + jnp.einsum('bhql,blhd->bqhd', p[..., n_nb:], v_ctx))

    out = lax.map(one_row, (jnp.arange(rows), q_rows))
    return out.transpose(1, 0, 2, 3, 4).reshape(b, s, h * dh)


def windowed_gqa(q, k, v, k_ctx, v_ctx, sinks):
    b, s, hkv, g, dh = q.shape
    n_blk = s // WA_BLOCK
    span = WA_BLOCK + 2 * WA_WINDOW
    scale = HEAD_DIM ** -0.5
    kp = jnp.pad(k, ((0, 0), (WA_WINDOW, WA_WINDOW), (0, 0), (0, 0)))
    vp = jnp.pad(v, ((0, 0), (WA_WINDOW, WA_WINDOW), (0, 0), (0, 0)))
    q_blocks = q.reshape(b, n_blk, WA_BLOCK, hkv, g, dh).transpose(1, 0, 2, 3, 4, 5)
    rel = np.arange(span)[None, :] - WA_WINDOW - np.arange(WA_BLOCK)[:, None]
    band = jnp.asarray(np.abs(rel) <= WA_WINDOW)
    s_sink = jnp.broadcast_to(sinks.astype(jnp.float32)[None, :, :, None, None], (b, hkv, g, WA_BLOCK, 1))

    def one_block(args):
        i, q_blk = args
        start = i * WA_BLOCK
        k_win = lax.dynamic_slice_in_dim(kp, start, span, axis=1)
        v_win = lax.dynamic_slice_in_dim(vp, start, span, axis=1)
        key_pos = start - WA_WINDOW + jnp.arange(span)
        valid = jnp.logical_and(band, ((key_pos >= 0) & (key_pos < s))[None, :])
        s_loc = jnp.einsum('bqhgd,bkhd->bhgqk', q_blk, k_win, preferred_element_type=jnp.float32) * scale
        s_loc = jnp.where(valid, s_loc, NEG_INF)
        s_ctx = jnp.einsum('bqhgd,blhd->bhgql', q_blk, k_ctx, preferred_element_type=jnp.float32) * scale
        p = jax.nn.softmax(jnp.concatenate([s_loc, s_ctx, s_sink], axis=-1), axis=-1).astype(v.dtype)
        return (jnp.einsum('bhgqk,bkhd->bqhgd', p[..., :span], v_win)
                + jnp.einsum('bhgql,blhd->bqhgd', p[..., span:-1], v_ctx))

    out = lax.map(one_block, (jnp.arange(n_blk), q_blocks))
    return out.transpose(1, 0, 2, 3, 4, 5).reshape(b, s, hkv * g * dh)


def context_attention(q, k, v, sinks):
    b, l, hkv, g, dh = q.shape
    sc = jnp.einsum('bqhgd,bkhd->bhgqk', q, k, preferred_element_type=jnp.float32) * (HEAD_DIM ** -0.5)
    if sinks is not None:
        sink = jnp.broadcast_to(sinks.astype(jnp.float32)[None, :, :, None, None], (b, hkv, g, l, 1))
        sc = jnp.concatenate([sc, sink], axis=-1)
    p = jax.nn.softmax(sc, axis=-1)[..., :l].astype(v.dtype)
    return jnp.einsum('bhgqk,bkhd->bqhgd', p, v).reshape(b, l, hkv * g * dh)


def short_conv(b_gate, c_gate, h, conv_w):
    u = c_gate * h
    n = u.shape[1]
    pad = SC_CONV_WIDTH // 2
    up = jnp.pad(u, ((0, 0), (pad, pad), (0, 0)))
    y = up[:, 0:n] * conv_w[0]
    for i in range(1, SC_CONV_WIDTH):
        y = y + up[:, i:i + n] * conv_w[i]
    return b_gate * y


def merge_branches(gate_logits, o_na, o_sc, o_wa, w_na_out, w_sc_out, w_wa_out, w_out):
    g_na, g_sc, g_wa = jnp.split(jax.nn.sigmoid(gate_logits), N_BRANCHES, axis=-1)
    merged = g_na * (o_na @ w_na_out) + g_sc * (o_sc @ w_sc_out) + g_wa * (o_wa @ w_wa_out)
    return merged @ w_out


def hybrid_mixer(h_lat, h_ctx, w_in, rpb, conv_w, sinks, w_na_out, w_sc_out, w_wa_out, w_out, with_ctx_out):
    b, s, _ = h_lat.shape
    l = h_ctx.shape[1]

    def heads(t, n):
        return t.reshape(t.shape[:2] + (n, HEAD_DIM))

    p_lat = h_lat @ w_in
    na_k, na_v, wa_k, wa_v, na_q, wa_q, sc_b, sc_c, sc_h, gates = split_cols(p_lat, KV_SIZES + REST_SIZES)
    if with_ctx_out:
        p_ctx = h_ctx @ w_in
        ctx_parts = split_cols(p_ctx, KV_SIZES + REST_SIZES)
    else:
        p_ctx = h_ctx @ w_in[:, :N_KV_COLS]
        ctx_parts = split_cols(p_ctx, KV_SIZES)
    na_kc, na_vc = heads(ctx_parts[0], NA_HEADS), heads(ctx_parts[1], NA_HEADS)
    wa_kc, wa_vc = heads(ctx_parts[2], WA_KV_HEADS), heads(ctx_parts[3], WA_KV_HEADS)
    sinks_g = sinks.reshape(WA_KV_HEADS, WA_GROUP)

    o_na = neighbourhood_attention(heads(na_q, NA_HEADS), heads(na_k, NA_HEADS), heads(na_v, NA_HEADS),
                                   na_kc, na_vc, rpb)
    o_sc = short_conv(sc_b, sc_c, sc_h, conv_w)
    q_wa = rope_2d(wa_q.reshape(b, s, WA_KV_HEADS, WA_GROUP, HEAD_DIM))
    k_wa = rope_2d(heads(wa_k, WA_KV_HEADS))
    o_wa = windowed_gqa(q_wa, k_wa, heads(wa_v, WA_KV_HEADS), wa_kc, wa_vc, sinks_g)
    y_lat = merge_branches(gates, o_na, o_sc, o_wa, w_na_out, w_sc_out, w_wa_out, w_out)
    if not with_ctx_out:
        return y_lat, None

    na_qc, wa_qc, sc_bc, sc_cc, sc_hc, gates_c = ctx_parts[4:]
    o_na_c = context_attention(heads(na_qc, NA_HEADS)[:, :, :, None], na_kc, na_vc, None)
    o_sc_c = short_conv(sc_bc, sc_cc, sc_hc, conv_w)
    o_wa_c = context_attention(wa_qc.reshape(b, l, WA_KV_HEADS, WA_GROUP, HEAD_DIM), wa_kc, wa_vc, sinks_g)
    y_ctx = merge_branches(gates_c, o_na_c, o_sc_c, o_wa_c, w_na_out, w_sc_out, w_wa_out, w_out)
    return y_lat, y_ctx


def expert_swiglu(x, w1, b1, w2, b2):
    u = x @ w1 + b1
    glu, lin = jnp.split(u, 2, axis=-1)
    glu = jnp.minimum(glu, SWIGLU_LIMIT)
    lin = jnp.clip(lin, -SWIGLU_LIMIT, SWIGLU_LIMIT)
    return (glu * jax.nn.sigmoid(SWIGLU_ALPHA * glu) * (lin + 1.0)) @ w2 + b2


def moe_tokens(h, w_router, b_router, w1, b1, w2, b2):
    n_tok, d = h.shape
    logits = jnp.dot(h, w_router, preferred_element_type=jnp.float32) + b_router.astype(jnp.float32)
    top_logit, top_e = lax.top_k(logits, TOP_K)
    gate = jax.nn.softmax(top_logit, axis=-1)
    n_asg = n_tok * TOP_K
    flat_e = top_e.reshape(n_asg)
    order = jnp.argsort(flat_e)
    e_sorted = flat_e[order]
    tok_sorted = order // TOP_K
    counts = jnp.bincount(flat_e, length=N_EXPERTS)
    padded = (counts + MOE_BLOCK - 1) // MOE_BLOCK * MOE_BLOCK
    starts = jnp.cumsum(counts) - counts
    pad_ends = jnp.cumsum(padded)
    slot = (pad_ends - padded)[e_sorted] + jnp.arange(n_asg) - starts[e_sorted]
    n_blocks = -(-(n_asg + N_EXPERTS * (MOE_BLOCK - 1)) // MOE_BLOCK)
    slot_tok = jnp.full((n_blocks * MOE_BLOCK,), n_tok, dtype=jnp.int32).at[slot].set(tok_sorted.astype(jnp.int32))
    block_e = jnp.minimum(jnp.searchsorted(pad_ends, jnp.arange(n_blocks) * MOE_BLOCK, side='right'),
                          N_EXPERTS - 1)
    h_pad = jnp.concatenate([h, jnp.zeros((1, d), h.dtype)], axis=0)

    def expert_block(args):
        tok, e = args
        return expert_swiglu(h_pad[tok], w1[e], b1[e], w2[e], b2[e])

    y_slot = lax.map(expert_block, (slot_tok.reshape(n_blocks, MOE_BLOCK), block_e))
    y_asg = y_slot.reshape(n_blocks * MOE_BLOCK, d)[slot]
    w_asg = gate.reshape(n_asg)[order].astype(h.dtype)
    return jax.ops.segment_sum(y_asg * w_asg[:, None], tok_sorted, num_segments=n_tok)


def moe_ffn(h, w_router, b_router, w1, b1, w2, b2):
    return lax.map(lambda hb: moe_tokens(hb, w_router, b_router, w1, b1, w2, b2), h)


def setup_inputs(seed: int = 0) -> dict:
    key = jax.random.key(seed)
    ks = jax.random.split(key, 24)
    d = D_MODEL

    def nrm(k, shape, s):
        return jax.random.normal(k, shape, jnp.float32) * s

    return {
        'x': nrm(ks[0], (BATCH, SEQ, d), 1.0),
        'c': nrm(ks[1], (BATCH, d), 1.0),
        'ctx': nrm(ks[2], (BATCH, CTX_LEN, d), 1.0),
        'c_ctx': nrm(ks[3], (d,), 1.0),
        'w_ada': nrm(ks[4], (DEPTH, d, N_ADA * d), 0.5 * d ** -0.5),
        'b_ada': nrm(ks[5], (DEPTH, N_ADA * d), 0.02),
        'g_mix': 1.0 + nrm(ks[6], (DEPTH, d), 0.05),
        'w_in': nrm(ks[7], (DEPTH, d, N_IN_COLS), d ** -0.5),
        'na_rpb': nrm(ks[8], (DEPTH, NA_HEADS, 2 * NA_WIN_ROWS - 1, 2 * NA_WIN_COLS - 1), 0.1),
        'sc_conv': nrm(ks[9], (DEPTH, SC_CONV_WIDTH, SC_WIDTH), SC_CONV_WIDTH ** -0.5),
        'wa_sinks': nrm(ks[10], (DEPTH, WA_Q_HEADS), 0.5),
        'w_na_out': nrm(ks[11], (DEPTH, NA_DIM, d), NA_DIM ** -0.5),
        'w_sc_out': nrm(ks[12], (DEPTH, SC_WIDTH, d), SC_WIDTH ** -0.5),
        'w_wa_out': nrm(ks[13], (DEPTH, WA_Q_DIM, d), WA_Q_DIM ** -0.5),
        'w_out': nrm(ks[14], (DEPTH, d, d), d ** -0.5),
        'g_ffn': 1.0 + nrm(ks[15], (DEPTH, d), 0.05),
        'w_router': nrm(ks[16], (DEPTH, d, N_EXPERTS), d ** -0.5),
        'b_router': nrm(ks[17], (DEPTH, N_EXPERTS), 0.01),
        'w_exp_in': nrm(ks[18], (DEPTH, N_EXPERTS, d, 2 * D_EXPERT), d ** -0.5),
        'b_exp_in': nrm(ks[19], (DEPTH, N_EXPERTS, 2 * D_EXPERT), 0.02),
        'w_exp_out': nrm(ks[20], (DEPTH, N_EXPERTS, D_EXPERT, d), D_EXPERT ** -0.5),
        'b_exp_out': nrm(ks[21], (DEPTH, N_EXPERTS, d), 0.02),
        'g_final': 1.0 + nrm(ks[22], (d,), 0.05),
    }


def reference(x, c, ctx, c_ctx, w_ada, b_ada, g_mix, w_in, na_rpb, sc_conv, wa_sinks, w_na_out, w_sc_out,
              w_wa_out, w_out, g_ffn, w_router, b_router, w_exp_in, b_exp_in, w_exp_out, b_exp_out, g_final):
    d = x.shape[-1]
    silu_c = jax.nn.silu(c)
    silu_cc = jax.nn.silu(c_ctx)
    x_lat, x_ctx = x, ctx
    for l in range(DEPTH):
        last = l == DEPTH - 1
        ada = (silu_c @ w_ada[l] + b_ada[l])[:, None, :]
        sh_m, sc_m, gt_m, sh_f, sc_f, gt_f = jnp.split(ada, N_ADA, axis=-1)
        n_ctx_mod = 2 if last else N_ADA
        ada_c = silu_cc @ w_ada[l][:, :n_ctx_mod * d] + b_ada[l][:n_ctx_mod * d]
        mod_c = jnp.split(ada_c, n_ctx_mod, axis=-1)
        h_lat = modulate(rms_norm(x_lat, g_mix[l]), sh_m, sc_m)
        h_ctx = modulate(rms_norm(x_ctx, g_mix[l]), mod_c[0], mod_c[1])
        y_lat, y_ctx = hybrid_mixer(h_lat, h_ctx, w_in[l], na_rpb[l], sc_conv[l], wa_sinks[l], w_na_out[l],
                                    w_sc_out[l], w_wa_out[l], w_out[l], not last)
        x_lat = x_lat + gt_m * y_lat
        f_lat = modulate(rms_norm(x_lat, g_ffn[l]), sh_f, sc_f)
        if last:
            x_lat = x_lat + gt_f * moe_ffn(f_lat, w_router[l], b_router[l], w_exp_in[l], b_exp_in[l],
                                           w_exp_out[l], b_exp_out[l])
        else:
            x_ctx = x_ctx + mod_c[2] * y_ctx
            f_ctx = modulate(rms_norm(x_ctx, g_ffn[l]), mod_c[3], mod_c[4])
            n_ctx = x_ctx.shape[1]
            f_all = moe_ffn(jnp.concatenate([f_ctx, f_lat], axis=1), w_router[l], b_router[l], w_exp_in[l],
                            b_exp_in[l], w_exp_out[l], b_exp_out[l])
            x_ctx = x_ctx + mod_c[5] * f_all[:, :n_ctx]
            x_lat = x_lat + gt_f * f_all[:, n_ctx:]
    return rms_norm(x_lat, g_final)
```

```python
import functools

import numpy as np
import jax
import jax.numpy as jnp
from jax import lax
from jax.experimental import pallas as pl
from jax.experimental.pallas import tpu as pltpu

F32 = jnp.float32
BF16 = jnp.bfloat16
U32 = jnp.uint32
I32 = jnp.int32

GRID_W = 64
HEAD_DIM = 64
NA_HEADS = 8
NA_WIN_ROWS = 8
NA_WIN_COLS = 16
SC_WIDTH = 1024
SC_CONV_WIDTH = 3
WA_Q_HEADS = 8
WA_KV_HEADS = 2
WA_GROUP = WA_Q_HEADS // WA_KV_HEADS
WA_WINDOW = 128
ROPE_BASE = 10000.0
N_BRANCHES = 3
TOP_K = 4
SWIGLU_LIMIT = 7.0
SWIGLU_ALPHA = 1.702
N_ADA = 6
RMS_EPS = 1e-6
NEG_INF = -1e30

NA_DIM = NA_HEADS * HEAD_DIM
WA_Q_DIM = WA_Q_HEADS * HEAD_DIM
WA_KV_DIM = WA_KV_HEADS * HEAD_DIM

LANES = 128
VMEM_BUDGET = 56 * 1024 * 1024

OFF_SC_B = 0
OFF_SC_C = OFF_SC_B + SC_WIDTH
OFF_SC_H = OFF_SC_C + SC_WIDTH
OFF_WA_Q = OFF_SC_H + SC_WIDTH
OFF_NA_Q = OFF_WA_Q + WA_Q_DIM
OFF_GATES = OFF_NA_Q + NA_DIM
KV_NA_K = 0
KV_NA_V = KV_NA_K + NA_DIM
KV_WA_K = KV_NA_V + NA_DIM
KV_WA_V = KV_WA_K + WA_KV_DIM
N_KV_COLS = KV_WA_V + WA_KV_DIM

NA_QROWS = 4
NA_SLAB_ROWS = NA_QROWS + NA_WIN_ROWS - 1
NA_TQ = NA_QROWS * GRID_W
NA_TK = NA_SLAB_ROWS * GRID_W
WA_TQ = 256
WA_TK = WA_TQ + 2 * WA_WINDOW

MOE_BLOCK = 256


def _params(semantics, vmem_bytes):
    limit = int(min(max(vmem_bytes * 5 // 4 + (4 << 20), 32 << 20), VMEM_BUDGET))
    return pltpu.CompilerParams(dimension_semantics=semantics, vmem_limit_bytes=limit)


def _pick(n, candidates):
    for c in candidates:
        if n % c == 0:
            return c
    return n


def _dot(a, b):
    return jnp.dot(a, b, preferred_element_type=F32)


def _dot_nt(a, b):
    return lax.dot_general(a, b, (((1,), (1,)), ((), ())), preferred_element_type=F32)


def _rms_mod(x, g, shift, scale):
    y = x * lax.rsqrt(jnp.mean(x * x, axis=-1, keepdims=True) + RMS_EPS)
    return (y * g) * (1.0 + scale) + shift


def _ada_kernel(c_ref, w_ref, b_ref, o_ref):
    c = c_ref[...]
    s = c * jax.nn.sigmoid(c)
    o_ref[0] = jnp.dot(s, w_ref[0], preferred_element_type=F32,
                       precision=lax.Precision.HIGHEST) + b_ref[0]


def _ada_call(cvec, w_ada, b_ada):
    depth, d, n = w_ada.shape
    tn = _pick(n, (1024, 768, 512, 256, 128))
    return pl.pallas_call(
        _ada_kernel,
        grid=(depth, n // tn),
        in_specs=[pl.BlockSpec((8, d), lambda l, j: (0, 0)),
                  pl.BlockSpec((1, d, tn), lambda l, j: (l, 0, j)),
                  pl.BlockSpec((1, 1, tn), lambda l, j: (l, 0, j))],
        out_specs=pl.BlockSpec((1, 8, tn), lambda l, j: (l, 0, j)),
        out_shape=jax.ShapeDtypeStruct((depth, 8, n), F32),
        compiler_params=_params(("arbitrary", "arbitrary"), 2 * d * tn * 4),
        name="ada",
    )(cvec, w_ada, b_ada.reshape(depth, 1, n))


def _rope128(x, cos, sin):
    lane = lax.broadcasted_iota(I32, x.shape, 1)
    fwd = pltpu.roll(x, LANES - HEAD_DIM // 4, 1)
    bwd = pltpu.roll(x, HEAD_DIM // 4, 1)
    partner = jnp.where((lane & (HEAD_DIM // 4)) == 0, fwd, bwd)
    return x * cos + partner * sin


def _proj_kernel(*refs, rope_tiles):
    if rope_tiles:
        x_ref, g_ref, sh_ref, sc_ref, cos_ref, sin_ref, w_ref, o_ref, h_ref = refs
    else:
        x_ref, g_ref, sh_ref, sc_ref, w_ref, o_ref, h_ref = refs
    j = pl.program_id(2)

    @pl.when(j == 0)
    def _():
        h_ref[...] = _rms_mod(x_ref[0], g_ref[...], sh_ref[0], sc_ref[0]).astype(BF16)

    acc = _dot(h_ref[...], w_ref[...])
    o_ref[0] = acc.astype(BF16)
    for jt, lo, width in rope_tiles:
        @pl.when(j == jt)
        def _(lo=lo, width=width):
            cos, sin = cos_ref[...], sin_ref[...]
            for c0 in range(lo, lo + width, LANES):
                o_ref[0, :, c0:c0 + LANES] = _rope128(acc[:, c0:c0 + LANES], cos, sin).astype(BF16)


def _proj_call(x, g, shift, scale, w, rope, tm, tn, rope_cols):
    b, n, d = x.shape
    nc = w.shape[1]
    rope_tiles = []
    if rope is not None:
        for off, width in rope_cols:
            assert off // tn == (off + width - 1) // tn and off % LANES == 0 and width % LANES == 0
            rope_tiles.append((off // tn, off % tn, width))
    in_specs = [pl.BlockSpec((1, tm, d), lambda bb, i, j: (bb, i, 0)),
                pl.BlockSpec((1, d), lambda bb, i, j: (0, 0)),
                pl.BlockSpec((1, 1, d), lambda bb, i, j: (bb, 0, 0)),
                pl.BlockSpec((1, 1, d), lambda bb, i, j: (bb, 0, 0))]
    args = [x, g.reshape(1, d), shift, scale]
    if rope is not None:
        in_specs += [pl.BlockSpec((tm, LANES), lambda bb, i, j: (i, 0))] * 2
        args += list(rope)
    in_specs.append(pl.BlockSpec((d, tn), lambda bb, i, j: (0, j)))
    args.append(w)
    vmem = 2 * tm * d * 4 + tm * d * 2 + 2 * d * tn * 2 + 2 * tm * tn * 2 + tm * tn * 4 + 4 * tm * LANES * 4
    return pl.pallas_call(
        functools.partial(_proj_kernel, rope_tiles=tuple(rope_tiles)),
        grid=(b, n // tm, nc // tn),
        in_specs=in_specs,
        out_specs=pl.BlockSpec((1, tm, tn), lambda bb, i, j: (bb, i, j)),
        out_shape=jax.ShapeDtypeStruct((b, n, nc), BF16),
        scratch_shapes=[pltpu.VMEM((tm, d), BF16)],
        compiler_params=_params(("arbitrary", "arbitrary", "arbitrary"), vmem),
        name="proj",
    )(*args)


def _softmax_av(parts, extra=None):
    m = functools.reduce(jnp.maximum, [jnp.max(s, axis=-1, keepdims=True) for s, _ in parts])
    if extra is not None:
        m = jnp.maximum(m, extra)
    l = 0.0 if extra is None else jnp.exp(extra - m)
    o = None
    for s, v in parts:
        p = jnp.exp(s - m)
        l = l + jnp.sum(p, axis=-1, keepdims=True)
        pv = _dot(p.astype(BF16), v)
        o = pv if o is None else o + pv
    return o / l


def _na_kernel(q_ref, k_ref, v_ref, kc_ref, vc_ref, bias_ref, o_ref, *, rows):
    i = pl.program_id(2)
    r_start = jnp.clip(NA_QROWS * i - NA_WIN_ROWS // 2, 0, rows - NA_SLAB_ROWS)
    start = pl.multiple_of(r_start * GRID_W, GRID_W)
    kslab = k_ref[0, pl.ds(start, NA_TK), :]
    vslab = v_ref[0, pl.ds(start, NA_TK), :]
    q, kc, vc = q_ref[0], kc_ref[0], vc_ref[0]
    outs = []
    for hh in range(LANES // HEAD_DIM):
        sl = slice(hh * HEAD_DIM, (hh + 1) * HEAD_DIM)
        qh = q[:, sl]
        s_nb = _dot_nt(qh, kslab[:, sl]) + bias_ref[0, hh]
        s_ctx = _dot_nt(qh, kc[:, sl])
        outs.append(_softmax_av([(s_nb, vslab[:, sl]), (s_ctx, vc[:, sl])]))
    o_ref[0] = jnp.concatenate(outs, axis=1).astype(BF16)


def _na_bias_tables(rpb, rows):
    nblk = rows // NA_QROWS
    assert nblk >= 3 and rows >= NA_SLAB_ROWS
    ql = np.arange(NA_TQ)
    kl = np.arange(NA_TK)
    ros, cos_, valids = [], [], []
    for i in (0, 1, nblk - 1):
        rs = int(np.clip(NA_QROWS * i - NA_WIN_ROWS // 2, 0, rows - NA_SLAB_ROWS))
        r = NA_QROWS * i + ql // GRID_W
        c = ql % GRID_W
        r0 = np.clip(r - NA_WIN_ROWS // 2, 0, rows - NA_WIN_ROWS)
        c0 = np.clip(c - NA_WIN_COLS // 2, 0, GRID_W - NA_WIN_COLS)
        kr = rs + kl // GRID_W
        kc = kl % GRID_W
        ok = ((kr[None, :] >= r0[:, None]) & (kr[None, :] < r0[:, None] + NA_WIN_ROWS)
              & (kc[None, :] >= c0[:, None]) & (kc[None, :] < c0[:, None] + NA_WIN_COLS))
        ro = np.where(ok, kr[None, :] - r[:, None] + NA_WIN_ROWS - 1, 0)
        co = np.where(ok, kc[None, :] - c[:, None] + NA_WIN_COLS - 1, 0)
        ros.append(ro), cos_.append(co), valids.append(ok)
    ro, co, ok = np.stack(ros), np.stack(cos_), np.stack(valids)
    vals = rpb.astype(F32)[:, ro, co]
    return jnp.where(ok[None], vals, NEG_INF).transpose(1, 0, 2, 3)


def _na_call(p_lat, p_ctx, bias, kv_off_lat, kv_off_ctx):
    b, s, _ = p_lat.shape
    l = p_ctx.shape[1]
    rows = s // GRID_W
    nblk = rows // NA_QROWS
    hp = NA_DIM // LANES
    qb, kb, vb = OFF_NA_Q // LANES, (kv_off_lat + KV_NA_K) // LANES, (kv_off_lat + KV_NA_V) // LANES
    kcb, vcb = (kv_off_ctx + KV_NA_K) // LANES, (kv_off_ctx + KV_NA_V) // LANES

    def case(i):
        return jnp.where(i == 0, 0, jnp.where(i == nblk - 1, 2, 1))

    vmem = 4 * s * LANES * 2 + 4 * NA_TQ * NA_TK * 4 + 8 * NA_TQ * (NA_TK + l) * 4
    return pl.pallas_call(
        functools.partial(_na_kernel, rows=rows),
        grid=(b, hp, nblk),
        in_specs=[pl.BlockSpec((1, NA_TQ, LANES), lambda bb, h, i: (bb, i, qb + h)),
                  pl.BlockSpec((1, s, LANES), lambda bb, h, i: (bb, 0, kb + h)),
                  pl.BlockSpec((1, s, LANES), lambda bb, h, i: (bb, 0, vb + h)),
                  pl.BlockSpec((1, l, LANES), lambda bb, h, i: (bb, 0, kcb + h)),
                  pl.BlockSpec((1, l, LANES), lambda bb, h, i: (bb, 0, vcb + h)),
                  pl.BlockSpec((1, LANES // HEAD_DIM, NA_TQ, NA_TK), lambda bb, h, i: (case(i), h, 0, 0))],
        out_specs=pl.BlockSpec((1, NA_TQ, LANES), lambda bb, h, i: (bb, i, h)),
        out_shape=jax.ShapeDtypeStruct((b, s, NA_DIM), BF16),
        compiler_params=_params(("arbitrary", "arbitrary", "arbitrary"), vmem),
        name="natten",
    )(p_lat, p_lat, p_lat, p_ctx, p_ctx, bias)


def _wa_kernel(sink_ref, q_ref, k_ref, v_ref, kc_ref, vc_ref, o_ref, *, seq):
    i = pl.program_id(1)
    start = pl.multiple_of(jnp.clip(WA_TQ * i - WA_WINDOW, 0, seq - WA_TK), WA_WINDOW)
    kslab = k_ref[0, pl.ds(start, WA_TK), :]
    vslab = v_ref[0, pl.ds(start, WA_TK), :]
    qpos = WA_TQ * i + lax.broadcasted_iota(I32, (WA_TQ, WA_TK), 0)
    kpos = start + lax.broadcasted_iota(I32, (WA_TQ, WA_TK), 1)
    band = jnp.where(jnp.abs(kpos - qpos) <= WA_WINDOW, 0.0, NEG_INF).astype(F32)
    q, kc, vc = q_ref[0], kc_ref[0], vc_ref[0]
    outs = []
    for hq in range(WA_Q_HEADS):
        hk = hq // WA_GROUP
        sk = slice(hk * HEAD_DIM, (hk + 1) * HEAD_DIM)
        qh = q[:, hq * HEAD_DIM:(hq + 1) * HEAD_DIM]
        s_loc = _dot_nt(qh, kslab[:, sk]) + band
        s_ctx = _dot_nt(qh, kc[:, sk])
        outs.append(_softmax_av([(s_loc, vslab[:, sk]), (s_ctx, vc[:, sk])], extra=sink_ref[hq]))
    o_ref[0] = jnp.concatenate(outs, axis=1).astype(BF16)


def _wa_call(p_lat, p_ctx, sinks, kv_off_lat, kv_off_ctx):
    b, s, _ = p_lat.shape
    l = p_ctx.shape[1]
    assert s % WA_TQ == 0 and s >= WA_TK
    qb = OFF_WA_Q // WA_Q_DIM
    kb, vb = (kv_off_lat + KV_WA_K) // LANES, (kv_off_lat + KV_WA_V) // LANES
    kcb, vcb = (kv_off_ctx + KV_WA_K) // LANES, (kv_off_ctx + KV_WA_V) // LANES
    vmem = 4 * s * LANES * 2 + 4 * WA_TQ * WA_Q_DIM * 2 + 10 * WA_TQ * (WA_TK + l) * 4
    return pl.pallas_call(
        functools.partial(_wa_kernel, seq=s),
        grid=(b, s // WA_TQ),
        in_specs=[pl.BlockSpec(memory_space=pltpu.SMEM),
                  pl.BlockSpec((1, WA_TQ, WA_Q_DIM), lambda bb, i: (bb, i, qb)),
                  pl.BlockSpec((1, s, LANES), lambda bb, i: (bb, 0, kb)),
                  pl.BlockSpec((1, s, LANES), lambda bb, i: (bb, 0, vb)),
                  pl.BlockSpec((1, l, LANES), lambda bb, i: (bb, 0, kcb)),
                  pl.BlockSpec((1, l, LANES), lambda bb, i: (bb, 0, vcb))],
        out_specs=pl.BlockSpec((1, WA_TQ, WA_Q_DIM), lambda bb, i: (bb, i, 0)),
        out_shape=jax.ShapeDtypeStruct((b, s, WA_Q_DIM), BF16),
        compiler_params=_params(("arbitrary", "arbitrary"), vmem),
        name="winattn",
    )(sinks, p_lat, p_lat, p_lat, p_ctx, p_ctx)


def _ctx_attn_kernel(sink_ref, naq_ref, nak_ref, nav_ref, waq_ref, wak_ref, wav_ref, ona_ref, owa_ref):
    q, k, v = naq_ref[0], nak_ref[0], nav_ref[0]
    outs = []
    for h in range(NA_HEADS):
        sl = slice(h * HEAD_DIM, (h + 1) * HEAD_DIM)
        outs.append(_softmax_av([(_dot_nt(q[:, sl], k[:, sl]), v[:, sl])]))
    ona_ref[0] = jnp.concatenate(outs, axis=1).astype(BF16)
    q, k, v = waq_ref[0], wak_ref[0], wav_ref[0]
    outs = []
    for hq in range(WA_Q_HEADS):
        sk = slice((hq // WA_GROUP) * HEAD_DIM, (hq // WA_GROUP + 1) * HEAD_DIM)
        qh = q[:, hq * HEAD_DIM:(hq + 1) * HEAD_DIM]
        outs.append(_softmax_av([(_dot_nt(qh, k[:, sk]), v[:, sk])], extra=sink_ref[hq]))
    owa_ref[0] = jnp.concatenate(outs, axis=1).astype(BF16)


def _ctx_attn_call(p_ctx, sinks, kv_off):
    b, l, _ = p_ctx.shape

    def spec(width, off):
        return pl.BlockSpec((1, l, width), lambda bb: (bb, 0, off // width))

    return pl.pallas_call(
        _ctx_attn_kernel,
        grid=(b,),
        in_specs=[pl.BlockSpec(memory_space=pltpu.SMEM),
                  spec(NA_DIM, OFF_NA_Q), spec(NA_DIM, kv_off + KV_NA_K), spec(NA_DIM, kv_off + KV_NA_V),
                  spec(WA_Q_DIM, OFF_WA_Q), spec(WA_KV_DIM, kv_off + KV_WA_K), spec(WA_KV_DIM, kv_off + KV_WA_V)],
        out_specs=[pl.BlockSpec((1, l, NA_DIM), lambda bb: (bb, 0, 0)),
                   pl.BlockSpec((1, l, WA_Q_DIM), lambda bb: (bb, 0, 0))],
        out_shape=[jax.ShapeDtypeStruct((b, l, NA_DIM), BF16), jax.ShapeDtypeStruct((b, l, WA_Q_DIM), BF16)],
        compiler_params=_params(("arbitrary",), 16 * l * NA_DIM * 4),
        name="ctxattn",
    )(sinks, p_ctx, p_ctx, p_ctx, p_ctx, p_ctx, p_ctx)


def _pack_bf16_pairs(x):
    n = x.shape[1] // 2
    lo = pltpu.bitcast(x[:, :n].astype(BF16).astype(F32), U32)
    hi = pltpu.bitcast(x[:, n:].astype(BF16).astype(F32), U32)
    return (hi & jnp.uint32(0xFFFF0000)) | (lo >> 16)


def _unpack_bf16_pairs(p):
    lo = pltpu.bitcast(p << 16, F32)
    hi = pltpu.bitcast(p & jnp.uint32(0xFFFF0000), F32)
    return jnp.concatenate([lo, hi], axis=1)


def _top_k_lanes(logits, n_exp):
    m_rows = logits.shape[0]
    col = lax.broadcasted_iota(I32, (m_rows, n_exp), 1).astype(F32)
    lane = lax.broadcasted_iota(I32, (m_rows, LANES), 1)
    work = logits
    vals = jnp.zeros((m_rows, LANES), F32)
    idxs = jnp.zeros((m_rows, LANES), F32)
    top = None
    denom = 0.0
    for k in range(TOP_K):
        mx = jnp.max(work, axis=-1, keepdims=True)
        ix = jnp.min(jnp.where(work == mx, col, float(n_exp)), axis=-1, keepdims=True)
        work = jnp.where(col == ix, -jnp.inf, work)
        top = mx if top is None else top
        e = jnp.exp(mx - top)
        denom = denom + e
        vals = jnp.where(lane == k, e, vals)
        idxs = jnp.where(lane == k, ix, idxs)
    return vals / denom, idxs.astype(I32)


def _merge_kernel(*refs, n_exp, has_halo):
    (x_ref, ona_ref, owa_ref, scb_ref, scc_ref, sch_ref) = refs[:6]
    refs = refs[6:]
    if has_halo:
        cprev_ref, hprev_ref, cnext_ref, hnext_ref = refs[:4]
        refs = refs[4:]
    (gna_ref, gsc_ref, gwa_ref, conv_ref, wna_ref, wsc_ref, wwa_ref, wout_ref, gt_ref, gffn_ref, shf_ref, scf_ref,
     wr_ref, br_ref, xo_ref, f_ref, te_ref, tg_ref, osc_ref, acc_ref) = refs
    i, n = pl.program_id(1), pl.program_id(2)
    tm = x_ref.shape[1]

    @pl.when(n == 0)
    def _():
        u = scc_ref[0].astype(F32) * sch_ref[0].astype(F32)
        zero = jnp.zeros((1, SC_WIDTH), F32)
        if has_halo:
            u_prev = jnp.where(i == 0, zero, cprev_ref[0, 7:8, :].astype(F32) * hprev_ref[0, 7:8, :].astype(F32))
            u_next = jnp.where(i == pl.num_programs(1) - 1, zero,
                               cnext_ref[0, 0:1, :].astype(F32) * hnext_ref[0, 0:1, :].astype(F32))
        else:
            u_prev = u_next = zero
        row = lax.broadcasted_iota(I32, u.shape, 0)
        u_m1 = jnp.where(row == 0, u_prev, pltpu.roll(u, 1, 0))
        u_p1 = jnp.where(row == tm - 1, u_next, pltpu.roll(u, tm - 1, 0))
        y = u_m1 * conv_ref[0:1, :] + u * conv_ref[1:2, :] + u_p1 * conv_ref[2:3, :]
        osc_ref[...] = (scb_ref[0].astype(F32) * y).astype(BF16)
        acc_ref[...] = jnp.zeros_like(acc_ref)

    def gate(ref):
        return jax.nn.sigmoid(ref[0].astype(F32))

    merged = (gate(gna_ref) * _dot(ona_ref[0], wna_ref[...])
              + gate(gsc_ref) * _dot(osc_ref[...], wsc_ref[...])
              + gate(gwa_ref) * _dot(owa_ref[0], wwa_ref[...]))
    acc_ref[...] += _dot(merged.astype(BF16), wout_ref[...])

    @pl.when(n == pl.num_programs(2) - 1)
    def _():
        xn = x_ref[0] + gt_ref[0] * acc_ref[...]
        xo_ref[0] = xn
        f = _rms_mod(xn, gffn_ref[...], shf_ref[0], scf_ref[0])
        f_ref[0] = _pack_bf16_pairs(f)
        f_hi = f.astype(BF16)
        f_lo = (f - f_hi.astype(F32)).astype(BF16)
        wr = wr_ref[...]
        logits = (_dot(f_hi, wr[:, :n_exp]) + _dot(f_lo, wr[:, :n_exp]) + _dot(f_hi, wr[:, n_exp:])) + br_ref[...]
        gates, idx = _top_k_lanes(logits, n_exp)
        tg_ref[0] = gates
        te_ref[0] = idx


def _merge_call(x, o_na, o_wa, p, conv_w, w_na, w_sc, w_wa, w_out, gt, g_ffn, sh_f, sc_f, w_r2, b_r, tm, tn):
    b, n_tok, d = x.shape
    n_exp = b_r.shape[-1]
    nt = n_tok // tm
    has_halo = nt > 1
    gate_b = OFF_GATES // tn
    dn = d // tn
    assert OFF_GATES % tn == 0 and d % tn == 0 and n_tok % tm == 0 and tm % 8 == 0
    rb = tm // 8

    def rows(width, off):
        return pl.BlockSpec((1, tm, width), lambda bb, i, n: (bb, i, off // width))

    in_specs = [pl.BlockSpec((1, tm, d), lambda bb, i, n: (bb, i, 0)),
                pl.BlockSpec((1, tm, NA_DIM), lambda bb, i, n: (bb, i, 0)),
                pl.BlockSpec((1, tm, WA_Q_DIM), lambda bb, i, n: (bb, i, 0)),
                rows(SC_WIDTH, OFF_SC_B), rows(SC_WIDTH, OFF_SC_C), rows(SC_WIDTH, OFF_SC_H)]
    args = [x, o_na, o_wa, p, p, p]
    if has_halo:
        last8 = n_tok // 8 - 1
        for off in (OFF_SC_C, OFF_SC_H):
            in_specs.append(pl.BlockSpec((1, 8, SC_WIDTH),
                                         lambda bb, i, n, off=off: (bb, jnp.maximum(i * rb - 1, 0), off // SC_WIDTH)))
        for off in (OFF_SC_C, OFF_SC_H):
            in_specs.append(pl.BlockSpec((1, 8, SC_WIDTH),
                                         lambda bb, i, n, off=off: (bb, jnp.minimum((i + 1) * rb, last8), off // SC_WIDTH)))
        args += [p, p, p, p]
    for br in range(N_BRANCHES):
        in_specs.append(pl.BlockSpec((1, tm, tn), lambda bb, i, n, br=br: (bb, i, gate_b + br * dn + n)))
        args.append(p)
    in_specs += [pl.BlockSpec((SC_CONV_WIDTH, SC_WIDTH), lambda bb, i, n: (0, 0)),
                 pl.BlockSpec((NA_DIM, tn), lambda bb, i, n: (0, n)),
                 pl.BlockSpec((SC_WIDTH, tn), lambda bb, i, n: (0, n)),
                 pl.BlockSpec((WA_Q_DIM, tn), lambda bb, i, n: (0, n)),
                 pl.BlockSpec((tn, d), lambda bb, i, n: (n, 0)),
                 pl.BlockSpec((1, 1, d), lambda bb, i, n: (bb, 0, 0)),
                 pl.BlockSpec((1, d), lambda bb, i, n: (0, 0)),
                 pl.BlockSpec((1, 1, d), lambda bb, i, n: (bb, 0, 0)),
                 pl.BlockSpec((1, 1, d), lambda bb, i, n: (bb, 0, 0)),
                 pl.BlockSpec((d, 2 * n_exp), lambda bb, i, n: (0, 0)),
                 pl.BlockSpec((1, n_exp), lambda bb, i, n: (0, 0))]
    args += [conv_w, w_na, w_sc, w_wa, w_out, gt, g_ffn.reshape(1, d), sh_f, sc_f, w_r2, b_r.reshape(1, n_exp)]
    out_specs = [pl.BlockSpec((1, tm, d), lambda bb, i, n: (bb, i, 0)),
                 pl.BlockSpec((1, tm, d // 2), lambda bb, i, n: (bb, i, 0)),
                 pl.BlockSpec((1, tm, LANES), lambda bb, i, n: (bb, i, 0)),
                 pl.BlockSpec((1, tm, LANES), lambda bb, i, n: (bb, i, 0))]
    out_shape = [jax.ShapeDtypeStruct((b, n_tok, d), F32), jax.ShapeDtypeStruct((b, n_tok, d // 2), U32),
                 jax.ShapeDtypeStruct((b, n_tok, LANES), I32), jax.ShapeDtypeStruct((b, n_tok, LANES), F32)]
    vmem = (4 * tm * d * 4 + 2 * tm * (d // 2) * 4 + 4 * tm * LANES * 4 + 2 * tm * (NA_DIM + WA_Q_DIM + 3 * SC_WIDTH) * 2
            + 6 * tm * tn * 2 + 2 * (NA_DIM + SC_WIDTH + WA_Q_DIM + d) * tn * 2 + tm * SC_WIDTH * 2 + tm * d * 4
            + 4 * tm * d * 4)
    return pl.pallas_call(
        functools.partial(_merge_kernel, n_exp=n_exp, has_halo=has_halo),
        grid=(b, nt, dn),
        in_specs=in_specs,
        out_specs=out_specs,
        out_shape=out_shape,
        scratch_shapes=[pltpu.VMEM((tm, SC_WIDTH), BF16), pltpu.VMEM((tm, d), F32)],
        compiler_params=_params(("arbitrary", "arbitrary", "arbitrary"), vmem),
        name="merge",
    )(*args)


def _moe_kernel(be_ref, nused_ref, tok_cur_ref, tok_nxt_ref, f_hbm, w1_ref, b1_ref, w2_ref, b2_ref, y_ref,
                xbuf, sem, *, d_exp):
    i = pl.program_id(0)
    n_used = nused_ref[0]
    slot = i % 2

    def issue(tok_ref, dst_slot):
        def body(r, carry):
            t = tok_ref[0, 0, r]
            pltpu.make_async_copy(f_hbm.at[pl.ds(t, 1), :], xbuf.at[dst_slot, pl.ds(r, 1), :], sem.at[dst_slot]).start()
            return carry
        lax.fori_loop(0, MOE_BLOCK, body, 0)

    @pl.when(jnp.logical_and(i == 0, n_used > 0))
    def _():
        issue(tok_cur_ref, 0)

    @pl.when(i + 1 < n_used)
    def _():
        issue(tok_nxt_ref, 1 - slot)

    @pl.when(i < n_used)
    def _():
        pltpu.make_async_copy(f_hbm.at[pl.ds(0, MOE_BLOCK), :], xbuf.at[slot], sem.at[slot]).wait()
        x = _unpack_bf16_pairs(xbuf[slot]).astype(BF16)
        u = _dot(x, w1_ref[0]) + b1_ref[0]
        glu = jnp.minimum(u[:, :d_exp], SWIGLU_LIMIT)
        lin = jnp.clip(u[:, d_exp:], -SWIGLU_LIMIT, SWIGLU_LIMIT)
        act = glu * jax.nn.sigmoid(SWIGLU_ALPHA * glu) * (lin + 1.0)
        y = _dot(act.astype(BF16), w2_ref[0]) + b2_ref[0]
        y_ref[...] = _pack_bf16_pairs(y)

    @pl.when(i >= n_used)
    def _():
        y_ref[...] = jnp.zeros_like(y_ref)


def _moe_call(block_e, n_used, slot_tok, f_all, w1, b1, w2, b2):
    n_blocks = block_e.shape[0]
    n_exp, d, two_de = w1.shape
    d_exp = two_de // 2
    grid_spec = pltpu.PrefetchScalarGridSpec(
        num_scalar_prefetch=2,
        grid=(n_blocks,),
        in_specs=[pl.BlockSpec((1, 1, MOE_BLOCK), lambda i, be, nu: (i, 0, 0), memory_space=pltpu.SMEM),
                  pl.BlockSpec((1, 1, MOE_BLOCK), lambda i, be, nu: (jnp.minimum(i + 1, n_blocks - 1), 0, 0),
                               memory_space=pltpu.SMEM),
                  pl.BlockSpec(memory_space=pl.ANY),
                  pl.BlockSpec((1, d, two_de), lambda i, be, nu: (be[i], 0, 0)),
                  pl.BlockSpec((1, 1, two_de), lambda i, be, nu: (be[i], 0, 0)),
                  pl.BlockSpec((1, d_exp, d), lambda i, be, nu: (be[i], 0, 0)),
                  pl.BlockSpec((1, 1, d), lambda i, be, nu: (be[i], 0, 0))],
        out_specs=pl.BlockSpec((MOE_BLOCK, d // 2), lambda i, be, nu: (i, 0)),
        scratch_shapes=[pltpu.VMEM((2, MOE_BLOCK, d // 2), U32), pltpu.SemaphoreType.DMA((2,))],
    )
    vmem = 2 * (d * two_de + d_exp * d) * 2 + 4 * MOE_BLOCK * (d // 2) * 4 + 6 * MOE_BLOCK * (d + two_de) * 4
    slot_tok3 = slot_tok.reshape(n_blocks, 1, MOE_BLOCK)
    return pl.pallas_call(
        functools.partial(_moe_kernel, d_exp=d_exp),
        grid_spec=grid_spec,
        out_shape=jax.ShapeDtypeStruct((n_blocks * MOE_BLOCK, d // 2), U32),
        compiler_params=_params(("arbitrary",), vmem),
        name="experts",
    )(block_e, n_used, slot_tok3, slot_tok3, f_all, w1, b1.reshape(n_exp, 1, two_de), w2, b2.reshape(n_exp, 1, d))


def _combine_kernel(slot_ref, x_ref, gate_ref, gt_ref, gfin_ref, y_hbm, o_ref, ybuf, sem, *, final):
    tm = x_ref.shape[1]

    def body(j, carry):
        s = slot_ref[0, 0, j]
        pltpu.make_async_copy(y_hbm.at[pl.ds(s, 1), :], ybuf.at[pl.ds(j, 1), :], sem.at[0]).start()
        return carry
    lax.fori_loop(0, TOP_K * tm, body, 0)
    pltpu.make_async_copy(y_hbm.at[pl.ds(0, TOP_K * tm), :], ybuf, sem.at[0]).wait()
    gates = gate_ref[0]
    acc = None
    for k in range(TOP_K):
        yk = _unpack_bf16_pairs(ybuf[k * tm:(k + 1) * tm, :]) * gates[:, k:k + 1]
        acc = yk if acc is None else acc + yk
    xn = x_ref[0] + gt_ref[0] * acc
    if final:
        xn = xn * lax.rsqrt(jnp.mean(xn * xn, axis=-1, keepdims=True) + RMS_EPS) * gfin_ref[...]
    o_ref[0] = xn


def _combine_call(x, slots, gates, gt, g_final, y_sorted, tm, final):
    b, n_tok, d = x.shape
    nt = n_tok // tm
    slots_t = slots.reshape(b, nt, tm, TOP_K).transpose(0, 1, 3, 2).reshape(b * nt, 1, TOP_K * tm)
    vmem = 4 * tm * d * 4 + TOP_K * tm * (d // 2) * 4 + 2 * tm * LANES * 4 + 4 * tm * d * 4
    return pl.pallas_call(
        functools.partial(_combine_kernel, final=final),
        grid=(b, nt),
        in_specs=[pl.BlockSpec((1, 1, TOP_K * tm), lambda bb, i: (bb * nt + i, 0, 0), memory_space=pltpu.SMEM),
                  pl.BlockSpec((1, tm, d), lambda bb, i: (bb, i, 0)),
                  pl.BlockSpec((1, tm, LANES), lambda bb, i: (bb, i, 0)),
                  pl.BlockSpec((1, 1, d), lambda bb, i: (bb, 0, 0)),
                  pl.BlockSpec((1, d), lambda bb, i: (0, 0)),
                  pl.BlockSpec(memory_space=pl.ANY)],
        out_specs=pl.BlockSpec((1, tm, d), lambda bb, i: (bb, i, 0)),
        out_shape=jax.ShapeDtypeStruct((b, n_tok, d), F32),
        scratch_shapes=[pltpu.VMEM((TOP_K * tm, d // 2), U32), pltpu.SemaphoreType.DMA((1,))],
        compiler_params=_params(("arbitrary", "arbitrary"), vmem),
        name="combine",
    )(slots_t, x, gates, gt, g_final.reshape(1, d), y_sorted)


def _route(top_e, n_exp):
    n_tok = top_e.shape[0]
    n_asg = n_tok * TOP_K
    flat_e = top_e.reshape(n_asg)
    onehot = (flat_e[:, None] == jnp.arange(n_exp, dtype=I32)[None, :]).astype(I32)
    csum = jnp.cumsum(onehot, axis=0)
    rank = jnp.sum(csum * onehot, axis=1) - 1
    counts = csum[-1]
    padded = (counts + MOE_BLOCK - 1) // MOE_BLOCK * MOE_BLOCK
    pad_ends = jnp.cumsum(padded)
    pad_starts = pad_ends - padded
    slot = pad_starts[flat_e] + rank
    n_blocks = -(-(n_asg + n_exp * (MOE_BLOCK - 1)) // MOE_BLOCK)
    tok = jnp.arange(n_asg, dtype=I32) // TOP_K
    slot_tok = jnp.zeros((n_blocks * MOE_BLOCK,), I32).at[slot].set(tok)
    block_e = jnp.minimum(jnp.searchsorted(pad_ends, jnp.arange(n_blocks, dtype=I32) * MOE_BLOCK, side='right'),
                          n_exp - 1).astype(I32)
    n_used = (pad_ends[-1] // MOE_BLOCK).astype(I32).reshape(1)
    return slot.reshape(n_tok, TOP_K).astype(I32), slot_tok, block_e, n_used


def _rope_tables(seq):
    t = np.arange(seq)
    quarter = HEAD_DIM // 4
    inv = jnp.asarray(ROPE_BASE, F32) ** (-jnp.arange(quarter, dtype=F32) / quarter)
    ang_r = jnp.asarray(t // GRID_W, F32)[:, None] * inv[None, :]
    ang_c = jnp.asarray(t % GRID_W, F32)[:, None] * inv[None, :]
    cos = jnp.concatenate([jnp.cos(ang_r)] * 2 + [jnp.cos(ang_c)] * 2, axis=-1)
    sin = jnp.concatenate([-jnp.sin(ang_r), jnp.sin(ang_r), -jnp.sin(ang_c), jnp.sin(ang_c)], axis=-1)
    reps = LANES // HEAD_DIM
    return jnp.tile(cos, (1, reps)), jnp.tile(sin, (1, reps))


def kernel(x, c, ctx, c_ctx, w_ada, b_ada, g_mix, w_in, na_rpb, sc_conv, wa_sinks, w_na_out, w_sc_out, w_wa_out,
           w_out, g_ffn, w_router, b_router, w_exp_in, b_exp_in, w_exp_out, b_exp_out, g_final):
    b, s, d = x.shape
    l = ctx.shape[1]
    depth = w_ada.shape[0]
    n_exp = w_router.shape[-1]
    rows = s // GRID_W
    off_kv = OFF_GATES + N_BRANCHES * d
    n_cols = off_kv + N_KV_COLS
    assert w_in.shape[-1] == n_cols and s % NA_TQ == 0

    cvec = jnp.zeros((8, d), F32).at[:b].set(c).at[b].set(c_ctx)
    ada = _ada_call(cvec, w_ada, b_ada).reshape(depth, 8, N_ADA, d)
    rope = _rope_tables(s)
    q_scale = HEAD_DIM ** -0.5

    tm_lat = _pick(s, (1024, 512, 256))
    tn_proj = _pick(n_cols, (768, 512, 256))
    tn_kv = _pick(N_KV_COLS, (640, 256))
    tm_merge = _pick(s, (512, 256))
    tn_merge = _pick(d, (512, 256))
    tm_comb = _pick(s, (256,))
    rope_cols = ((OFF_WA_Q, WA_Q_DIM), (off_kv + KV_WA_K, WA_KV_DIM))

    x_lat, x_ctx = x, ctx
    for layer in range(depth):
        last = layer == depth - 1
        mod_lat = [ada[layer, :b, k][:, None, :] for k in range(N_ADA)]
        mod_ctx = [jnp.broadcast_to(ada[layer, b, k][None, None, :], (b, 1, d)) for k in range(N_ADA)]

        wl = w_in[layer]
        w_perm = jnp.concatenate(
            [wl[:, 2304:5376], wl[:, 1792:2304] * q_scale, wl[:, 1280:1792] * q_scale, wl[:, 5376:], wl[:, :1280]],
            axis=1).astype(BF16)
        p_lat = _proj_call(x_lat, g_mix[layer], mod_lat[0], mod_lat[1], w_perm, rope, tm_lat, tn_proj, rope_cols)
        if last:
            p_ctx = _proj_call(x_ctx, g_mix[layer], mod_ctx[0], mod_ctx[1], w_perm[:, off_kv:], None, l, tn_kv, ())
            ctx_kv = 0
        else:
            p_ctx = _proj_call(x_ctx, g_mix[layer], mod_ctx[0], mod_ctx[1], w_perm, None, l, tn_proj, ())
            ctx_kv = off_kv

        bias = _na_bias_tables(na_rpb[layer], rows)
        o_na = _na_call(p_lat, p_ctx, bias, off_kv, ctx_kv)
        o_wa = _wa_call(p_lat, p_ctx, wa_sinks[layer], off_kv, ctx_kv)

        wr = w_router[layer]
        wr_hi = wr.astype(BF16)
        wr_lo = (wr - wr_hi.astype(F32)).astype(BF16)
        w_r2 = jnp.concatenate([wr_hi, wr_lo], axis=1)
        merge_w = (sc_conv[layer], w_na_out[layer].astype(BF16), w_sc_out[layer].astype(BF16),
                   w_wa_out[layer].astype(BF16), w_out[layer].astype(BF16))
        x_lat, f_lat, te_lat, tg_lat = _merge_call(
            x_lat, o_na, o_wa, p_lat, *merge_w, mod_lat[2], g_ffn[layer], mod_lat[3], mod_lat[4],
            w_r2, b_router[layer], tm_merge, tn_merge)
        f_all = f_lat.reshape(b * s, d // 2)
        te_all = te_lat.reshape(b * s, LANES)[:, :TOP_K]
        if not last:
            o_na_c, o_wa_c = _ctx_attn_call(p_ctx, wa_sinks[layer], off_kv)
            x_ctx, f_ctx, te_ctx, tg_ctx = _merge_call(
                x_ctx, o_na_c, o_wa_c, p_ctx, *merge_w, mod_ctx[2], g_ffn[layer], mod_ctx[3], mod_ctx[4],
                w_r2, b_router[layer], l, tn_merge)
            f_all = jnp.concatenate([f_all, f_ctx.reshape(b * l, d // 2)], axis=0)
            te_all = jnp.concatenate([te_all, te_ctx.reshape(b * l, LANES)[:, :TOP_K]], axis=0)

        slots, slot_tok, block_e, n_used = _route(te_all, n_exp)
        y_sorted = _moe_call(block_e, n_used, slot_tok, f_all, w_exp_in[layer].astype(BF16), b_exp_in[layer],
                             w_exp_out[layer].astype(BF16), b_exp_out[layer])
        x_lat = _combine_call(x_lat, slots[:b * s].reshape(b, s, TOP_K), tg_lat, mod_lat[5], g_final, y_sorted,
                              tm_comb, last)
        if not last:
            x_ctx = _combine_call(x_ctx, slots[b * s:].reshape(b, l, TOP_K), tg_ctx, mod_ctx[5], g_final, y_sorted,
                                  l, False)
    return x_lat
```

```python
import functools

import numpy as np
import jax
import jax.numpy as jnp
from jax import lax
from jax.experimental import pallas as pl
from jax.experimental.pallas import tpu as pltpu

F32 = jnp.float32
BF16 = jnp.bfloat16
U32 = jnp.uint32
I32 = jnp.int32

GRID_W = 64
HEAD_DIM = 64
NA_HEADS = 8
NA_WIN_ROWS = 8
NA_WIN_COLS = 16
SC_WIDTH = 1024
SC_CONV_WIDTH = 3
WA_Q_HEADS = 8
WA_KV_HEADS = 2
WA_GROUP = WA_Q_HEADS // WA_KV_HEADS
WA_WINDOW = 128
ROPE_BASE = 10000.0
N_BRANCHES = 3
TOP_K = 4
SWIGLU_LIMIT = 7.0
SWIGLU_ALPHA = 1.702
N_ADA = 6
RMS_EPS = 1e-6
NEG_INF = -1e30

NA_DIM = NA_HEADS * HEAD_DIM
WA_Q_DIM = WA_Q_HEADS * HEAD_DIM
WA_KV_DIM = WA_KV_HEADS * HEAD_DIM

LANES = 128
VMEM_BUDGET = 56 * 1024 * 1024

REF_OFF_NA_Q = 2 * NA_DIM + 2 * WA_KV_DIM
REF_OFF_WA_Q = REF_OFF_NA_Q + NA_DIM
REF_OFF_SC = REF_OFF_WA_Q + WA_Q_DIM
REF_OFF_GATES = REF_OFF_SC + 3 * SC_WIDTH

OFF_SC_B = 0
OFF_SC_C = OFF_SC_B + SC_WIDTH
OFF_SC_H = OFF_SC_C + SC_WIDTH
OFF_WA_Q = OFF_SC_H + SC_WIDTH
OFF_NA_Q = OFF_WA_Q + WA_Q_DIM
OFF_GATES = OFF_NA_Q + NA_DIM
KV_NA_K = 0
KV_NA_V = KV_NA_K + NA_DIM
KV_WA_K = KV_NA_V + NA_DIM
KV_WA_V = KV_WA_K + WA_KV_DIM
N_KV_COLS = KV_WA_V + WA_KV_DIM

NA_QROWS = 4
NA_SLAB_ROWS = NA_QROWS + NA_WIN_ROWS - 1
NA_TQ = NA_QROWS * GRID_W
NA_TK = NA_SLAB_ROWS * GRID_W
WA_TQ = 256
WA_TK = WA_TQ + 2 * WA_WINDOW

MOE_BLOCK = 256


def _params(semantics, vmem_bytes):
    limit = int(min(max(vmem_bytes * 5 // 4 + (4 << 20), 32 << 20), VMEM_BUDGET))
    return pltpu.CompilerParams(dimension_semantics=semantics, vmem_limit_bytes=limit)


def _pick(n, candidates):
    for c in candidates:
        if n % c == 0:
            return c
    return n


def _dot(a, b):
    return jnp.dot(a, b, preferred_element_type=F32)


def _dot_nt(a, b):
    return lax.dot_general(a, b, (((1,), (1,)), ((), ())), preferred_element_type=F32)


def _rms_mod(x, g, shift, scale):
    y = x * lax.rsqrt(jnp.mean(x * x, axis=-1, keepdims=True) + RMS_EPS)
    return (y * g) * (1.0 + scale) + shift


def _ada_kernel(c_ref, w_ref, b_ref, o_ref):
    c = c_ref[...]
    s = c * jax.nn.sigmoid(c)
    o_ref[0] = jnp.dot(s, w_ref[0], preferred_element_type=F32,
                       precision=lax.Precision.HIGHEST) + b_ref[0]


def _ada_call(cvec, w_ada, b_ada):
    depth, d, n = w_ada.shape
    tn = _pick(n, (1024, 768, 512, 256, 128))
    return pl.pallas_call(
        _ada_kernel,
        grid=(depth, n // tn),
        in_specs=[pl.BlockSpec((8, d), lambda l, j: (0, 0)),
                  pl.BlockSpec((1, d, tn), lambda l, j: (l, 0, j)),
                  pl.BlockSpec((1, 1, tn), lambda l, j: (l, 0, j))],
        out_specs=pl.BlockSpec((1, 8, tn), lambda l, j: (l, 0, j)),
        out_shape=jax.ShapeDtypeStruct((depth, 8, n), F32),
        compiler_params=_params(("arbitrary", "arbitrary"), 2 * d * tn * 4),
        name="ada",
    )(cvec, w_ada, b_ada.reshape(depth, 1, n))


def _rope128(x, cos, sin):
    lane = lax.broadcasted_iota(I32, x.shape, 1)
    fwd = pltpu.roll(x, LANES - HEAD_DIM // 4, 1)
    bwd = pltpu.roll(x, HEAD_DIM // 4, 1)
    partner = jnp.where((lane & (HEAD_DIM // 4)) == 0, fwd, bwd)
    return x * cos + partner * sin


def _proj_kernel(*refs, rope_tiles):
    if rope_tiles:
        x_ref, g_ref, sh_ref, sc_ref, cos_ref, sin_ref, w_ref, o_ref, h_ref = refs
    else:
        x_ref, g_ref, sh_ref, sc_ref, w_ref, o_ref, h_ref = refs
    j = pl.program_id(2)

    @pl.when(j == 0)
    def _():
        h_ref[...] = _rms_mod(x_ref[0], g_ref[0], sh_ref[0], sc_ref[0]).astype(BF16)

    acc = _dot(h_ref[...], w_ref[0])
    o_ref[0] = acc.astype(BF16)
    for jt, lo, width in rope_tiles:
        @pl.when(j == jt)
        def _(lo=lo, width=width):
            cos, sin = cos_ref[...], sin_ref[...]
            for c0 in range(lo, lo + width, LANES):
                o_ref[0, :, c0:c0 + LANES] = _rope128(acc[:, c0:c0 + LANES], cos, sin).astype(BF16)


def _proj_call(x, g, shift, scale, w_all, layer, col0, nc, rope, tm, tn, rope_cols):
    b, n, d = x.shape
    assert col0 % tn == 0 and nc % tn == 0 and n % tm == 0
    jb = col0 // tn
    rope_tiles = []
    if rope is not None:
        for off, width in rope_cols:
            assert off // tn == (off + width - 1) // tn and off % LANES == 0 and width % LANES == 0
            rope_tiles.append((off // tn, off % tn, width))
    in_specs = [pl.BlockSpec((1, tm, d), lambda bb, i, j: (bb, i, 0)),
                pl.BlockSpec((1, 1, d), lambda bb, i, j: (layer, 0, 0)),
                pl.BlockSpec((1, 1, d), lambda bb, i, j: (bb, 0, 0)),
                pl.BlockSpec((1, 1, d), lambda bb, i, j: (bb, 0, 0))]
    args = [x, g.reshape(g.shape[0], 1, d), shift, scale]
    if rope is not None:
        in_specs += [pl.BlockSpec((tm, LANES), lambda bb, i, j: (i, 0))] * 2
        args += list(rope)
    in_specs.append(pl.BlockSpec((1, d, tn), lambda bb, i, j: (layer, 0, jb + j)))
    args.append(w_all)
    vmem = 2 * tm * d * 4 + tm * d * 2 + 2 * d * tn * 2 + 2 * tm * tn * 2 + tm * tn * 4 + 4 * tm * LANES * 4
    return pl.pallas_call(
        functools.partial(_proj_kernel, rope_tiles=tuple(rope_tiles)),
        grid=(b, n // tm, nc // tn),
        in_specs=in_specs,
        out_specs=pl.BlockSpec((1, tm, tn), lambda bb, i, j: (bb, i, j)),
        out_shape=jax.ShapeDtypeStruct((b, n, nc), BF16),
        scratch_shapes=[pltpu.VMEM((tm, d), BF16)],
        compiler_params=_params(("arbitrary", "arbitrary", "arbitrary"), vmem),
        name="proj",
    )(*args)


def _softmax_av(parts, extra=None):
    m = functools.reduce(jnp.maximum, [jnp.max(s, axis=-1, keepdims=True) for s, _ in parts])
    if extra is not None:
        m = jnp.maximum(m, extra)
    l = 0.0 if extra is None else jnp.exp(extra - m)
    o = None
    for s, v in parts:
        p = jnp.exp(s - m)
        l = l + jnp.sum(p, axis=-1, keepdims=True)
        pv = _dot(p.astype(BF16), v)
        o = pv if o is None else o + pv
    return o / l


def _na_kernel(q_ref, k_ref, v_ref, kc_ref, vc_ref, bias_ref, o_ref, *, rows):
    i = pl.program_id(2)
    r_start = jnp.clip(NA_QROWS * i - NA_WIN_ROWS // 2, 0, rows - NA_SLAB_ROWS)
    start = pl.multiple_of(r_start * GRID_W, GRID_W)
    kslab = k_ref[0, pl.ds(start, NA_TK), :]
    vslab = v_ref[0, pl.ds(start, NA_TK), :]
    q, kc, vc = q_ref[0], kc_ref[0], vc_ref[0]
    outs = []
    for hh in range(LANES // HEAD_DIM):
        sl = slice(hh * HEAD_DIM, (hh + 1) * HEAD_DIM)
        qh = q[:, sl]
        s_nb = _dot_nt(qh, kslab[:, sl]) + bias_ref[0, hh]
        s_ctx = _dot_nt(qh, kc[:, sl])
        outs.append(_softmax_av([(s_nb, vslab[:, sl]), (s_ctx, vc[:, sl])]))
    o_ref[0] = jnp.concatenate(outs, axis=1).astype(BF16)


def _na_bias_tables(rpb, rows):
    nblk = rows // NA_QROWS
    assert nblk >= 3 and rows >= NA_SLAB_ROWS
    n_heads, n_dr, n_dc = rpb.shape
    half = NA_WIN_COLS - 1
    vec = jnp.zeros((n_heads, n_dr, LANES), F32)
    vec = vec.at[..., :n_dc - half].set(rpb[..., half:]).at[..., LANES - half:].set(rpb[..., :half])
    toep = jnp.tile(vec, (1, 1, GRID_W))[..., :GRID_W * (LANES - 1)]
    toep = toep.reshape(n_heads, n_dr, GRID_W, LANES - 1)[..., :GRID_W]
    col = np.arange(GRID_W)
    c0 = np.clip(col - NA_WIN_COLS // 2, 0, GRID_W - NA_WIN_COLS)
    col_ok = (col[None, :] >= c0[:, None]) & (col[None, :] < c0[:, None] + NA_WIN_COLS)
    toep = jnp.where(col_ok[None, None], toep, NEG_INF)
    masked = jnp.full((n_heads, GRID_W, GRID_W), NEG_INF, F32)
    cases = []
    for i in (0, 1, nblk - 1):
        rs = int(np.clip(NA_QROWS * i - NA_WIN_ROWS // 2, 0, rows - NA_SLAB_ROWS))
        q_rows = []
        for rl in range(NA_QROWS):
            r = NA_QROWS * i + rl
            r0 = int(np.clip(r - NA_WIN_ROWS // 2, 0, rows - NA_WIN_ROWS))
            tiles = []
            for kl in range(NA_SLAB_ROWS):
                kr = rs + kl
                tiles.append(toep[:, kr - r + NA_WIN_ROWS - 1] if r0 <= kr < r0 + NA_WIN_ROWS else masked)
            q_rows.append(jnp.concatenate(tiles, axis=2))
        cases.append(jnp.concatenate(q_rows, axis=1))
    return jnp.stack(cases)


def _na_call(p_lat, p_ctx, bias, kv_off_lat, kv_off_ctx):
    b, s, _ = p_lat.shape
    l = p_ctx.shape[1]
    rows = s // GRID_W
    nblk = rows // NA_QROWS
    hp = NA_DIM // LANES
    qb, kb, vb = OFF_NA_Q // LANES, (kv_off_lat + KV_NA_K) // LANES, (kv_off_lat + KV_NA_V) // LANES
    kcb, vcb = (kv_off_ctx + KV_NA_K) // LANES, (kv_off_ctx + KV_NA_V) // LANES

    def case(i):
        return jnp.where(i == 0, 0, jnp.where(i == nblk - 1, 2, 1))

    vmem = 4 * s * LANES * 2 + 4 * NA_TQ * NA_TK * 4 + 8 * NA_TQ * (NA_TK + l) * 4
    return pl.pallas_call(
        functools.partial(_na_kernel, rows=rows),
        grid=(b, hp, nblk),
        in_specs=[pl.BlockSpec((1, NA_TQ, LANES), lambda bb, h, i: (bb, i, qb + h)),
                  pl.BlockSpec((1, s, LANES), lambda bb, h, i: (bb, 0, kb + h)),
                  pl.BlockSpec((1, s, LANES), lambda bb, h, i: (bb, 0, vb + h)),
                  pl.BlockSpec((1, l, LANES), lambda bb, h, i: (bb, 0, kcb + h)),
                  pl.BlockSpec((1, l, LANES), lambda bb, h, i: (bb, 0, vcb + h)),
                  pl.BlockSpec((1, LANES // HEAD_DIM, NA_TQ, NA_TK), lambda bb, h, i: (case(i), h, 0, 0))],
        out_specs=pl.BlockSpec((1, NA_TQ, LANES), lambda bb, h, i: (bb, i, h)),
        out_shape=jax.ShapeDtypeStruct((b, s, NA_DIM), BF16),
        compiler_params=_params(("arbitrary", "arbitrary", "arbitrary"), vmem),
        name="natten",
    )(p_lat, p_lat, p_lat, p_ctx, p_ctx, bias)


def _wa_kernel(sink_ref, q_ref, k_ref, v_ref, kc_ref, vc_ref, o_ref, *, seq):
    i = pl.program_id(1)
    start = pl.multiple_of(jnp.clip(WA_TQ * i - WA_WINDOW, 0, seq - WA_TK), WA_WINDOW)
    kslab = k_ref[0, pl.ds(start, WA_TK), :]
    vslab = v_ref[0, pl.ds(start, WA_TK), :]
    qpos = WA_TQ * i + lax.broadcasted_iota(I32, (WA_TQ, WA_TK), 0)
    kpos = start + lax.broadcasted_iota(I32, (WA_TQ, WA_TK), 1)
    band = jnp.where(jnp.abs(kpos - qpos) <= WA_WINDOW, 0.0, NEG_INF).astype(F32)
    q, kc, vc = q_ref[0], kc_ref[0], vc_ref[0]
    outs = []
    for hq in range(WA_Q_HEADS):
        hk = hq // WA_GROUP
        sk = slice(hk * HEAD_DIM, (hk + 1) * HEAD_DIM)
        qh = q[:, hq * HEAD_DIM:(hq + 1) * HEAD_DIM]
        s_loc = _dot_nt(qh, kslab[:, sk]) + band
        s_ctx = _dot_nt(qh, kc[:, sk])
        outs.append(_softmax_av([(s_loc, vslab[:, sk]), (s_ctx, vc[:, sk])], extra=sink_ref[hq]))
    o_ref[0] = jnp.concatenate(outs, axis=1).astype(BF16)


def _wa_call(p_lat, p_ctx, sinks, kv_off_lat, kv_off_ctx):
    b, s, _ = p_lat.shape
    l = p_ctx.shape[1]
    assert s % WA_TQ == 0 and s >= WA_TK
    qb = OFF_WA_Q // WA_Q_DIM
    kb, vb = (kv_off_lat + KV_WA_K) // LANES, (kv_off_lat + KV_WA_V) // LANES
    kcb, vcb = (kv_off_ctx + KV_WA_K) // LANES, (kv_off_ctx + KV_WA_V) // LANES
    vmem = 4 * s * LANES * 2 + 4 * WA_TQ * WA_Q_DIM * 2 + 10 * WA_TQ * (WA_TK + l) * 4
    return pl.pallas_call(
        functools.partial(_wa_kernel, seq=s),
        grid=(b, s // WA_TQ),
        in_specs=[pl.BlockSpec(memory_space=pltpu.SMEM),
                  pl.BlockSpec((1, WA_TQ, WA_Q_DIM), lambda bb, i: (bb, i, qb)),
                  pl.BlockSpec((1, s, LANES), lambda bb, i: (bb, 0, kb)),
                  pl.BlockSpec((1, s, LANES), lambda bb, i: (bb, 0, vb)),
                  pl.BlockSpec((1, l, LANES), lambda bb, i: (bb, 0, kcb)),
                  pl.BlockSpec((1, l, LANES), lambda bb, i: (bb, 0, vcb))],
        out_specs=pl.BlockSpec((1, WA_TQ, WA_Q_DIM), lambda bb, i: (bb, i, 0)),
        out_shape=jax.ShapeDtypeStruct((b, s, WA_Q_DIM), BF16),
        compiler_params=_params(("arbitrary", "arbitrary"), vmem),
        name="winattn",
    )(sinks, p_lat, p_lat, p_lat, p_ctx, p_ctx)


def _ctx_attn_kernel(sink_ref, naq_ref, nak_ref, nav_ref, waq_ref, wak_ref, wav_ref, ona_ref, owa_ref):
    q, k, v = naq_ref[0], nak_ref[0], nav_ref[0]
    outs = []
    for h in range(NA_HEADS):
        sl = slice(h * HEAD_DIM, (h + 1) * HEAD_DIM)
        outs.append(_softmax_av([(_dot_nt(q[:, sl], k[:, sl]), v[:, sl])]))
    ona_ref[0] = jnp.concatenate(outs, axis=1).astype(BF16)
    q, k, v = waq_ref[0], wak_ref[0], wav_ref[0]
    outs = []
    for hq in range(WA_Q_HEADS):
        sk = slice((hq // WA_GROUP) * HEAD_DIM, (hq // WA_GROUP + 1) * HEAD_DIM)
        qh = q[:, hq * HEAD_DIM:(hq + 1) * HEAD_DIM]
        outs.append(_softmax_av([(_dot_nt(qh, k[:, sk]), v[:, sk])], extra=sink_ref[hq]))
    owa_ref[0] = jnp.concatenate(outs, axis=1).astype(BF16)


def _ctx_attn_call(p_ctx, sinks, kv_off):
    b, l, _ = p_ctx.shape

    def spec(width, off):
        return pl.BlockSpec((1, l, width), lambda bb: (bb, 0, off // width))

    return pl.pallas_call(
        _ctx_attn_kernel,
        grid=(b,),
        in_specs=[pl.BlockSpec(memory_space=pltpu.SMEM),
                  spec(NA_DIM, OFF_NA_Q), spec(NA_DIM, kv_off + KV_NA_K), spec(NA_DIM, kv_off + KV_NA_V),
                  spec(WA_Q_DIM, OFF_WA_Q), spec(WA_KV_DIM, kv_off + KV_WA_K), spec(WA_KV_DIM, kv_off + KV_WA_V)],
        out_specs=[pl.BlockSpec((1, l, NA_DIM), lambda bb: (bb, 0, 0)),
                   pl.BlockSpec((1, l, WA_Q_DIM), lambda bb: (bb, 0, 0))],
        out_shape=[jax.ShapeDtypeStruct((b, l, NA_DIM), BF16), jax.ShapeDtypeStruct((b, l, WA_Q_DIM), BF16)],
        compiler_params=_params(("arbitrary",), 16 * l * NA_DIM * 4),
        name="ctxattn",
    )(sinks, p_ctx, p_ctx, p_ctx, p_ctx, p_ctx, p_ctx)


def _pack_bf16_pairs(x):
    n = x.shape[1] // 2
    lo = pltpu.bitcast(x[:, :n].astype(BF16).astype(F32), U32)
    hi = pltpu.bitcast(x[:, n:].astype(BF16).astype(F32), U32)
    return (hi & jnp.uint32(0xFFFF0000)) | (lo >> 16)


def _store_token_tiles(ref, packed):
    m, n = packed.shape
    sub = n // LANES
    for s in range(sub):
        ref[pl.ds(s, m, stride=sub), :] = packed[:, s * LANES:(s + 1) * LANES]


def _load_token_tiles(ref, m, dtype):
    sub = ref.shape[0] // m
    chunks = [ref[pl.ds(s, m, stride=sub), :] for s in range(sub)]
    lo = [pltpu.bitcast(p << 16, F32).astype(dtype) for p in chunks]
    hi = [pltpu.bitcast(p & jnp.uint32(0xFFFF0000), F32).astype(dtype) for p in chunks]
    return jnp.concatenate(lo + hi, axis=1)


def _top_k_lanes(logits, n_exp):
    m_rows = logits.shape[0]
    col = lax.broadcasted_iota(I32, (m_rows, n_exp), 1).astype(F32)
    lane = lax.broadcasted_iota(I32, (m_rows, LANES), 1)
    work = logits
    vals = jnp.zeros((m_rows, LANES), F32)
    idxs = jnp.zeros((m_rows, LANES), F32)
    top = None
    denom = 0.0
    for k in range(TOP_K):
        mx = jnp.max(work, axis=-1, keepdims=True)
        ix = jnp.min(jnp.where(work == mx, col, float(n_exp)), axis=-1, keepdims=True)
        work = jnp.where(col == ix, -jnp.inf, work)
        top = mx if top is None else top
        e = jnp.exp(mx - top)
        denom = denom + e
        vals = jnp.where(lane == k, e, vals)
        idxs = jnp.where(lane == k, ix, idxs)
    return vals / denom, idxs.astype(I32)


def _merge_kernel(*refs, n_exp, has_halo):
    (x_ref, ona_ref, owa_ref, scb_ref, scc_ref, sch_ref) = refs[:6]
    refs = refs[6:]
    if has_halo:
        cprev_ref, hprev_ref, cnext_ref, hnext_ref = refs[:4]
        refs = refs[4:]
    (gna_ref, gsc_ref, gwa_ref, conv_ref, wna_ref, wsc_ref, wwa_ref, wout_ref, gt_ref, gffn_ref, shf_ref, scf_ref,
     wr_ref, br_ref, xo_ref, f_ref, te_ref, tg_ref, osc_ref, acc_ref) = refs
    i, n = pl.program_id(1), pl.program_id(2)
    tm = x_ref.shape[1]

    @pl.when(n == 0)
    def _():
        u = scc_ref[0].astype(F32) * sch_ref[0].astype(F32)
        zero = jnp.zeros((1, SC_WIDTH), F32)
        if has_halo:
            u_prev = jnp.where(i == 0, zero, cprev_ref[0, 7:8, :].astype(F32) * hprev_ref[0, 7:8, :].astype(F32))
            u_next = jnp.where(i == pl.num_programs(1) - 1, zero,
                               cnext_ref[0, 0:1, :].astype(F32) * hnext_ref[0, 0:1, :].astype(F32))
        else:
            u_prev = u_next = zero
        row = lax.broadcasted_iota(I32, u.shape, 0)
        u_m1 = jnp.where(row == 0, u_prev, pltpu.roll(u, 1, 0))
        u_p1 = jnp.where(row == tm - 1, u_next, pltpu.roll(u, tm - 1, 0))
        y = u_m1 * conv_ref[0, 0:1, :] + u * conv_ref[0, 1:2, :] + u_p1 * conv_ref[0, 2:3, :]
        osc_ref[...] = (scb_ref[0].astype(F32) * y).astype(BF16)
        acc_ref[...] = jnp.zeros_like(acc_ref)

    def gate(ref):
        return jax.nn.sigmoid(ref[0].astype(F32))

    merged = (gate(gna_ref) * _dot(ona_ref[0], wna_ref[0])
              + gate(gsc_ref) * _dot(osc_ref[...], wsc_ref[0])
              + gate(gwa_ref) * _dot(owa_ref[0], wwa_ref[0]))
    acc_ref[...] += _dot(merged.astype(BF16), wout_ref[0])

    @pl.when(n == pl.num_programs(2) - 1)
    def _():
        xn = x_ref[0] + gt_ref[0] * acc_ref[...]
        xo_ref[0] = xn
        f = _rms_mod(xn, gffn_ref[0], shf_ref[0], scf_ref[0])
        _store_token_tiles(f_ref.at[0], _pack_bf16_pairs(f))
        f_hi = f.astype(BF16)
        f_lo = (f - f_hi.astype(F32)).astype(BF16)
        wr = wr_ref[0]
        logits = (_dot(f_hi, wr[:, :n_exp]) + _dot(f_lo, wr[:, :n_exp]) + _dot(f_hi, wr[:, n_exp:])) + br_ref[0]
        gates, idx = _top_k_lanes(logits, n_exp)
        tg_ref[0] = gates
        te_ref[0] = idx


def _merge_call(x, o_na, o_wa, p, layer, conv_w, w_na, w_sc, w_wa, w_out, gt, g_ffn, sh_f, sc_f, w_r2, b_r, tm, tn):
    b, n_tok, d = x.shape
    depth = w_out.shape[0]
    n_exp = b_r.shape[-1]
    sub = d // 2 // LANES
    nt = n_tok // tm
    has_halo = nt > 1
    gate_b = OFF_GATES // tn
    dn = d // tn
    assert OFF_GATES % tn == 0 and d % tn == 0 and n_tok % tm == 0 and tm % 8 == 0
    rb = tm // 8

    def rows(width, off):
        return pl.BlockSpec((1, tm, width), lambda bb, i, n: (bb, i, off // width))

    in_specs = [pl.BlockSpec((1, tm, d), lambda bb, i, n: (bb, i, 0)),
                pl.BlockSpec((1, tm, NA_DIM), lambda bb, i, n: (bb, i, 0)),
                pl.BlockSpec((1, tm, WA_Q_DIM), lambda bb, i, n: (bb, i, 0)),
                rows(SC_WIDTH, OFF_SC_B), rows(SC_WIDTH, OFF_SC_C), rows(SC_WIDTH, OFF_SC_H)]
    args = [x, o_na, o_wa, p, p, p]
    if has_halo:
        last8 = n_tok // 8 - 1
        for off in (OFF_SC_C, OFF_SC_H):
            in_specs.append(pl.BlockSpec((1, 8, SC_WIDTH),
                                         lambda bb, i, n, off=off: (bb, jnp.maximum(i * rb - 1, 0), off // SC_WIDTH)))
        for off in (OFF_SC_C, OFF_SC_H):
            in_specs.append(pl.BlockSpec((1, 8, SC_WIDTH),
                                         lambda bb, i, n, off=off: (bb, jnp.minimum((i + 1) * rb, last8), off // SC_WIDTH)))
        args += [p, p, p, p]
    for br in range(N_BRANCHES):
        in_specs.append(pl.BlockSpec((1, tm, tn), lambda bb, i, n, br=br: (bb, i, gate_b + br * dn + n)))
        args.append(p)
    in_specs += [pl.BlockSpec((1, SC_CONV_WIDTH, SC_WIDTH), lambda bb, i, n: (layer, 0, 0)),
                 pl.BlockSpec((1, NA_DIM, tn), lambda bb, i, n: (layer, 0, n)),
                 pl.BlockSpec((1, SC_WIDTH, tn), lambda bb, i, n: (layer, 0, n)),
                 pl.BlockSpec((1, WA_Q_DIM, tn), lambda bb, i, n: (layer, 0, n)),
                 pl.BlockSpec((1, tn, d), lambda bb, i, n: (layer, n, 0)),
                 pl.BlockSpec((1, 1, d), lambda bb, i, n: (bb, 0, 0)),
                 pl.BlockSpec((1, 1, d), lambda bb, i, n: (layer, 0, 0)),
                 pl.BlockSpec((1, 1, d), lambda bb, i, n: (bb, 0, 0)),
                 pl.BlockSpec((1, 1, d), lambda bb, i, n: (bb, 0, 0)),
                 pl.BlockSpec((1, d, 2 * n_exp), lambda bb, i, n: (layer, 0, 0)),
                 pl.BlockSpec((1, 1, n_exp), lambda bb, i, n: (layer, 0, 0))]
    args += [conv_w, w_na, w_sc, w_wa, w_out, gt, g_ffn.reshape(depth, 1, d), sh_f, sc_f, w_r2,
             b_r.reshape(depth, 1, n_exp)]
    out_specs = [pl.BlockSpec((1, tm, d), lambda bb, i, n: (bb, i, 0)),
                 pl.BlockSpec((1, tm * sub, LANES), lambda bb, i, n: (bb, i, 0)),
                 pl.BlockSpec((1, tm, LANES), lambda bb, i, n: (bb, i, 0)),
                 pl.BlockSpec((1, tm, LANES), lambda bb, i, n: (bb, i, 0))]
    out_shape = [jax.ShapeDtypeStruct((b, n_tok, d), F32), jax.ShapeDtypeStruct((b, n_tok * sub, LANES), U32),
                 jax.ShapeDtypeStruct((b, n_tok, LANES), I32), jax.ShapeDtypeStruct((b, n_tok, LANES), F32)]
    vmem = (4 * tm * d * 4 + 2 * tm * (d // 2) * 4 + 4 * tm * LANES * 4 + 2 * tm * (NA_DIM + WA_Q_DIM + 3 * SC_WIDTH) * 2
            + 6 * tm * tn * 2 + 2 * (NA_DIM + SC_WIDTH + WA_Q_DIM + d) * tn * 2 + tm * SC_WIDTH * 2 + tm * d * 4
            + 4 * tm * d * 4)
    return pl.pallas_call(
        functools.partial(_merge_kernel, n_exp=n_exp, has_halo=has_halo),
        grid=(b, nt, dn),
        in_specs=in_specs,
        out_specs=out_specs,
        out_shape=out_shape,
        scratch_shapes=[pltpu.VMEM((tm, SC_WIDTH), BF16), pltpu.VMEM((tm, d), F32)],
        compiler_params=_params(("arbitrary", "arbitrary", "arbitrary"), vmem),
        name="merge",
    )(*args)


def _moe_kernel(be_ref, nused_ref, tok_cur_ref, tok_nxt_ref, dst_prv_ref, dst_cur_ref, f_hbm,
                w1_ref, b1_ref, w2_ref, b2_ref, y_hbm, xbuf, ybuf, w1b, w2b, gsem, ssem, *, d_exp, sub):
    i = pl.program_id(0)
    n_used = nused_ref[0]
    slot = i % 2
    other = 1 - slot
    rows = MOE_BLOCK * sub

    def gather(tok_ref, r, dst_slot):
        src = pl.multiple_of(tok_ref[0, 0, r] * sub, sub)
        return pltpu.make_async_copy(f_hbm.at[pl.ds(src, sub), :], xbuf.at[dst_slot, pl.ds(r * sub, sub), :],
                                     gsem.at[dst_slot])

    def scatter(dst_ref, r, src_slot):
        dst = pl.multiple_of(dst_ref[0, 0, r] * sub, sub)
        return pltpu.make_async_copy(ybuf.at[src_slot, pl.ds(r * sub, sub), :], y_hbm.at[pl.ds(dst, sub), :],
                                     ssem.at[src_slot])

    def wait_gather(s):
        pltpu.make_async_copy(f_hbm.at[pl.ds(0, rows), :], xbuf.at[s], gsem.at[s]).wait()

    def wait_scatter(s):
        pltpu.make_async_copy(ybuf.at[s], y_hbm.at[pl.ds(0, rows), :], ssem.at[s]).wait()

    @pl.when(i < n_used)
    def _():
        @pl.when(i == 0)
        def _():
            def body(r, carry):
                gather(tok_cur_ref, r, 0).start()
                return carry
            lax.fori_loop(0, MOE_BLOCK, body, 0)
            ybuf[1] = jnp.zeros(ybuf.shape[1:], U32)

        @pl.when(i > 0)
        def _():
            wait_scatter(slot)

        @pl.when(jnp.logical_or(i == 0, be_ref[i] != be_ref[jnp.maximum(i - 1, 0)]))
        def _():
            w1b[...] = w1_ref[0, 0].astype(BF16)
            w2b[...] = w2_ref[0, 0].astype(BF16)

        wait_gather(slot)
        x = _load_token_tiles(xbuf.at[slot], MOE_BLOCK, BF16)
        n_chunks = 2
        cw = d_exp // n_chunks
        per = MOE_BLOCK // n_chunks
        y = None
        for c in range(n_chunks):
            for r in range(c * per, (c + 1) * per):
                gather(tok_nxt_ref, r, other).start()
                scatter(dst_prv_ref, r, other).start()
            glu = _dot(x, w1b[:, c * cw:(c + 1) * cw]) + b1_ref[0, 0, :, c * cw:(c + 1) * cw]
            lin = _dot(x, w1b[:, d_exp + c * cw:d_exp + (c + 1) * cw]) + b1_ref[0, 0, :, d_exp + c * cw:d_exp + (c + 1) * cw]
            glu = jnp.minimum(glu, SWIGLU_LIMIT)
            lin = jnp.clip(lin, -SWIGLU_LIMIT, SWIGLU_LIMIT)
            act = glu * jax.nn.sigmoid(SWIGLU_ALPHA * glu) * (lin + 1.0)
            yc = _dot(act.astype(BF16), w2b[c * cw:(c + 1) * cw, :])
            y = yc if y is None else y + yc
        _store_token_tiles(ybuf.at[slot], _pack_bf16_pairs(y + b2_ref[0, 0]))

        @pl.when(i == n_used - 1)
        def _():
            def body(r, carry):
                scatter(dst_cur_ref, r, slot).start()
                return carry
            lax.fori_loop(0, MOE_BLOCK, body, 0)
            wait_gather(other)
            wait_scatter(other)
            wait_scatter(slot)

    @pl.when(i >= n_used)
    def _():
        ybuf[0] = jnp.zeros(ybuf.shape[1:], U32)

        def body(r, carry):
            scatter(dst_cur_ref, r, 0).start()
            return carry
        lax.fori_loop(0, MOE_BLOCK, body, 0)
        wait_scatter(0)


def _moe_call(layer, block_e, n_used, slot_tok, slot_dst, f_all, w1, b1, w2, b2, n_out_rows):
    n_blocks = block_e.shape[0]
    depth, n_exp, d, two_de = w1.shape
    d_exp = two_de // 2
    sub = d // 2 // LANES
    assert d_exp % (2 * LANES) == 0
    rows = MOE_BLOCK * sub

    def smem(index):
        return pl.BlockSpec((1, 1, MOE_BLOCK), index, memory_space=pltpu.SMEM)

    grid_spec = pltpu.PrefetchScalarGridSpec(
        num_scalar_prefetch=2,
        grid=(n_blocks,),
        in_specs=[smem(lambda i, be, nu: (i, 0, 0)),
                  smem(lambda i, be, nu: (jnp.minimum(i + 1, n_blocks - 1), 0, 0)),
                  smem(lambda i, be, nu: (jnp.where(i == 0, n_blocks, i - 1), 0, 0)),
                  smem(lambda i, be, nu: (i, 0, 0)),
                  pl.BlockSpec(memory_space=pl.ANY),
                  pl.BlockSpec((1, 1, d, two_de), lambda i, be, nu: (layer, be[i], 0, 0)),
                  pl.BlockSpec((1, 1, 1, two_de), lambda i, be, nu: (layer, be[i], 0, 0)),
                  pl.BlockSpec((1, 1, d_exp, d), lambda i, be, nu: (layer, be[i], 0, 0)),
                  pl.BlockSpec((1, 1, 1, d), lambda i, be, nu: (layer, be[i], 0, 0))],
        out_specs=pl.BlockSpec(memory_space=pl.ANY),
        scratch_shapes=[pltpu.VMEM((2, rows, LANES), U32), pltpu.VMEM((2, rows, LANES), U32),
                        pltpu.VMEM((d, two_de), BF16), pltpu.VMEM((d_exp, d), BF16),
                        pltpu.SemaphoreType.DMA((2,)), pltpu.SemaphoreType.DMA((2,))],
    )
    vmem = (2 * (d * two_de + d_exp * d) * 4 + (d * two_de + d_exp * d) * 2 + 4 * rows * LANES * 4
            + 6 * MOE_BLOCK * (d + two_de) * 4)
    tok3 = slot_tok.reshape(n_blocks, 1, MOE_BLOCK)
    dst3 = slot_dst.reshape(n_blocks + 1, 1, MOE_BLOCK)
    return pl.pallas_call(
        functools.partial(_moe_kernel, d_exp=d_exp, sub=sub),
        grid_spec=grid_spec,
        out_shape=jax.ShapeDtypeStruct((n_out_rows * sub, LANES), U32),
        compiler_params=pltpu.CompilerParams(dimension_semantics=("arbitrary",),
                                             vmem_limit_bytes=int(min(vmem + (4 << 20), VMEM_BUDGET)),
                                             disable_bounds_checks=True),
        name="experts",
    )(block_e, n_used, tok3, tok3, dst3, dst3, f_all, w1, b1.reshape(depth, n_exp, 1, two_de), w2,
      b2.reshape(depth, n_exp, 1, d))


def _combine_kernel(x_ref, gate_ref, gt_ref, gfin_ref, *refs, final):
    y_refs, o_ref = refs[:TOP_K], refs[TOP_K]
    tm = x_ref.shape[1]
    gates = gate_ref[0]
    acc = None
    for k in range(TOP_K):
        yk = _load_token_tiles(y_refs[k], tm, F32) * gates[:, k:k + 1]
        acc = yk if acc is None else acc + yk
    xn = x_ref[0] + gt_ref[0] * acc
    if final:
        xn = xn * lax.rsqrt(jnp.mean(xn * xn, axis=-1, keepdims=True) + RMS_EPS) * gfin_ref[...]
    o_ref[0] = xn


def _combine_call(x, gates, gt, g_final, y_tok, tok0, n_tok_all, tm, final):
    b, n_tok, d = x.shape
    nt = n_tok // tm
    sub = d // 2 // LANES
    assert tok0 % tm == 0 and n_tok_all % tm == 0 and n_tok % tm == 0
    plane = n_tok_all // tm
    vmem = 4 * tm * d * 4 + 2 * TOP_K * tm * (d // 2) * 4 + 2 * tm * LANES * 4 + 4 * tm * d * 4
    y_specs = [pl.BlockSpec((tm * sub, LANES), lambda bb, i, k=k: (k * plane + tok0 // tm + bb * nt + i, 0))
               for k in range(TOP_K)]
    return pl.pallas_call(
        functools.partial(_combine_kernel, final=final),
        grid=(b, nt),
        in_specs=[pl.BlockSpec((1, tm, d), lambda bb, i: (bb, i, 0)),
                  pl.BlockSpec((1, tm, LANES), lambda bb, i: (bb, i, 0)),
                  pl.BlockSpec((1, 1, d), lambda bb, i: (bb, 0, 0)),
                  pl.BlockSpec((1, d), lambda bb, i: (0, 0))] + y_specs,
        out_specs=pl.BlockSpec((1, tm, d), lambda bb, i: (bb, i, 0)),
        out_shape=jax.ShapeDtypeStruct((b, n_tok, d), F32),
        compiler_params=_params(("arbitrary", "arbitrary"), vmem),
        name="combine",
    )(x, gates, gt, g_final.reshape(1, d), *([y_tok] * TOP_K))


def _route(top_e, n_exp):
    n_tok = top_e.shape[0]
    n_asg = n_tok * TOP_K
    flat_e = top_e.reshape(n_asg)
    onehot = (flat_e[:, None] == jnp.arange(n_exp, dtype=I32)[None, :]).astype(I32)
    csum = jnp.cumsum(onehot, axis=0)
    rank = jnp.sum(csum * onehot, axis=1) - 1
    counts = csum[-1]
    padded = (counts + MOE_BLOCK - 1) // MOE_BLOCK * MOE_BLOCK
    pad_ends = jnp.cumsum(padded)
    pad_starts = pad_ends - padded
    slot = jnp.sum(jnp.where(onehot > 0, pad_starts[None, :], 0), axis=1) + rank
    n_blocks = -(-(n_asg + n_exp * (MOE_BLOCK - 1)) // MOE_BLOCK)
    n_slots = n_blocks * MOE_BLOCK
    asg = jnp.full((n_slots,), -1, I32).at[slot].set(jnp.arange(n_asg, dtype=I32))
    real = asg >= 0
    asg0 = jnp.maximum(asg, 0)
    slot_tok = asg0 // TOP_K
    pad_rank = jnp.cumsum(jnp.logical_not(real).astype(I32)) - 1
    slot_dst = jnp.where(real, (asg0 % TOP_K) * n_tok + asg0 // TOP_K, n_asg + pad_rank)
    slot_dst = jnp.concatenate([slot_dst, n_slots + jnp.arange(MOE_BLOCK, dtype=I32)])
    block_start = jnp.arange(n_blocks, dtype=I32) * MOE_BLOCK
    block_e = jnp.minimum(jnp.sum((pad_ends[None, :] <= block_start[:, None]).astype(I32), axis=1), n_exp - 1)
    n_used = (pad_ends[-1] // MOE_BLOCK).astype(I32).reshape(1)
    return slot_tok, slot_dst, block_e, n_used, n_slots + MOE_BLOCK


def _rope_tables(seq):
    t = np.arange(seq)
    quarter = HEAD_DIM // 4
    inv = jnp.asarray(ROPE_BASE, F32) ** (-jnp.arange(quarter, dtype=F32) / quarter)
    ang_r = jnp.asarray(t // GRID_W, F32)[:, None] * inv[None, :]
    ang_c = jnp.asarray(t % GRID_W, F32)[:, None] * inv[None, :]
    cos = jnp.concatenate([jnp.cos(ang_r)] * 2 + [jnp.cos(ang_c)] * 2, axis=-1)
    sin = jnp.concatenate([-jnp.sin(ang_r), jnp.sin(ang_r), -jnp.sin(ang_c), jnp.sin(ang_c)], axis=-1)
    reps = LANES // HEAD_DIM
    return jnp.tile(cos, (1, reps)), jnp.tile(sin, (1, reps))


def kernel(x, c, ctx, c_ctx, w_ada, b_ada, g_mix, w_in, na_rpb, sc_conv, wa_sinks, w_na_out, w_sc_out, w_wa_out,
           w_out, g_ffn, w_router, b_router, w_exp_in, b_exp_in, w_exp_out, b_exp_out, g_final):
    b, s, d = x.shape
    l = ctx.shape[1]
    depth = w_ada.shape[0]
    n_exp = w_router.shape[-1]
    rows = s // GRID_W
    off_kv = OFF_GATES + N_BRANCHES * d
    n_cols = off_kv + N_KV_COLS
    assert w_in.shape[-1] == n_cols and s % NA_TQ == 0

    cvec = jnp.zeros((8, d), F32).at[:b].set(c).at[b].set(c_ctx)
    ada = _ada_call(cvec, w_ada, b_ada).reshape(depth, 8, N_ADA, d)
    rope = _rope_tables(s)
    q_scale = HEAD_DIM ** -0.5

    tm_lat = _pick(s, (1024, 512, 256))
    tn_proj = _pick(n_cols, (768, 512, 256))
    tn_kv = _pick(np.gcd(N_KV_COLS, off_kv), (640, 256, 128))
    tm_merge = _pick(s, (512, 256))
    tn_merge = _pick(d, (512, 256))
    tm_comb = _pick(s, (256,))
    rope_cols = ((OFF_WA_Q, WA_Q_DIM), (off_kv + KV_WA_K, WA_KV_DIM))

    w_perm = jnp.concatenate(
        [w_in[..., REF_OFF_SC:REF_OFF_GATES], w_in[..., REF_OFF_WA_Q:REF_OFF_SC] * q_scale,
         w_in[..., REF_OFF_NA_Q:REF_OFF_WA_Q] * q_scale, w_in[..., REF_OFF_GATES:], w_in[..., :REF_OFF_NA_Q]],
        axis=-1).astype(BF16)
    wr_hi = w_router.astype(BF16)
    wr_lo = (w_router - wr_hi.astype(F32)).astype(BF16)
    w_r2 = jnp.concatenate([wr_hi, wr_lo], axis=-1)
    merge_w = (sc_conv, w_na_out.astype(BF16), w_sc_out.astype(BF16), w_wa_out.astype(BF16), w_out.astype(BF16))
    sub = d // 2 // LANES

    x_lat, x_ctx = x, ctx
    for layer in range(depth):
        last = layer == depth - 1
        mod_lat = [ada[layer, :b, k][:, None, :] for k in range(N_ADA)]
        mod_ctx = [jnp.broadcast_to(ada[layer, b, k][None, None, :], (b, 1, d)) for k in range(N_ADA)]

        p_lat = _proj_call(x_lat, g_mix, mod_lat[0], mod_lat[1], w_perm, layer, 0, n_cols, rope, tm_lat, tn_proj,
                           rope_cols)
        if last:
            p_ctx = _proj_call(x_ctx, g_mix, mod_ctx[0], mod_ctx[1], w_perm, layer, off_kv, N_KV_COLS, None, l,
                               tn_kv, ())
            ctx_kv = 0
        else:
            p_ctx = _proj_call(x_ctx, g_mix, mod_ctx[0], mod_ctx[1], w_perm, layer, 0, n_cols, None, l, tn_proj, ())
            ctx_kv = off_kv

        bias = _na_bias_tables(na_rpb[layer], rows)
        o_na = _na_call(p_lat, p_ctx, bias, off_kv, ctx_kv)
        o_wa = _wa_call(p_lat, p_ctx, wa_sinks[layer], off_kv, ctx_kv)

        x_lat, f_lat, te_lat, tg_lat = _merge_call(
            x_lat, o_na, o_wa, p_lat, layer, *merge_w, mod_lat[2], g_ffn, mod_lat[3], mod_lat[4],
            w_r2, b_router, tm_merge, tn_merge)
        f_all = f_lat.reshape(b * s * sub, LANES)
        te_all = te_lat.reshape(b * s, LANES)[:, :TOP_K]
        if not last:
            o_na_c, o_wa_c = _ctx_attn_call(p_ctx, wa_sinks[layer], off_kv)
            x_ctx, f_ctx, te_ctx, tg_ctx = _merge_call(
                x_ctx, o_na_c, o_wa_c, p_ctx, layer, *merge_w, mod_ctx[2], g_ffn, mod_ctx[3], mod_ctx[4],
                w_r2, b_router, l, tn_merge)
            f_all = jnp.concatenate([f_all, f_ctx.reshape(b * l * sub, LANES)], axis=0)
            te_all = jnp.concatenate([te_all, te_ctx.reshape(b * l, LANES)[:, :TOP_K]], axis=0)

        n_tok_all = te_all.shape[0]
        slot_tok, slot_dst, block_e, n_used, n_out_rows = _route(te_all, n_exp)
        y_tok = _moe_call(layer, block_e, n_used, slot_tok, slot_dst, f_all, w_exp_in, b_exp_in, w_exp_out, b_exp_out,
                          n_out_rows)
        x_lat = _combine_call(x_lat, tg_lat, mod_lat[5], g_final, y_tok, 0, n_tok_all, tm_comb, last)
        if not last:
            x_ctx = _combine_call(x_ctx, tg_ctx, mod_ctx[5], g_final, y_tok, b * s, n_tok_all, l, False)
    return x_lat
```

```python
import functools

import numpy as np
import jax
import jax.numpy as jnp
from jax import lax
from jax.experimental import pallas as pl
from jax.experimental.pallas import tpu as pltpu

F32 = jnp.float32
BF16 = jnp.bfloat16
U32 = jnp.uint32
I32 = jnp.int32

GRID_W = 64
HEAD_DIM = 64
NA_HEADS = 8
NA_WIN_ROWS = 8
NA_WIN_COLS = 16
SC_WIDTH = 1024
SC_CONV_WIDTH = 3
WA_Q_HEADS = 8
WA_KV_HEADS = 2
WA_GROUP = WA_Q_HEADS // WA_KV_HEADS
WA_WINDOW = 128
ROPE_BASE = 10000.0
N_BRANCHES = 3
TOP_K = 4
SWIGLU_LIMIT = 7.0
SWIGLU_ALPHA = 1.702
N_ADA = 6
RMS_EPS = 1e-6
NEG_INF = -1e30

NA_DIM = NA_HEADS * HEAD_DIM
WA_Q_DIM = WA_Q_HEADS * HEAD_DIM
WA_KV_DIM = WA_KV_HEADS * HEAD_DIM

LANES = 128
VMEM_BUDGET = 56 * 1024 * 1024

REF_OFF_NA_Q = 2 * NA_DIM + 2 * WA_KV_DIM
REF_OFF_WA_Q = REF_OFF_NA_Q + NA_DIM
REF_OFF_SC = REF_OFF_WA_Q + WA_Q_DIM
REF_OFF_GATES = REF_OFF_SC + 3 * SC_WIDTH

OFF_SC_B = 0
OFF_SC_C = OFF_SC_B + SC_WIDTH
OFF_SC_H = OFF_SC_C + SC_WIDTH
OFF_WA_Q = OFF_SC_H + SC_WIDTH
OFF_NA_Q = OFF_WA_Q + WA_Q_DIM
OFF_GATES = OFF_NA_Q + NA_DIM
KV_NA_K = 0
KV_NA_V = KV_NA_K + NA_DIM
KV_WA_K = KV_NA_V + NA_DIM
KV_WA_V = KV_WA_K + WA_KV_DIM
N_KV_COLS = KV_WA_V + WA_KV_DIM

NA_QROWS = 4
NA_SLAB_ROWS = NA_QROWS + NA_WIN_ROWS - 1
NA_TQ = NA_QROWS * GRID_W
NA_TK = NA_SLAB_ROWS * GRID_W
WA_TQ = 256
WA_TK = WA_TQ + 2 * WA_WINDOW

MOE_BLOCK = 256


def _params(semantics, vmem_bytes):
    limit = int(min(max(vmem_bytes * 5 // 4 + (4 << 20), 32 << 20), VMEM_BUDGET))
    return pltpu.CompilerParams(dimension_semantics=semantics, vmem_limit_bytes=limit)


def _pick(n, candidates):
    for c in candidates:
        if n % c == 0:
            return c
    return n


def _dot(a, b):
    return jnp.dot(a, b, preferred_element_type=F32)


def _dot_nt(a, b):
    return lax.dot_general(a, b, (((1,), (1,)), ((), ())), preferred_element_type=F32)


def _rms_mod(x, g, shift, scale):
    y = x * lax.rsqrt(jnp.mean(x * x, axis=-1, keepdims=True) + RMS_EPS)
    return (y * g) * (1.0 + scale) + shift


def _ada_kernel(c_ref, w_ref, b_ref, o_ref):
    c = c_ref[...]
    s = c * jax.nn.sigmoid(c)
    o_ref[0] = jnp.dot(s, w_ref[0], preferred_element_type=F32,
                       precision=lax.Precision.HIGHEST) + b_ref[0]


def _ada_call(cvec, w_ada, b_ada):
    depth, d, n = w_ada.shape
    tn = _pick(n, (1024, 768, 512, 256, 128))
    return pl.pallas_call(
        _ada_kernel,
        grid=(depth, n // tn),
        in_specs=[pl.BlockSpec((8, d), lambda l, j: (0, 0)),
                  pl.BlockSpec((1, d, tn), lambda l, j: (l, 0, j)),
                  pl.BlockSpec((1, 1, tn), lambda l, j: (l, 0, j))],
        out_specs=pl.BlockSpec((1, 8, tn), lambda l, j: (l, 0, j)),
        out_shape=jax.ShapeDtypeStruct((depth, 8, n), F32),
        compiler_params=_params(("arbitrary", "arbitrary"), 2 * d * tn * 4),
        name="ada",
    )(cvec, w_ada, b_ada.reshape(depth, 1, n))


def _rope128(x, cos, sin):
    lane = lax.broadcasted_iota(I32, x.shape, 1)
    fwd = pltpu.roll(x, LANES - HEAD_DIM // 4, 1)
    bwd = pltpu.roll(x, HEAD_DIM // 4, 1)
    partner = jnp.where((lane & (HEAD_DIM // 4)) == 0, fwd, bwd)
    return x * cos + partner * sin


def _gather_rows(y_hbm, ybuf, sem, slot_ref, r, buf, sub):
    src = pl.multiple_of(slot_ref[0, 0, r] * sub, sub)
    dst = pl.multiple_of(r * sub, sub)
    return pltpu.make_async_copy(y_hbm.at[pl.ds(src, sub), :], ybuf.at[buf, pl.ds(dst, sub), :], sem.at[buf])


def _wait_rows(y_hbm, ybuf, sem, buf):
    pltpu.make_async_copy(y_hbm.at[pl.ds(0, ybuf.shape[1]), :], ybuf.at[buf], sem.at[buf]).wait()


def _combine_rows(ybuf, buf, gates, tm, sub):
    acc = None
    for k in range(TOP_K):
        yk = _load_token_tiles(ybuf.at[buf, pl.ds(k * tm * sub, tm * sub), :], tm, F32) * gates[:, k:k + 1]
        acc = yk if acc is None else acc + yk
    return acc


def _proj_kernel(*refs, rope_tiles, combine, per):
    refs = list(refs)
    if combine:
        slot_cur_ref, slot_nxt_ref, gate_ref, gtf_ref, y_hbm = refs[:5]
        refs = refs[5:]
    x_ref, g_ref, sh_ref, sc_ref = refs[:4]
    refs = refs[4:]
    if rope_tiles:
        cos_ref, sin_ref = refs[:2]
        refs = refs[2:]
    w_ref, o_ref = refs[:2]
    refs = refs[2:]
    if combine:
        xo_ref, h_ref, ybuf, sem = refs
    else:
        (h_ref,) = refs
    j = pl.program_id(2)
    tm, d = x_ref.shape[1], x_ref.shape[2]
    sub = d // 2 // LANES
    if combine:
        t = pl.program_id(0) * pl.num_programs(1) + pl.program_id(1)
        cur = t % 2
        last_step = jnp.logical_and(t == pl.num_programs(0) * pl.num_programs(1) - 1, j == pl.num_programs(2) - 1)

    @pl.when(j == 0)
    def _():
        if combine:
            @pl.when(t == 0)
            def _():
                def body(r, carry):
                    _gather_rows(y_hbm, ybuf, sem, slot_cur_ref, r, 0, sub).start()
                    return carry
                lax.fori_loop(0, ybuf.shape[1] // sub, body, 0)

            _wait_rows(y_hbm, ybuf, sem, cur)
            x = x_ref[0] + gtf_ref[0] * _combine_rows(ybuf, cur, gate_ref[0], tm, sub)
            xo_ref[0] = x
        else:
            x = x_ref[0]
        h_ref[...] = _rms_mod(x, g_ref[0], sh_ref[0], sc_ref[0]).astype(BF16)

    if combine:
        for q in range(per):
            _gather_rows(y_hbm, ybuf, sem, slot_nxt_ref, j * per + q, 1 - cur, sub).start()
    acc = _dot(h_ref[...], w_ref[0])
    o_ref[0] = acc.astype(BF16)
    for jt, lo, width in rope_tiles:
        @pl.when(j == jt)
        def _(lo=lo, width=width):
            cos, sin = cos_ref[...], sin_ref[...]
            for c0 in range(lo, lo + width, LANES):
                o_ref[0, :, c0:c0 + LANES] = _rope128(acc[:, c0:c0 + LANES], cos, sin).astype(BF16)
    if combine:
        @pl.when(last_step)
        def _():
            _wait_rows(y_hbm, ybuf, sem, 1 - cur)


def _tile_slots(slots, tm, n_rows):
    b, n, _ = slots.shape
    st = slots.reshape(b, n // tm, tm, TOP_K).transpose(0, 1, 3, 2).reshape(b * (n // tm), 1, TOP_K * tm)
    return jnp.pad(st, ((0, 0), (0, 0), (0, n_rows - TOP_K * tm)))


def _proj_call(x, g, shift, scale, w_all, layer, col0, nc, rope, tm, tn, rope_cols, moe=None):
    b, n, d = x.shape
    assert col0 % tn == 0 and nc % tn == 0 and n % tm == 0
    jb = col0 // tn
    ni, nj = n // tm, nc // tn
    sub = d // 2 // LANES
    rope_tiles = []
    if rope is not None:
        for off, width in rope_cols:
            assert off // tn == (off + width - 1) // tn and off % LANES == 0 and width % LANES == 0
            rope_tiles.append((off // tn, off % tn, width))
    in_specs, args, per = [], [], 0
    vmem = 2 * tm * d * 4 + tm * d * 2 + 2 * d * tn * 2 + 2 * tm * tn * 2 + tm * tn * 4 + 4 * tm * LANES * 4
    if moe is not None:
        slots, gates, gt, y_sorted = moe
        per = -(-TOP_K * tm // nj)
        n_rows = per * nj
        slots_t = _tile_slots(slots, tm, n_rows)
        n_t = b * ni

        def smem(index):
            return pl.BlockSpec((1, 1, n_rows), index, memory_space=pltpu.SMEM)

        in_specs += [smem(lambda bb, i, j: (bb * ni + i, 0, 0)),
                     smem(lambda bb, i, j: (jnp.minimum(bb * ni + i + 1, n_t - 1), 0, 0)),
                     pl.BlockSpec((1, tm, LANES), lambda bb, i, j: (bb, i, 0)),
                     pl.BlockSpec((1, 1, d), lambda bb, i, j: (bb, 0, 0)),
                     pl.BlockSpec(memory_space=pl.ANY)]
        args += [slots_t, slots_t, gates, gt, y_sorted]
        vmem += 2 * tm * d * 4 + 2 * n_rows * sub * LANES * 4 + 3 * tm * d * 4
    in_specs += [pl.BlockSpec((1, tm, d), lambda bb, i, j: (bb, i, 0)),
                 pl.BlockSpec((1, 1, d), lambda bb, i, j: (layer, 0, 0)),
                 pl.BlockSpec((1, 1, d), lambda bb, i, j: (bb, 0, 0)),
                 pl.BlockSpec((1, 1, d), lambda bb, i, j: (bb, 0, 0))]
    args += [x, g.reshape(g.shape[0], 1, d), shift, scale]
    if rope is not None:
        in_specs += [pl.BlockSpec((tm, LANES), lambda bb, i, j: (i, 0))] * 2
        args += list(rope)
    in_specs.append(pl.BlockSpec((1, d, tn), lambda bb, i, j: (layer, 0, jb + j)))
    args.append(w_all)
    out_specs = [pl.BlockSpec((1, tm, tn), lambda bb, i, j: (bb, i, j))]
    out_shape = [jax.ShapeDtypeStruct((b, n, nc), BF16)]
    scratch = [pltpu.VMEM((tm, d), BF16)]
    if moe is not None:
        out_specs.append(pl.BlockSpec((1, tm, d), lambda bb, i, j: (bb, i, 0)))
        out_shape.append(jax.ShapeDtypeStruct((b, n, d), F32))
        scratch += [pltpu.VMEM((2, n_rows * sub, LANES), U32), pltpu.SemaphoreType.DMA((2,))]
    limit = int(min(max(vmem * 5 // 4 + (4 << 20), 32 << 20), VMEM_BUDGET))
    out = pl.pallas_call(
        functools.partial(_proj_kernel, rope_tiles=tuple(rope_tiles), combine=moe is not None, per=per),
        grid=(b, ni, nj),
        in_specs=in_specs,
        out_specs=out_specs,
        out_shape=out_shape,
        scratch_shapes=scratch,
        compiler_params=pltpu.CompilerParams(dimension_semantics=("arbitrary", "arbitrary", "arbitrary"),
                                             vmem_limit_bytes=limit, disable_bounds_checks=moe is not None),
        name="proj",
    )(*args)
    return out if moe is not None else out[0]


def _softmax_av(parts, extra=None):
    m = functools.reduce(jnp.maximum, [jnp.max(s, axis=-1, keepdims=True) for s, _ in parts])
    if extra is not None:
        m = jnp.maximum(m, extra)
    l = 0.0 if extra is None else jnp.exp(extra - m)
    o = None
    for s, v in parts:
        p = jnp.exp(s - m)
        l = l + jnp.sum(p, axis=-1, keepdims=True)
        pv = _dot(p.astype(BF16), v)
        o = pv if o is None else o + pv
    return o / l


def _na_kernel(q_ref, k_ref, v_ref, kc_ref, vc_ref, bias_ref, o_ref, *, rows):
    i = pl.program_id(2)
    r_start = jnp.clip(NA_QROWS * i - NA_WIN_ROWS // 2, 0, rows - NA_SLAB_ROWS)
    start = pl.multiple_of(r_start * GRID_W, GRID_W)
    kslab = k_ref[0, pl.ds(start, NA_TK), :]
    vslab = v_ref[0, pl.ds(start, NA_TK), :]
    q, kc, vc = q_ref[0], kc_ref[0], vc_ref[0]
    outs = []
    for hh in range(LANES // HEAD_DIM):
        sl = slice(hh * HEAD_DIM, (hh + 1) * HEAD_DIM)
        qh = q[:, sl]
        s_nb = _dot_nt(qh, kslab[:, sl]) + bias_ref[0, hh]
        s_ctx = _dot_nt(qh, kc[:, sl])
        outs.append(_softmax_av([(s_nb, vslab[:, sl]), (s_ctx, vc[:, sl])]))
    o_ref[0] = jnp.concatenate(outs, axis=1).astype(BF16)


def _na_bias_tables(rpb, rows):
    nblk = rows // NA_QROWS
    assert nblk >= 3 and rows >= NA_SLAB_ROWS
    n_heads, n_dr, n_dc = rpb.shape
    half = NA_WIN_COLS - 1
    vec = jnp.zeros((n_heads, n_dr, LANES), F32)
    vec = vec.at[..., :n_dc - half].set(rpb[..., half:]).at[..., LANES - half:].set(rpb[..., :half])
    toep = jnp.tile(vec, (1, 1, GRID_W))[..., :GRID_W * (LANES - 1)]
    toep = toep.reshape(n_heads, n_dr, GRID_W, LANES - 1)[..., :GRID_W]
    col = np.arange(GRID_W)
    c0 = np.clip(col - NA_WIN_COLS // 2, 0, GRID_W - NA_WIN_COLS)
    col_ok = (col[None, :] >= c0[:, None]) & (col[None, :] < c0[:, None] + NA_WIN_COLS)
    toep = jnp.where(col_ok[None, None], toep, NEG_INF)
    masked = jnp.full((n_heads, GRID_W, GRID_W), NEG_INF, F32)
    cases = []
    for i in (0, 1, nblk - 1):
        rs = int(np.clip(NA_QROWS * i - NA_WIN_ROWS // 2, 0, rows - NA_SLAB_ROWS))
        q_rows = []
        for rl in range(NA_QROWS):
            r = NA_QROWS * i + rl
            r0 = int(np.clip(r - NA_WIN_ROWS // 2, 0, rows - NA_WIN_ROWS))
            tiles = []
            for kl in range(NA_SLAB_ROWS):
                kr = rs + kl
                tiles.append(toep[:, kr - r + NA_WIN_ROWS - 1] if r0 <= kr < r0 + NA_WIN_ROWS else masked)
            q_rows.append(jnp.concatenate(tiles, axis=2))
        cases.append(jnp.concatenate(q_rows, axis=1))
    return jnp.stack(cases)


def _na_call(p_lat, p_ctx, bias, kv_off_lat, kv_off_ctx):
    b, s, _ = p_lat.shape
    l = p_ctx.shape[1]
    rows = s // GRID_W
    nblk = rows // NA_QROWS
    hp = NA_DIM // LANES
    qb, kb, vb = OFF_NA_Q // LANES, (kv_off_lat + KV_NA_K) // LANES, (kv_off_lat + KV_NA_V) // LANES
    kcb, vcb = (kv_off_ctx + KV_NA_K) // LANES, (kv_off_ctx + KV_NA_V) // LANES

    def case(i):
        return jnp.where(i == 0, 0, jnp.where(i == nblk - 1, 2, 1))

    vmem = 4 * s * LANES * 2 + 4 * NA_TQ * NA_TK * 4 + 8 * NA_TQ * (NA_TK + l) * 4
    return pl.pallas_call(
        functools.partial(_na_kernel, rows=rows),
        grid=(b, hp, nblk),
        in_specs=[pl.BlockSpec((1, NA_TQ, LANES), lambda bb, h, i: (bb, i, qb + h)),
                  pl.BlockSpec((1, s, LANES), lambda bb, h, i: (bb, 0, kb + h)),
                  pl.BlockSpec((1, s, LANES), lambda bb, h, i: (bb, 0, vb + h)),
                  pl.BlockSpec((1, l, LANES), lambda bb, h, i: (bb, 0, kcb + h)),
                  pl.BlockSpec((1, l, LANES), lambda bb, h, i: (bb, 0, vcb + h)),
                  pl.BlockSpec((1, LANES // HEAD_DIM, NA_TQ, NA_TK), lambda bb, h, i: (case(i), h, 0, 0))],
        out_specs=pl.BlockSpec((1, NA_TQ, LANES), lambda bb, h, i: (bb, i, h)),
        out_shape=jax.ShapeDtypeStruct((b, s, NA_DIM), BF16),
        compiler_params=_params(("arbitrary", "arbitrary", "arbitrary"), vmem),
        name="natten",
    )(p_lat, p_lat, p_lat, p_ctx, p_ctx, bias)


def _wa_kernel(sink_ref, q_ref, k_ref, v_ref, kc_ref, vc_ref, o_ref, *, seq):
    i = pl.program_id(1)
    start = pl.multiple_of(jnp.clip(WA_TQ * i - WA_WINDOW, 0, seq - WA_TK), WA_WINDOW)
    kslab = k_ref[0, pl.ds(start, WA_TK), :]
    vslab = v_ref[0, pl.ds(start, WA_TK), :]
    qpos = WA_TQ * i + lax.broadcasted_iota(I32, (WA_TQ, WA_TK), 0)
    kpos = start + lax.broadcasted_iota(I32, (WA_TQ, WA_TK), 1)
    band = jnp.where(jnp.abs(kpos - qpos) <= WA_WINDOW, 0.0, NEG_INF).astype(F32)
    q, kc, vc = q_ref[0], kc_ref[0], vc_ref[0]
    outs = []
    for hq in range(WA_Q_HEADS):
        hk = hq // WA_GROUP
        sk = slice(hk * HEAD_DIM, (hk + 1) * HEAD_DIM)
        qh = q[:, hq * HEAD_DIM:(hq + 1) * HEAD_DIM]
        s_loc = _dot_nt(qh, kslab[:, sk]) + band
        s_ctx = _dot_nt(qh, kc[:, sk])
        outs.append(_softmax_av([(s_loc, vslab[:, sk]), (s_ctx, vc[:, sk])], extra=sink_ref[hq]))
    o_ref[0] = jnp.concatenate(outs, axis=1).astype(BF16)


def _wa_call(p_lat, p_ctx, sinks, kv_off_lat, kv_off_ctx):
    b, s, _ = p_lat.shape
    l = p_ctx.shape[1]
    assert s % WA_TQ == 0 and s >= WA_TK
    qb = OFF_WA_Q // WA_Q_DIM
    kb, vb = (kv_off_lat + KV_WA_K) // LANES, (kv_off_lat + KV_WA_V) // LANES
    kcb, vcb = (kv_off_ctx + KV_WA_K) // LANES, (kv_off_ctx + KV_WA_V) // LANES
    vmem = 4 * s * LANES * 2 + 4 * WA_TQ * WA_Q_DIM * 2 + 10 * WA_TQ * (WA_TK + l) * 4
    return pl.pallas_call(
        functools.partial(_wa_kernel, seq=s),
        grid=(b, s // WA_TQ),
        in_specs=[pl.BlockSpec(memory_space=pltpu.SMEM),
                  pl.BlockSpec((1, WA_TQ, WA_Q_DIM), lambda bb, i: (bb, i, qb)),
                  pl.BlockSpec((1, s, LANES), lambda bb, i: (bb, 0, kb)),
                  pl.BlockSpec((1, s, LANES), lambda bb, i: (bb, 0, vb)),
                  pl.BlockSpec((1, l, LANES), lambda bb, i: (bb, 0, kcb)),
                  pl.BlockSpec((1, l, LANES), lambda bb, i: (bb, 0, vcb))],
        out_specs=pl.BlockSpec((1, WA_TQ, WA_Q_DIM), lambda bb, i: (bb, i, 0)),
        out_shape=jax.ShapeDtypeStruct((b, s, WA_Q_DIM), BF16),
        compiler_params=_params(("arbitrary", "arbitrary"), vmem),
        name="winattn",
    )(sinks, p_lat, p_lat, p_lat, p_ctx, p_ctx)


def _ctx_attn_kernel(sink_ref, naq_ref, nak_ref, nav_ref, waq_ref, wak_ref, wav_ref, ona_ref, owa_ref):
    q, k, v = naq_ref[0], nak_ref[0], nav_ref[0]
    outs = []
    for h in range(NA_HEADS):
        sl = slice(h * HEAD_DIM, (h + 1) * HEAD_DIM)
        outs.append(_softmax_av([(_dot_nt(q[:, sl], k[:, sl]), v[:, sl])]))
    ona_ref[0] = jnp.concatenate(outs, axis=1).astype(BF16)
    q, k, v = waq_ref[0], wak_ref[0], wav_ref[0]
    outs = []
    for hq in range(WA_Q_HEADS):
        sk = slice((hq // WA_GROUP) * HEAD_DIM, (hq // WA_GROUP + 1) * HEAD_DIM)
        qh = q[:, hq * HEAD_DIM:(hq + 1) * HEAD_DIM]
        outs.append(_softmax_av([(_dot_nt(qh, k[:, sk]), v[:, sk])], extra=sink_ref[hq]))
    owa_ref[0] = jnp.concatenate(outs, axis=1).astype(BF16)


def _ctx_attn_call(p_ctx, sinks, kv_off):
    b, l, _ = p_ctx.shape

    def spec(width, off):
        return pl.BlockSpec((1, l, width), lambda bb: (bb, 0, off // width))

    return pl.pallas_call(
        _ctx_attn_kernel,
        grid=(b,),
        in_specs=[pl.BlockSpec(memory_space=pltpu.SMEM),
                  spec(NA_DIM, OFF_NA_Q), spec(NA_DIM, kv_off + KV_NA_K), spec(NA_DIM, kv_off + KV_NA_V),
                  spec(WA_Q_DIM, OFF_WA_Q), spec(WA_KV_DIM, kv_off + KV_WA_K), spec(WA_KV_DIM, kv_off + KV_WA_V)],
        out_specs=[pl.BlockSpec((1, l, NA_DIM), lambda bb: (bb, 0, 0)),
                   pl.BlockSpec((1, l, WA_Q_DIM), lambda bb: (bb, 0, 0))],
        out_shape=[jax.ShapeDtypeStruct((b, l, NA_DIM), BF16), jax.ShapeDtypeStruct((b, l, WA_Q_DIM), BF16)],
        compiler_params=_params(("arbitrary",), 16 * l * NA_DIM * 4),
        name="ctxattn",
    )(sinks, p_ctx, p_ctx, p_ctx, p_ctx, p_ctx, p_ctx)


def _pack_bf16_pairs(x):
    n = x.shape[1] // 2
    lo = pltpu.bitcast(x[:, :n].astype(BF16).astype(F32), U32)
    hi = pltpu.bitcast(x[:, n:].astype(BF16).astype(F32), U32)
    return (hi & jnp.uint32(0xFFFF0000)) | (lo >> 16)


def _store_token_tiles(ref, packed):
    m, n = packed.shape
    sub = n // LANES
    for s in range(sub):
        ref[pl.ds(s, m, stride=sub), :] = packed[:, s * LANES:(s + 1) * LANES]


def _load_token_tiles(ref, m, dtype):
    sub = ref.shape[0] // m
    chunks = [ref[pl.ds(s, m, stride=sub), :] for s in range(sub)]
    lo = [pltpu.bitcast(p << 16, F32).astype(dtype) for p in chunks]
    hi = [pltpu.bitcast(p & jnp.uint32(0xFFFF0000), F32).astype(dtype) for p in chunks]
    return jnp.concatenate(lo + hi, axis=1)


def _top_k_lanes(logits, n_exp):
    m_rows = logits.shape[0]
    col = lax.broadcasted_iota(I32, (m_rows, n_exp), 1).astype(F32)
    lane = lax.broadcasted_iota(I32, (m_rows, LANES), 1)
    work = logits
    vals = jnp.zeros((m_rows, LANES), F32)
    idxs = jnp.zeros((m_rows, LANES), F32)
    top = None
    denom = 0.0
    for k in range(TOP_K):
        mx = jnp.max(work, axis=-1, keepdims=True)
        ix = jnp.min(jnp.where(work == mx, col, float(n_exp)), axis=-1, keepdims=True)
        work = jnp.where(col == ix, -jnp.inf, work)
        top = mx if top is None else top
        e = jnp.exp(mx - top)
        denom = denom + e
        vals = jnp.where(lane == k, e, vals)
        idxs = jnp.where(lane == k, ix, idxs)
    return vals / denom, idxs.astype(I32)


def _merge_kernel(*refs, n_exp, has_halo, tn):
    (x_ref, ona_ref, owa_ref, scb_ref, scc_ref, sch_ref) = refs[:6]
    refs = refs[6:]
    if has_halo:
        cprev_ref, hprev_ref, cnext_ref, hnext_ref = refs[:4]
        refs = refs[4:]
    (gna_ref, gsc_ref, gwa_ref, conv_ref, wna_ref, wsc_ref, wwa_ref, wout_ref, gt_ref, gffn_ref, shf_ref, scf_ref,
     wr_ref, br_ref, xo_ref, f_ref, te_ref, tg_ref) = refs
    i = pl.program_id(1)
    tm, d = x_ref.shape[1], x_ref.shape[2]

    u = scc_ref[0].astype(F32) * sch_ref[0].astype(F32)
    zero = jnp.zeros((1, SC_WIDTH), F32)
    if has_halo:
        u_prev = jnp.where(i == 0, zero, cprev_ref[0, 7:8, :].astype(F32) * hprev_ref[0, 7:8, :].astype(F32))
        u_next = jnp.where(i == pl.num_programs(1) - 1, zero,
                           cnext_ref[0, 0:1, :].astype(F32) * hnext_ref[0, 0:1, :].astype(F32))
    else:
        u_prev = u_next = zero
    row = lax.broadcasted_iota(I32, u.shape, 0)
    u_m1 = jnp.where(row == 0, u_prev, pltpu.roll(u, 1, 0))
    u_p1 = jnp.where(row == tm - 1, u_next, pltpu.roll(u, tm - 1, 0))
    y = u_m1 * conv_ref[0, 0:1, :] + u * conv_ref[0, 1:2, :] + u_p1 * conv_ref[0, 2:3, :]
    o_sc = (scb_ref[0].astype(F32) * y).astype(BF16)
    o_na, o_wa = ona_ref[0], owa_ref[0]

    def gate(ref, cs):
        return jax.nn.sigmoid(ref[0, :, cs].astype(F32))

    acc = None
    for n in range(d // tn):
        cs = slice(n * tn, (n + 1) * tn)
        merged = (gate(gna_ref, cs) * _dot(o_na, wna_ref[0, :, cs])
                  + gate(gsc_ref, cs) * _dot(o_sc, wsc_ref[0, :, cs])
                  + gate(gwa_ref, cs) * _dot(o_wa, wwa_ref[0, :, cs]))
        part = _dot(merged.astype(BF16), wout_ref[0, cs, :])
        acc = part if acc is None else acc + part

    xn = x_ref[0] + gt_ref[0] * acc
    xo_ref[0] = xn
    f = _rms_mod(xn, gffn_ref[0], shf_ref[0], scf_ref[0])
    _store_token_tiles(f_ref.at[0], _pack_bf16_pairs(f))
    f_hi = f.astype(BF16)
    f_lo = (f - f_hi.astype(F32)).astype(BF16)
    wr = wr_ref[0]
    logits = (_dot(f_hi, wr[:, :n_exp]) + _dot(f_lo, wr[:, :n_exp]) + _dot(f_hi, wr[:, n_exp:])) + br_ref[0]
    gates, idx = _top_k_lanes(logits, n_exp)
    tg_ref[0] = gates
    te_ref[0] = idx


def _merge_call(x, o_na, o_wa, p, layer, conv_w, w_na, w_sc, w_wa, w_out, gt, g_ffn, sh_f, sc_f, w_r2, b_r, tm, tn):
    b, n_tok, d = x.shape
    depth = w_out.shape[0]
    n_exp = b_r.shape[-1]
    sub = d // 2 // LANES
    nt = n_tok // tm
    has_halo = nt > 1
    assert OFF_GATES % d == 0 and d % tn == 0 and n_tok % tm == 0 and tm % 8 == 0
    gate_b = OFF_GATES // d
    rb = tm // 8

    def rows(width, off):
        return pl.BlockSpec((1, tm, width), lambda bb, i: (bb, i, off // width))

    def resident(shape):
        return pl.BlockSpec((1,) + shape, lambda bb, i: (layer, 0, 0), pipeline_mode=pl.Buffered(1))

    in_specs = [pl.BlockSpec((1, tm, d), lambda bb, i: (bb, i, 0)),
                pl.BlockSpec((1, tm, NA_DIM), lambda bb, i: (bb, i, 0)),
                pl.BlockSpec((1, tm, WA_Q_DIM), lambda bb, i: (bb, i, 0)),
                rows(SC_WIDTH, OFF_SC_B), rows(SC_WIDTH, OFF_SC_C), rows(SC_WIDTH, OFF_SC_H)]
    args = [x, o_na, o_wa, p, p, p]
    if has_halo:
        last8 = n_tok // 8 - 1
        for off in (OFF_SC_C, OFF_SC_H):
            in_specs.append(pl.BlockSpec((1, 8, SC_WIDTH),
                                         lambda bb, i, off=off: (bb, jnp.maximum(i * rb - 1, 0), off // SC_WIDTH)))
        for off in (OFF_SC_C, OFF_SC_H):
            in_specs.append(pl.BlockSpec((1, 8, SC_WIDTH),
                                         lambda bb, i, off=off: (bb, jnp.minimum((i + 1) * rb, last8), off // SC_WIDTH)))
        args += [p, p, p, p]
    for br in range(N_BRANCHES):
        in_specs.append(pl.BlockSpec((1, tm, d), lambda bb, i, br=br: (bb, i, gate_b + br)))
        args.append(p)
    in_specs += [resident((SC_CONV_WIDTH, SC_WIDTH)),
                 resident((NA_DIM, d)), resident((SC_WIDTH, d)), resident((WA_Q_DIM, d)), resident((d, d)),
                 pl.BlockSpec((1, 1, d), lambda bb, i: (bb, 0, 0)),
                 resident((1, d)),
                 pl.BlockSpec((1, 1, d), lambda bb, i: (bb, 0, 0)),
                 pl.BlockSpec((1, 1, d), lambda bb, i: (bb, 0, 0)),
                 resident((d, 2 * n_exp)), resident((1, n_exp))]
    args += [conv_w, w_na, w_sc, w_wa, w_out, gt, g_ffn.reshape(depth, 1, d), sh_f, sc_f, w_r2,
             b_r.reshape(depth, 1, n_exp)]
    out_specs = [pl.BlockSpec((1, tm, d), lambda bb, i: (bb, i, 0)),
                 pl.BlockSpec((1, tm * sub, LANES), lambda bb, i: (bb, i, 0)),
                 pl.BlockSpec((1, tm, LANES), lambda bb, i: (bb, i, 0)),
                 pl.BlockSpec((1, tm, LANES), lambda bb, i: (bb, i, 0))]
    out_shape = [jax.ShapeDtypeStruct((b, n_tok, d), F32), jax.ShapeDtypeStruct((b, n_tok * sub, LANES), U32),
                 jax.ShapeDtypeStruct((b, n_tok, LANES), I32), jax.ShapeDtypeStruct((b, n_tok, LANES), F32)]
    vmem = (4 * tm * d * 4 + 2 * tm * (d // 2) * 4 + 4 * tm * LANES * 4
            + 2 * tm * (NA_DIM + WA_Q_DIM + 3 * SC_WIDTH + N_BRANCHES * d) * 2
            + (NA_DIM + SC_WIDTH + WA_Q_DIM + d) * d * 2 + d * 2 * n_exp * 4 + 5 * tm * d * 4)
    return pl.pallas_call(
        functools.partial(_merge_kernel, n_exp=n_exp, has_halo=has_halo, tn=tn),
        grid=(b, nt),
        in_specs=in_specs,
        out_specs=out_specs,
        out_shape=out_shape,
        compiler_params=_params(("arbitrary", "arbitrary"), vmem),
        name="merge",
    )(*args)


def _moe_kernel(be_ref, nused_ref, tok_cur_ref, tok_nxt_ref, f_hbm, w1_ref, b1_ref, w2_ref, b2_ref, y_ref,
                xbuf, w1b, w2b, gsem, *, d_exp, sub):
    i = pl.program_id(0)
    n_used = nused_ref[0]
    slot = i % 2
    other = 1 - slot
    rows = MOE_BLOCK * sub

    def gather(tok_ref, r, dst_slot):
        src = pl.multiple_of(tok_ref[0, 0, r] * sub, sub)
        return pltpu.make_async_copy(f_hbm.at[pl.ds(src, sub), :], xbuf.at[dst_slot, pl.ds(r * sub, sub), :],
                                     gsem.at[dst_slot])

    def wait_gather(s):
        pltpu.make_async_copy(f_hbm.at[pl.ds(0, rows), :], xbuf.at[s], gsem.at[s]).wait()

    @pl.when(i < n_used)
    def _():
        @pl.when(i == 0)
        def _():
            def body(r, carry):
                gather(tok_cur_ref, r, 0).start()
                return carry
            lax.fori_loop(0, MOE_BLOCK, body, 0)

        @pl.when(jnp.logical_or(i == 0, be_ref[i] != be_ref[jnp.maximum(i - 1, 0)]))
        def _():
            w1b[...] = w1_ref[0, 0].astype(BF16)
            w2b[...] = w2_ref[0, 0].astype(BF16)

        wait_gather(slot)
        x = _load_token_tiles(xbuf.at[slot], MOE_BLOCK, BF16)
        n_chunks = 2
        cw = d_exp // n_chunks
        per = MOE_BLOCK // n_chunks
        y = None
        for c in range(n_chunks):
            for r in range(c * per, (c + 1) * per):
                gather(tok_nxt_ref, r, other).start()
            glu = _dot(x, w1b[:, c * cw:(c + 1) * cw]) + b1_ref[0, 0, :, c * cw:(c + 1) * cw]
            lin = _dot(x, w1b[:, d_exp + c * cw:d_exp + (c + 1) * cw]) + b1_ref[0, 0, :, d_exp + c * cw:d_exp + (c + 1) * cw]
            glu = jnp.minimum(glu, SWIGLU_LIMIT)
            lin = jnp.clip(lin, -SWIGLU_LIMIT, SWIGLU_LIMIT)
            act = glu * jax.nn.sigmoid(SWIGLU_ALPHA * glu) * (lin + 1.0)
            yc = _dot(act.astype(BF16), w2b[c * cw:(c + 1) * cw, :])
            y = yc if y is None else y + yc
        _store_token_tiles(y_ref, _pack_bf16_pairs(y + b2_ref[0, 0]))

        @pl.when(i == n_used - 1)
        def _():
            wait_gather(other)

    @pl.when(i >= n_used)
    def _():
        y_ref[...] = jnp.zeros_like(y_ref)


def _moe_call(layer, block_e, n_used, slot_tok, f_all, w1, b1, w2, b2):
    n_blocks = block_e.shape[0]
    depth, n_exp, d, two_de = w1.shape
    d_exp = two_de // 2
    sub = d // 2 // LANES
    assert d_exp % (2 * LANES) == 0
    rows = MOE_BLOCK * sub

    def smem(index):
        return pl.BlockSpec((1, 1, MOE_BLOCK), index, memory_space=pltpu.SMEM)

    grid_spec = pltpu.PrefetchScalarGridSpec(
        num_scalar_prefetch=2,
        grid=(n_blocks,),
        in_specs=[smem(lambda i, be, nu: (i, 0, 0)),
                  smem(lambda i, be, nu: (jnp.minimum(i + 1, n_blocks - 1), 0, 0)),
                  pl.BlockSpec(memory_space=pl.ANY),
                  pl.BlockSpec((1, 1, d, two_de), lambda i, be, nu: (layer, be[i], 0, 0)),
                  pl.BlockSpec((1, 1, 1, two_de), lambda i, be, nu: (layer, be[i], 0, 0)),
                  pl.BlockSpec((1, 1, d_exp, d), lambda i, be, nu: (layer, be[i], 0, 0)),
                  pl.BlockSpec((1, 1, 1, d), lambda i, be, nu: (layer, be[i], 0, 0))],
        out_specs=pl.BlockSpec((rows, LANES), lambda i, be, nu: (i, 0)),
        scratch_shapes=[pltpu.VMEM((2, rows, LANES), U32),
                        pltpu.VMEM((d, two_de), BF16), pltpu.VMEM((d_exp, d), BF16),
                        pltpu.SemaphoreType.DMA((2,))],
    )
    vmem = (2 * (d * two_de + d_exp * d) * 4 + (d * two_de + d_exp * d) * 2 + 4 * rows * LANES * 4
            + 6 * MOE_BLOCK * (d + two_de) * 4)
    tok3 = slot_tok.reshape(n_blocks, 1, MOE_BLOCK)
    return pl.pallas_call(
        functools.partial(_moe_kernel, d_exp=d_exp, sub=sub),
        grid_spec=grid_spec,
        out_shape=jax.ShapeDtypeStruct((n_blocks * rows, LANES), U32),
        compiler_params=pltpu.CompilerParams(dimension_semantics=("arbitrary",),
                                             vmem_limit_bytes=int(min(vmem + (4 << 20), VMEM_BUDGET)),
                                             disable_bounds_checks=True),
        name="experts",
    )(block_e, n_used, tok3, tok3, f_all, w1, b1.reshape(depth, n_exp, 1, two_de), w2, b2.reshape(depth, n_exp, 1, d))


def _final_kernel(slot_cur_ref, slot_nxt_ref, x_ref, gate_ref, gt_ref, gfin_ref, y_hbm, o_ref, ybuf, sem):
    t = pl.program_id(0) * pl.num_programs(1) + pl.program_id(1)
    n_t = pl.num_programs(0) * pl.num_programs(1)
    tm, d = x_ref.shape[1], x_ref.shape[2]
    sub = d // 2 // LANES
    cur = t % 2

    def issue(slot_ref, buf):
        def body(r, carry):
            _gather_rows(y_hbm, ybuf, sem, slot_ref, r, buf, sub).start()
            return carry
        lax.fori_loop(0, TOP_K * tm, body, 0)

    @pl.when(t == 0)
    def _():
        issue(slot_cur_ref, 0)

    @pl.when(t + 1 < n_t)
    def _():
        issue(slot_nxt_ref, 1 - cur)

    _wait_rows(y_hbm, ybuf, sem, cur)
    xn = x_ref[0] + gt_ref[0] * _combine_rows(ybuf, cur, gate_ref[0], tm, sub)
    o_ref[0] = xn * lax.rsqrt(jnp.mean(xn * xn, axis=-1, keepdims=True) + RMS_EPS) * gfin_ref[...]


def _final_call(x, slots, gates, gt, g_final, y_sorted, tm):
    b, n_tok, d = x.shape
    nt = n_tok // tm
    sub = d // 2 // LANES
    n_rows = TOP_K * tm
    slots_t = _tile_slots(slots, tm, n_rows)
    n_t = b * nt

    def smem(index):
        return pl.BlockSpec((1, 1, n_rows), index, memory_space=pltpu.SMEM)

    vmem = 4 * tm * d * 4 + 2 * n_rows * sub * LANES * 4 + 2 * tm * LANES * 4 + 4 * tm * d * 4
    return pl.pallas_call(
        _final_kernel,
        grid=(b, nt),
        in_specs=[smem(lambda bb, i: (bb * nt + i, 0, 0)),
                  smem(lambda bb, i: (jnp.minimum(bb * nt + i + 1, n_t - 1), 0, 0)),
                  pl.BlockSpec((1, tm, d), lambda bb, i: (bb, i, 0)),
                  pl.BlockSpec((1, tm, LANES), lambda bb, i: (bb, i, 0)),
                  pl.BlockSpec((1, 1, d), lambda bb, i: (bb, 0, 0)),
                  pl.BlockSpec((1, d), lambda bb, i: (0, 0)),
                  pl.BlockSpec(memory_space=pl.ANY)],
        out_specs=pl.BlockSpec((1, tm, d), lambda bb, i: (bb, i, 0)),
        out_shape=jax.ShapeDtypeStruct((b, n_tok, d), F32),
        scratch_shapes=[pltpu.VMEM((2, n_rows * sub, LANES), U32), pltpu.SemaphoreType.DMA((2,))],
        compiler_params=pltpu.CompilerParams(dimension_semantics=("arbitrary", "arbitrary"),
                                             vmem_limit_bytes=int(min(vmem * 5 // 4 + (4 << 20), VMEM_BUDGET)),
                                             disable_bounds_checks=True),
        name="final",
    )(slots_t, slots_t, x, gates, gt, g_final.reshape(1, d), y_sorted)


def _route(top_e, n_exp):
    n_tok = top_e.shape[0]
    n_asg = n_tok * TOP_K
    flat_e = top_e.reshape(n_asg)
    onehot = (flat_e[:, None] == jnp.arange(n_exp, dtype=I32)[None, :]).astype(I32)
    csum = jnp.cumsum(onehot, axis=0)
    rank = jnp.sum(csum * onehot, axis=1) - 1
    counts = csum[-1]
    padded = (counts + MOE_BLOCK - 1) // MOE_BLOCK * MOE_BLOCK
    pad_ends = jnp.cumsum(padded)
    pad_starts = pad_ends - padded
    slot = jnp.sum(jnp.where(onehot > 0, pad_starts[None, :], 0), axis=1) + rank
    n_blocks = -(-(n_asg + n_exp * (MOE_BLOCK - 1)) // MOE_BLOCK)
    n_slots = n_blocks * MOE_BLOCK
    slot_tok = jnp.zeros((n_slots,), I32).at[slot].set(jnp.arange(n_asg, dtype=I32) // TOP_K)
    block_start = jnp.arange(n_blocks, dtype=I32) * MOE_BLOCK
    block_e = jnp.minimum(jnp.sum((pad_ends[None, :] <= block_start[:, None]).astype(I32), axis=1), n_exp - 1)
    n_used = (pad_ends[-1] // MOE_BLOCK).astype(I32).reshape(1)
    return slot.reshape(n_tok, TOP_K), slot_tok, block_e, n_used


def _rope_tables(seq):
    t = np.arange(seq)
    quarter = HEAD_DIM // 4
    inv = jnp.asarray(ROPE_BASE, F32) ** (-jnp.arange(quarter, dtype=F32) / quarter)
    ang_r = jnp.asarray(t // GRID_W, F32)[:, None] * inv[None, :]
    ang_c = jnp.asarray(t % GRID_W, F32)[:, None] * inv[None, :]
    cos = jnp.concatenate([jnp.cos(ang_r)] * 2 + [jnp.cos(ang_c)] * 2, axis=-1)
    sin = jnp.concatenate([-jnp.sin(ang_r), jnp.sin(ang_r), -jnp.sin(ang_c), jnp.sin(ang_c)], axis=-1)
    reps = LANES // HEAD_DIM
    return jnp.tile(cos, (1, reps)), jnp.tile(sin, (1, reps))


def kernel(x, c, ctx, c_ctx, w_ada, b_ada, g_mix, w_in, na_rpb, sc_conv, wa_sinks, w_na_out, w_sc_out, w_wa_out,
           w_out, g_ffn, w_router, b_router, w_exp_in, b_exp_in, w_exp_out, b_exp_out, g_final):
    b, s, d = x.shape
    l = ctx.shape[1]
    depth = w_ada.shape[0]
    n_exp = w_router.shape[-1]
    rows = s // GRID_W
    off_kv = OFF_GATES + N_BRANCHES * d
    n_cols = off_kv + N_KV_COLS
    assert w_in.shape[-1] == n_cols and s % NA_TQ == 0

    cvec = jnp.zeros((8, d), F32).at[:b].set(c).at[b].set(c_ctx)
    ada = _ada_call(cvec, w_ada, b_ada).reshape(depth, 8, N_ADA, d)
    rope = _rope_tables(s)
    q_scale = HEAD_DIM ** -0.5

    tm_lat = _pick(s, (1024, 512, 256))
    tm_lat_moe = _pick(s, (512, 256))
    tn_proj = _pick(n_cols, (768, 512, 256))
    tn_kv = _pick(np.gcd(N_KV_COLS, off_kv), (640, 256, 128))
    tm_merge = _pick(s, (256,))
    tn_merge = _pick(d, (512, 256))
    tm_comb = _pick(s, (256,))
    rope_cols = ((OFF_WA_Q, WA_Q_DIM), (off_kv + KV_WA_K, WA_KV_DIM))

    w_perm = jnp.concatenate(
        [w_in[..., REF_OFF_SC:REF_OFF_GATES], w_in[..., REF_OFF_WA_Q:REF_OFF_SC] * q_scale,
         w_in[..., REF_OFF_NA_Q:REF_OFF_WA_Q] * q_scale, w_in[..., REF_OFF_GATES:], w_in[..., :REF_OFF_NA_Q]],
        axis=-1).astype(BF16)
    wr_hi = w_router.astype(BF16)
    wr_lo = (w_router - wr_hi.astype(F32)).astype(BF16)
    w_r2 = jnp.concatenate([wr_hi, wr_lo], axis=-1)
    merge_w = (sc_conv, w_na_out.astype(BF16), w_sc_out.astype(BF16), w_wa_out.astype(BF16), w_out.astype(BF16))
    sub = d // 2 // LANES

    x_lat, x_ctx = x, ctx
    moe_lat = moe_ctx = None
    for layer in range(depth):
        last = layer == depth - 1
        mod_lat = [ada[layer, :b, k][:, None, :] for k in range(N_ADA)]
        mod_ctx = [jnp.broadcast_to(ada[layer, b, k][None, None, :], (b, 1, d)) for k in range(N_ADA)]

        p_lat = _proj_call(x_lat, g_mix, mod_lat[0], mod_lat[1], w_perm, layer, 0, n_cols, rope,
                           tm_lat if moe_lat is None else tm_lat_moe, tn_proj, rope_cols, moe_lat)
        if last:
            p_ctx = _proj_call(x_ctx, g_mix, mod_ctx[0], mod_ctx[1], w_perm, layer, off_kv, N_KV_COLS, None, l,
                               tn_kv, (), moe_ctx)
            ctx_kv = 0
        else:
            p_ctx = _proj_call(x_ctx, g_mix, mod_ctx[0], mod_ctx[1], w_perm, layer, 0, n_cols, None, l, tn_proj, (),
                               moe_ctx)
            ctx_kv = off_kv
        if moe_lat is not None:
            p_lat, x_lat = p_lat
            p_ctx, x_ctx = p_ctx

        bias = _na_bias_tables(na_rpb[layer], rows)
        o_na = _na_call(p_lat, p_ctx, bias, off_kv, ctx_kv)
        o_wa = _wa_call(p_lat, p_ctx, wa_sinks[layer], off_kv, ctx_kv)

        x_lat, f_lat, te_lat, tg_lat = _merge_call(
            x_lat, o_na, o_wa, p_lat, layer, *merge_w, mod_lat[2], g_ffn, mod_lat[3], mod_lat[4],
            w_r2, b_router, tm_merge, tn_merge)
        f_all = f_lat.reshape(b * s * sub, LANES)
        te_all = te_lat.reshape(b * s, LANES)[:, :TOP_K]
        if not last:
            o_na_c, o_wa_c = _ctx_attn_call(p_ctx, wa_sinks[layer], off_kv)
            x_ctx, f_ctx, te_ctx, tg_ctx = _merge_call(
                x_ctx, o_na_c, o_wa_c, p_ctx, layer, *merge_w, mod_ctx[2], g_ffn, mod_ctx[3], mod_ctx[4],
                w_r2, b_router, l, tn_merge)
            f_all = jnp.concatenate([f_all, f_ctx.reshape(b * l * sub, LANES)], axis=0)
            te_all = jnp.concatenate([te_all, te_ctx.reshape(b * l, LANES)[:, :TOP_K]], axis=0)

        slots, slot_tok, block_e, n_used = _route(te_all, n_exp)
        y_sorted = _moe_call(layer, block_e, n_used, slot_tok, f_all, w_exp_in, b_exp_in, w_exp_out, b_exp_out)
        moe_lat = (slots[:b * s].reshape(b, s, TOP_K), tg_lat, mod_lat[5], y_sorted)
        if not last:
            moe_ctx = (slots[b * s:].reshape(b, l, TOP_K), tg_ctx, mod_ctx[5], y_sorted)
    return _final_call(x_lat, *moe_lat[:3], g_final, moe_lat[3], tm_comb)
```

```python
import functools

import numpy as np
import jax
import jax.numpy as jnp
from jax import lax
from jax.experimental import pallas as pl
from jax.experimental.pallas import tpu as pltpu

F32 = jnp.float32
BF16 = jnp.bfloat16
U32 = jnp.uint32
I32 = jnp.int32

GRID_W = 64
HEAD_DIM = 64
NA_HEADS = 8
NA_WIN_ROWS = 8
NA_WIN_COLS = 16
SC_WIDTH = 1024
SC_CONV_WIDTH = 3
WA_Q_HEADS = 8
WA_KV_HEADS = 2
WA_GROUP = WA_Q_HEADS // WA_KV_HEADS
WA_WINDOW = 128
ROPE_BASE = 10000.0
N_BRANCHES = 3
TOP_K = 4
SWIGLU_LIMIT = 7.0
SWIGLU_ALPHA = 1.702
N_ADA = 6
RMS_EPS = 1e-6
NEG_INF = -1e30

NA_DIM = NA_HEADS * HEAD_DIM
WA_Q_DIM = WA_Q_HEADS * HEAD_DIM
WA_KV_DIM = WA_KV_HEADS * HEAD_DIM

LANES = 128
VMEM_BUDGET = 56 * 1024 * 1024

REF_OFF_NA_Q = 2 * NA_DIM + 2 * WA_KV_DIM
REF_OFF_WA_Q = REF_OFF_NA_Q + NA_DIM
REF_OFF_SC = REF_OFF_WA_Q + WA_Q_DIM
REF_OFF_GATES = REF_OFF_SC + 3 * SC_WIDTH

OFF_SC_B = 0
OFF_SC_C = OFF_SC_B + SC_WIDTH
OFF_SC_H = OFF_SC_C + SC_WIDTH
OFF_WA_Q = OFF_SC_H + SC_WIDTH
OFF_NA_Q = OFF_WA_Q + WA_Q_DIM
OFF_GATES = OFF_NA_Q + NA_DIM
KV_NA_K = 0
KV_NA_V = KV_NA_K + NA_DIM
KV_WA_K = KV_NA_V + NA_DIM
KV_WA_V = KV_WA_K + WA_KV_DIM
N_KV_COLS = KV_WA_V + WA_KV_DIM

NA_QROWS = 4
NA_SLAB_ROWS = NA_QROWS + NA_WIN_ROWS - 1
NA_TQ = NA_QROWS * GRID_W
NA_TK = NA_SLAB_ROWS * GRID_W
WA_TQ = 256
WA_TK = WA_TQ + 2 * WA_WINDOW

MOE_BLOCK = 512
MOE_BUFS = 3


def _params(semantics, vmem_bytes):
    limit = int(min(max(vmem_bytes * 5 // 4 + (4 << 20), 32 << 20), VMEM_BUDGET))
    return pltpu.CompilerParams(dimension_semantics=semantics, vmem_limit_bytes=limit)


def _pick(n, candidates):
    for c in candidates:
        if n % c == 0:
            return c
    return n


def _dot(a, b):
    return jnp.dot(a, b, preferred_element_type=F32)


def _dot_nt(a, b):
    return lax.dot_general(a, b, (((1,), (1,)), ((), ())), preferred_element_type=F32)


def _rms_mod(x, g, shift, scale):
    y = x * lax.rsqrt(jnp.mean(x * x, axis=-1, keepdims=True) + RMS_EPS)
    return (y * g) * (1.0 + scale) + shift


def _ada_kernel(c_ref, w_ref, b_ref, o_ref):
    c = c_ref[...]
    s = c * jax.nn.sigmoid(c)
    o_ref[0] = jnp.dot(s, w_ref[0], preferred_element_type=F32,
                       precision=lax.Precision.HIGHEST) + b_ref[0]


def _ada_call(cvec, w_ada, b_ada):
    depth, d, n = w_ada.shape
    tn = _pick(n, (1024, 768, 512, 256, 128))
    return pl.pallas_call(
        _ada_kernel,
        grid=(depth, n // tn),
        in_specs=[pl.BlockSpec((8, d), lambda l, j: (0, 0)),
                  pl.BlockSpec((1, d, tn), lambda l, j: (l, 0, j)),
                  pl.BlockSpec((1, 1, tn), lambda l, j: (l, 0, j))],
        out_specs=pl.BlockSpec((1, 8, tn), lambda l, j: (l, 0, j)),
        out_shape=jax.ShapeDtypeStruct((depth, 8, n), F32),
        compiler_params=_params(("arbitrary", "arbitrary"), 2 * d * tn * 4),
        name="ada",
    )(cvec, w_ada, b_ada.reshape(depth, 1, n))


def _rope128(x, cos, sin):
    lane = lax.broadcasted_iota(I32, x.shape, 1)
    fwd = pltpu.roll(x, LANES - HEAD_DIM // 4, 1)
    bwd = pltpu.roll(x, HEAD_DIM // 4, 1)
    partner = jnp.where((lane & (HEAD_DIM // 4)) == 0, fwd, bwd)
    return x * cos + partner * sin


def _gather_rows(y_hbm, ybuf, sem, slot_ref, r, buf, sub):
    src = pl.multiple_of(slot_ref[0, 0, r] * sub, sub)
    dst = pl.multiple_of(r * sub, sub)
    return pltpu.make_async_copy(y_hbm.at[pl.ds(src, sub), :], ybuf.at[buf, pl.ds(dst, sub), :], sem.at[buf])


def _wait_rows(y_hbm, ybuf, sem, buf):
    pltpu.make_async_copy(y_hbm.at[pl.ds(0, ybuf.shape[1]), :], ybuf.at[buf], sem.at[buf]).wait()


def _combine_rows(ybuf, buf, gates, tm, sub):
    acc = None
    for k in range(TOP_K):
        yk = _load_token_tiles(ybuf.at[buf, pl.ds(k * tm * sub, tm * sub), :], tm, F32) * gates[:, k:k + 1]
        acc = yk if acc is None else acc + yk
    return acc


def _proj_kernel(*refs, rope_tiles, combine, per):
    refs = list(refs)
    if combine:
        slot_cur_ref, slot_nxt_ref, gate_ref, gtf_ref, y_hbm = refs[:5]
        refs = refs[5:]
    x_ref, g_ref, sh_ref, sc_ref = refs[:4]
    refs = refs[4:]
    if rope_tiles:
        cos_ref, sin_ref = refs[:2]
        refs = refs[2:]
    w_ref, o_ref = refs[:2]
    refs = refs[2:]
    if combine:
        xo_ref, h_ref, ybuf, sem = refs
    else:
        (h_ref,) = refs
    j = pl.program_id(2)
    tm, d = x_ref.shape[1], x_ref.shape[2]
    sub = d // 2 // LANES
    if combine:
        t = pl.program_id(0) * pl.num_programs(1) + pl.program_id(1)
        cur = t % 2
        last_step = jnp.logical_and(t == pl.num_programs(0) * pl.num_programs(1) - 1, j == pl.num_programs(2) - 1)

    @pl.when(j == 0)
    def _():
        if combine:
            @pl.when(t == 0)
            def _():
                def body(r, carry):
                    _gather_rows(y_hbm, ybuf, sem, slot_cur_ref, r, 0, sub).start()
                    return carry
                lax.fori_loop(0, ybuf.shape[1] // sub, body, 0)

            _wait_rows(y_hbm, ybuf, sem, cur)
            x = x_ref[0] + gtf_ref[0] * _combine_rows(ybuf, cur, gate_ref[0], tm, sub)
            xo_ref[0] = x
        else:
            x = x_ref[0]
        h_ref[...] = _rms_mod(x, g_ref[0], sh_ref[0], sc_ref[0]).astype(BF16)

    if combine:
        for q in range(per):
            _gather_rows(y_hbm, ybuf, sem, slot_nxt_ref, j * per + q, 1 - cur, sub).start()
    acc = _dot(h_ref[...], w_ref[0])
    o_ref[0] = acc.astype(BF16)
    for jt, lo, width in rope_tiles:
        @pl.when(j == jt)
        def _(lo=lo, width=width):
            cos, sin = cos_ref[...], sin_ref[...]
            for c0 in range(lo, lo + width, LANES):
                o_ref[0, :, c0:c0 + LANES] = _rope128(acc[:, c0:c0 + LANES], cos, sin).astype(BF16)
    if combine:
        @pl.when(last_step)
        def _():
            _wait_rows(y_hbm, ybuf, sem, 1 - cur)


def _tile_slots(slots, tm, n_rows):
    b, n, _ = slots.shape
    st = slots.reshape(b, n // tm, tm, TOP_K).transpose(0, 1, 3, 2).reshape(b * (n // tm), 1, TOP_K * tm)
    return jnp.pad(st, ((0, 0), (0, 0), (0, n_rows - TOP_K * tm)))


def _proj_call(x, g, shift, scale, w_all, layer, col0, nc, rope, tm, tn, rope_cols, moe=None):
    b, n, d = x.shape
    assert col0 % tn == 0 and nc % tn == 0 and n % tm == 0
    jb = col0 // tn
    ni, nj = n // tm, nc // tn
    sub = d // 2 // LANES
    rope_tiles = []
    if rope is not None:
        for off, width in rope_cols:
            assert off // tn == (off + width - 1) // tn and off % LANES == 0 and width % LANES == 0
            rope_tiles.append((off // tn, off % tn, width))
    in_specs, args, per = [], [], 0
    vmem = 2 * tm * d * 4 + tm * d * 2 + 2 * d * tn * 2 + 2 * tm * tn * 2 + tm * tn * 4 + 4 * tm * LANES * 4
    if moe is not None:
        slots, gates, gt, y_sorted = moe
        per = -(-TOP_K * tm // nj)
        n_rows = per * nj
        slots_t = _tile_slots(slots, tm, n_rows)
        n_t = b * ni

        def smem(index):
            return pl.BlockSpec((1, 1, n_rows), index, memory_space=pltpu.SMEM)

        in_specs += [smem(lambda bb, i, j: (bb * ni + i, 0, 0)),
                     smem(lambda bb, i, j: (jnp.minimum(bb * ni + i + 1, n_t - 1), 0, 0)),
                     pl.BlockSpec((1, tm, LANES), lambda bb, i, j: (bb, i, 0)),
                     pl.BlockSpec((1, 1, d), lambda bb, i, j: (bb, 0, 0)),
                     pl.BlockSpec(memory_space=pl.ANY)]
        args += [slots_t, slots_t, gates, gt, y_sorted]
        vmem += 2 * tm * d * 4 + 2 * n_rows * sub * LANES * 4 + 3 * tm * d * 4
    in_specs += [pl.BlockSpec((1, tm, d), lambda bb, i, j: (bb, i, 0)),
                 pl.BlockSpec((1, 1, d), lambda bb, i, j: (layer, 0, 0)),
                 pl.BlockSpec((1, 1, d), lambda bb, i, j: (bb, 0, 0)),
                 pl.BlockSpec((1, 1, d), lambda bb, i, j: (bb, 0, 0))]
    args += [x, g.reshape(g.shape[0], 1, d), shift, scale]
    if rope is not None:
        in_specs += [pl.BlockSpec((tm, LANES), lambda bb, i, j: (i, 0))] * 2
        args += list(rope)
    in_specs.append(pl.BlockSpec((1, d, tn), lambda bb, i, j: (layer, 0, jb + j)))
    args.append(w_all)
    out_specs = [pl.BlockSpec((1, tm, tn), lambda bb, i, j: (bb, i, j))]
    out_shape = [jax.ShapeDtypeStruct((b, n, nc), BF16)]
    scratch = [pltpu.VMEM((tm, d), BF16)]
    if moe is not None:
        out_specs.append(pl.BlockSpec((1, tm, d), lambda bb, i, j: (bb, i, 0)))
        out_shape.append(jax.ShapeDtypeStruct((b, n, d), F32))
        scratch += [pltpu.VMEM((2, n_rows * sub, LANES), U32), pltpu.SemaphoreType.DMA((2,))]
    limit = int(min(max(vmem * 5 // 4 + (4 << 20), 32 << 20), VMEM_BUDGET))
    out = pl.pallas_call(
        functools.partial(_proj_kernel, rope_tiles=tuple(rope_tiles), combine=moe is not None, per=per),
        grid=(b, ni, nj),
        in_specs=in_specs,
        out_specs=out_specs,
        out_shape=out_shape,
        scratch_shapes=scratch,
        compiler_params=pltpu.CompilerParams(dimension_semantics=("arbitrary", "arbitrary", "arbitrary"),
                                             vmem_limit_bytes=limit, disable_bounds_checks=moe is not None),
        name="proj",
    )(*args)
    return out if moe is not None else out[0]


def _softmax_av(parts, extra=None):
    m = functools.reduce(jnp.maximum, [jnp.max(s, axis=-1, keepdims=True) for s, _ in parts])
    if extra is not None:
        m = jnp.maximum(m, extra)
    l = 0.0 if extra is None else jnp.exp(extra - m)
    o = None
    for s, v in parts:
        p = jnp.exp(s - m)
        l = l + jnp.sum(p, axis=-1, keepdims=True)
        pv = _dot(p.astype(BF16), v)
        o = pv if o is None else o + pv
    return o / l


def _na_kernel(q_ref, k_ref, v_ref, kc_ref, vc_ref, bias_ref, o_ref, *, rows):
    i = pl.program_id(2)
    r_start = jnp.clip(NA_QROWS * i - NA_WIN_ROWS // 2, 0, rows - NA_SLAB_ROWS)
    start = pl.multiple_of(r_start * GRID_W, GRID_W)
    kslab = k_ref[0, pl.ds(start, NA_TK), :]
    vslab = v_ref[0, pl.ds(start, NA_TK), :]
    q, kc, vc = q_ref[0], kc_ref[0], vc_ref[0]
    outs = []
    for hh in range(LANES // HEAD_DIM):
        sl = slice(hh * HEAD_DIM, (hh + 1) * HEAD_DIM)
        qh = q[:, sl]
        s_nb = _dot_nt(qh, kslab[:, sl]) + bias_ref[0, hh]
        s_ctx = _dot_nt(qh, kc[:, sl])
        outs.append(_softmax_av([(s_nb, vslab[:, sl]), (s_ctx, vc[:, sl])]))
    o_ref[0] = jnp.concatenate(outs, axis=1).astype(BF16)


def _na_bias_tables(rpb, rows):
    nblk = rows // NA_QROWS
    assert nblk >= 3 and rows >= NA_SLAB_ROWS
    n_heads, n_dr, n_dc = rpb.shape
    half = NA_WIN_COLS - 1
    vec = jnp.zeros((n_heads, n_dr, LANES), F32)
    vec = vec.at[..., :n_dc - half].set(rpb[..., half:]).at[..., LANES - half:].set(rpb[..., :half])
    toep = jnp.tile(vec, (1, 1, GRID_W))[..., :GRID_W * (LANES - 1)]
    toep = toep.reshape(n_heads, n_dr, GRID_W, LANES - 1)[..., :GRID_W]
    col = np.arange(GRID_W)
    c0 = np.clip(col - NA_WIN_COLS // 2, 0, GRID_W - NA_WIN_COLS)
    col_ok = (col[None, :] >= c0[:, None]) & (col[None, :] < c0[:, None] + NA_WIN_COLS)
    toep = jnp.where(col_ok[None, None], toep, NEG_INF)
    masked = jnp.full((n_heads, GRID_W, GRID_W), NEG_INF, F32)
    cases = []
    for i in (0, 1, nblk - 1):
        rs = int(np.clip(NA_QROWS * i - NA_WIN_ROWS // 2, 0, rows - NA_SLAB_ROWS))
        q_rows = []
        for rl in range(NA_QROWS):
            r = NA_QROWS * i + rl
            r0 = int(np.clip(r - NA_WIN_ROWS // 2, 0, rows - NA_WIN_ROWS))
            tiles = []
            for kl in range(NA_SLAB_ROWS):
                kr = rs + kl
                tiles.append(toep[:, kr - r + NA_WIN_ROWS - 1] if r0 <= kr < r0 + NA_WIN_ROWS else masked)
            q_rows.append(jnp.concatenate(tiles, axis=2))
        cases.append(jnp.concatenate(q_rows, axis=1))
    return jnp.stack(cases)


def _na_call(p_lat, p_ctx, bias, kv_off_lat, kv_off_ctx):
    b, s, _ = p_lat.shape
    l = p_ctx.shape[1]
    rows = s // GRID_W
    nblk = rows // NA_QROWS
    hp = NA_DIM // LANES
    qb, kb, vb = OFF_NA_Q // LANES, (kv_off_lat + KV_NA_K) // LANES, (kv_off_lat + KV_NA_V) // LANES
    kcb, vcb = (kv_off_ctx + KV_NA_K) // LANES, (kv_off_ctx + KV_NA_V) // LANES

    def case(i):
        return jnp.where(i == 0, 0, jnp.where(i == nblk - 1, 2, 1))

    vmem = 4 * s * LANES * 2 + 4 * NA_TQ * NA_TK * 4 + 8 * NA_TQ * (NA_TK + l) * 4
    return pl.pallas_call(
        functools.partial(_na_kernel, rows=rows),
        grid=(b, hp, nblk),
        in_specs=[pl.BlockSpec((1, NA_TQ, LANES), lambda bb, h, i: (bb, i, qb + h)),
                  pl.BlockSpec((1, s, LANES), lambda bb, h, i: (bb, 0, kb + h)),
                  pl.BlockSpec((1, s, LANES), lambda bb, h, i: (bb, 0, vb + h)),
                  pl.BlockSpec((1, l, LANES), lambda bb, h, i: (bb, 0, kcb + h)),
                  pl.BlockSpec((1, l, LANES), lambda bb, h, i: (bb, 0, vcb + h)),
                  pl.BlockSpec((1, LANES // HEAD_DIM, NA_TQ, NA_TK), lambda bb, h, i: (case(i), h, 0, 0))],
        out_specs=pl.BlockSpec((1, NA_TQ, LANES), lambda bb, h, i: (bb, i, h)),
        out_shape=jax.ShapeDtypeStruct((b, s, NA_DIM), BF16),
        compiler_params=_params(("arbitrary", "arbitrary", "arbitrary"), vmem),
        name="natten",
    )(p_lat, p_lat, p_lat, p_ctx, p_ctx, bias)


def _wa_kernel(sink_ref, q_ref, k_ref, v_ref, kc_ref, vc_ref, o_ref, *, seq):
    i = pl.program_id(1)
    start = pl.multiple_of(jnp.clip(WA_TQ * i - WA_WINDOW, 0, seq - WA_TK), WA_WINDOW)
    kslab = k_ref[0, pl.ds(start, WA_TK), :]
    vslab = v_ref[0, pl.ds(start, WA_TK), :]
    qpos = WA_TQ * i + lax.broadcasted_iota(I32, (WA_TQ, WA_TK), 0)
    kpos = start + lax.broadcasted_iota(I32, (WA_TQ, WA_TK), 1)
    band = jnp.where(jnp.abs(kpos - qpos) <= WA_WINDOW, 0.0, NEG_INF).astype(F32)
    q, kc, vc = q_ref[0], kc_ref[0], vc_ref[0]
    outs = []
    for hq in range(WA_Q_HEADS):
        hk = hq // WA_GROUP
        sk = slice(hk * HEAD_DIM, (hk + 1) * HEAD_DIM)
        qh = q[:, hq * HEAD_DIM:(hq + 1) * HEAD_DIM]
        s_loc = _dot_nt(qh, kslab[:, sk]) + band
        s_ctx = _dot_nt(qh, kc[:, sk])
        outs.append(_softmax_av([(s_loc, vslab[:, sk]), (s_ctx, vc[:, sk])], extra=sink_ref[hq]))
    o_ref[0] = jnp.concatenate(outs, axis=1).astype(BF16)


def _wa_call(p_lat, p_ctx, sinks, kv_off_lat, kv_off_ctx):
    b, s, _ = p_lat.shape
    l = p_ctx.shape[1]
    assert s % WA_TQ == 0 and s >= WA_TK
    qb = OFF_WA_Q // WA_Q_DIM
    kb, vb = (kv_off_lat + KV_WA_K) // LANES, (kv_off_lat + KV_WA_V) // LANES
    kcb, vcb = (kv_off_ctx + KV_WA_K) // LANES, (kv_off_ctx + KV_WA_V) // LANES
    vmem = 4 * s * LANES * 2 + 4 * WA_TQ * WA_Q_DIM * 2 + 10 * WA_TQ * (WA_TK + l) * 4
    return pl.pallas_call(
        functools.partial(_wa_kernel, seq=s),
        grid=(b, s // WA_TQ),
        in_specs=[pl.BlockSpec(memory_space=pltpu.SMEM),
                  pl.BlockSpec((1, WA_TQ, WA_Q_DIM), lambda bb, i: (bb, i, qb)),
                  pl.BlockSpec((1, s, LANES), lambda bb, i: (bb, 0, kb)),
                  pl.BlockSpec((1, s, LANES), lambda bb, i: (bb, 0, vb)),
                  pl.BlockSpec((1, l, LANES), lambda bb, i: (bb, 0, kcb)),
                  pl.BlockSpec((1, l, LANES), lambda bb, i: (bb, 0, vcb))],
        out_specs=pl.BlockSpec((1, WA_TQ, WA_Q_DIM), lambda bb, i: (bb, i, 0)),
        out_shape=jax.ShapeDtypeStruct((b, s, WA_Q_DIM), BF16),
        compiler_params=_params(("arbitrary", "arbitrary"), vmem),
        name="winattn",
    )(sinks, p_lat, p_lat, p_lat, p_ctx, p_ctx)


def _ctx_attn_kernel(sink_ref, naq_ref, nak_ref, nav_ref, waq_ref, wak_ref, wav_ref, ona_ref, owa_ref):
    q, k, v = naq_ref[0], nak_ref[0], nav_ref[0]
    outs = []
    for h in range(NA_HEADS):
        sl = slice(h * HEAD_DIM, (h + 1) * HEAD_DIM)
        outs.append(_softmax_av([(_dot_nt(q[:, sl], k[:, sl]), v[:, sl])]))
    ona_ref[0] = jnp.concatenate(outs, axis=1).astype(BF16)
    q, k, v = waq_ref[0], wak_ref[0], wav_ref[0]
    outs = []
    for hq in range(WA_Q_HEADS):
        sk = slice((hq // WA_GROUP) * HEAD_DIM, (hq // WA_GROUP + 1) * HEAD_DIM)
        qh = q[:, hq * HEAD_DIM:(hq + 1) * HEAD_DIM]
        outs.append(_softmax_av([(_dot_nt(qh, k[:, sk]), v[:, sk])], extra=sink_ref[hq]))
    owa_ref[0] = jnp.concatenate(outs, axis=1).astype(BF16)


def _ctx_attn_call(p_ctx, sinks, kv_off):
    b, l, _ = p_ctx.shape

    def spec(width, off):
        return pl.BlockSpec((1, l, width), lambda bb: (bb, 0, off // width))

    return pl.pallas_call(
        _ctx_attn_kernel,
        grid=(b,),
        in_specs=[pl.BlockSpec(memory_space=pltpu.SMEM),
                  spec(NA_DIM, OFF_NA_Q), spec(NA_DIM, kv_off + KV_NA_K), spec(NA_DIM, kv_off + KV_NA_V),
                  spec(WA_Q_DIM, OFF_WA_Q), spec(WA_KV_DIM, kv_off + KV_WA_K), spec(WA_KV_DIM, kv_off + KV_WA_V)],
        out_specs=[pl.BlockSpec((1, l, NA_DIM), lambda bb: (bb, 0, 0)),
                   pl.BlockSpec((1, l, WA_Q_DIM), lambda bb: (bb, 0, 0))],
        out_shape=[jax.ShapeDtypeStruct((b, l, NA_DIM), BF16), jax.ShapeDtypeStruct((b, l, WA_Q_DIM), BF16)],
        compiler_params=_params(("arbitrary",), 16 * l * NA_DIM * 4),
        name="ctxattn",
    )(sinks, p_ctx, p_ctx, p_ctx, p_ctx, p_ctx, p_ctx)


def _pack_bf16_pairs(x):
    n = x.shape[1] // 2
    lo = pltpu.bitcast(x[:, :n].astype(BF16).astype(F32), U32)
    hi = pltpu.bitcast(x[:, n:].astype(BF16).astype(F32), U32)
    return (hi & jnp.uint32(0xFFFF0000)) | (lo >> 16)


def _store_token_tiles(ref, packed):
    m, n = packed.shape
    sub = n // LANES
    for s in range(sub):
        ref[pl.ds(s, m, stride=sub), :] = packed[:, s * LANES:(s + 1) * LANES]


def _load_token_tiles(ref, m, dtype):
    sub = ref.shape[0] // m
    chunks = [ref[pl.ds(s, m, stride=sub), :] for s in range(sub)]
    lo = [pltpu.bitcast(p << 16, F32).astype(dtype) for p in chunks]
    hi = [pltpu.bitcast(p & jnp.uint32(0xFFFF0000), F32).astype(dtype) for p in chunks]
    return jnp.concatenate(lo + hi, axis=1)


def _top_k_lanes(logits, n_exp):
    m_rows = logits.shape[0]
    col = lax.broadcasted_iota(I32, (m_rows, n_exp), 1).astype(F32)
    lane = lax.broadcasted_iota(I32, (m_rows, LANES), 1)
    work = logits
    vals = jnp.zeros((m_rows, LANES), F32)
    idxs = jnp.zeros((m_rows, LANES), F32)
    top = None
    denom = 0.0
    for k in range(TOP_K):
        mx = jnp.max(work, axis=-1, keepdims=True)
        ix = jnp.min(jnp.where(work == mx, col, float(n_exp)), axis=-1, keepdims=True)
        work = jnp.where(col == ix, -jnp.inf, work)
        top = mx if top is None else top
        e = jnp.exp(mx - top)
        denom = denom + e
        vals = jnp.where(lane == k, e, vals)
        idxs = jnp.where(lane == k, ix, idxs)
    return vals / denom, idxs.astype(I32)


def _merge_kernel(*refs, n_exp, has_halo, tn):
    (x_ref, ona_ref, owa_ref, scb_ref, scc_ref, sch_ref) = refs[:6]
    refs = refs[6:]
    if has_halo:
        cprev_ref, hprev_ref, cnext_ref, hnext_ref = refs[:4]
        refs = refs[4:]
    (gna_ref, gsc_ref, gwa_ref, conv_ref, wna_ref, wsc_ref, wwa_ref, wout_ref, gt_ref, gffn_ref, shf_ref, scf_ref,
     wr_ref, br_ref, xo_ref, f_ref, te_ref, tg_ref) = refs
    i = pl.program_id(1)
    tm, d = x_ref.shape[1], x_ref.shape[2]

    u = scc_ref[0].astype(F32) * sch_ref[0].astype(F32)
    zero = jnp.zeros((1, SC_WIDTH), F32)
    if has_halo:
        u_prev = jnp.where(i == 0, zero, cprev_ref[0, 7:8, :].astype(F32) * hprev_ref[0, 7:8, :].astype(F32))
        u_next = jnp.where(i == pl.num_programs(1) - 1, zero,
                           cnext_ref[0, 0:1, :].astype(F32) * hnext_ref[0, 0:1, :].astype(F32))
    else:
        u_prev = u_next = zero
    row = lax.broadcasted_iota(I32, u.shape, 0)
    u_m1 = jnp.where(row == 0, u_prev, pltpu.roll(u, 1, 0))
    u_p1 = jnp.where(row == tm - 1, u_next, pltpu.roll(u, tm - 1, 0))
    y = u_m1 * conv_ref[0, 0:1, :] + u * conv_ref[0, 1:2, :] + u_p1 * conv_ref[0, 2:3, :]
    o_sc = (scb_ref[0].astype(F32) * y).astype(BF16)
    o_na, o_wa = ona_ref[0], owa_ref[0]

    def gate(ref, cs):
        return jax.nn.sigmoid(ref[0, :, cs].astype(F32))

    acc = None
    for n in range(d // tn):
        cs = slice(n * tn, (n + 1) * tn)
        merged = (gate(gna_ref, cs) * _dot(o_na, wna_ref[0, :, cs])
                  + gate(gsc_ref, cs) * _dot(o_sc, wsc_ref[0, :, cs])
                  + gate(gwa_ref, cs) * _dot(o_wa, wwa_ref[0, :, cs]))
        part = _dot(merged.astype(BF16), wout_ref[0, cs, :])
        acc = part if acc is None else acc + part

    xn = x_ref[0] + gt_ref[0] * acc
    xo_ref[0] = xn
    f = _rms_mod(xn, gffn_ref[0], shf_ref[0], scf_ref[0])
    _store_token_tiles(f_ref.at[0], _pack_bf16_pairs(f))
    f_hi = f.astype(BF16)
    f_lo = (f - f_hi.astype(F32)).astype(BF16)
    wr = wr_ref[0]
    logits = (_dot(f_hi, wr[:, :n_exp]) + _dot(f_lo, wr[:, :n_exp]) + _dot(f_hi, wr[:, n_exp:])) + br_ref[0]
    gates, idx = _top_k_lanes(logits, n_exp)
    tg_ref[0] = gates
    te_ref[0] = idx


def _merge_call(x, o_na, o_wa, p, layer, conv_w, w_na, w_sc, w_wa, w_out, gt, g_ffn, sh_f, sc_f, w_r2, b_r, tm, tn):
    b, n_tok, d = x.shape
    depth = w_out.shape[0]
    n_exp = b_r.shape[-1]
    sub = d // 2 // LANES
    nt = n_tok // tm
    has_halo = nt > 1
    assert OFF_GATES % d == 0 and d % tn == 0 and n_tok % tm == 0 and tm % 8 == 0
    gate_b = OFF_GATES // d
    rb = tm // 8

    def rows(width, off):
        return pl.BlockSpec((1, tm, width), lambda bb, i: (bb, i, off // width))

    def resident(shape):
        return pl.BlockSpec((1,) + shape, lambda bb, i: (layer, 0, 0), pipeline_mode=pl.Buffered(1))

    in_specs = [pl.BlockSpec((1, tm, d), lambda bb, i: (bb, i, 0)),
                pl.BlockSpec((1, tm, NA_DIM), lambda bb, i: (bb, i, 0)),
                pl.BlockSpec((1, tm, WA_Q_DIM), lambda bb, i: (bb, i, 0)),
                rows(SC_WIDTH, OFF_SC_B), rows(SC_WIDTH, OFF_SC_C), rows(SC_WIDTH, OFF_SC_H)]
    args = [x, o_na, o_wa, p, p, p]
    if has_halo:
        last8 = n_tok // 8 - 1
        for off in (OFF_SC_C, OFF_SC_H):
            in_specs.append(pl.BlockSpec((1, 8, SC_WIDTH),
                                         lambda bb, i, off=off: (bb, jnp.maximum(i * rb - 1, 0), off // SC_WIDTH)))
        for off in (OFF_SC_C, OFF_SC_H):
            in_specs.append(pl.BlockSpec((1, 8, SC_WIDTH),
                                         lambda bb, i, off=off: (bb, jnp.minimum((i + 1) * rb, last8), off // SC_WIDTH)))
        args += [p, p, p, p]
    for br in range(N_BRANCHES):
        in_specs.append(pl.BlockSpec((1, tm, d), lambda bb, i, br=br: (bb, i, gate_b + br)))
        args.append(p)
    in_specs += [resident((SC_CONV_WIDTH, SC_WIDTH)),
                 resident((NA_DIM, d)), resident((SC_WIDTH, d)), resident((WA_Q_DIM, d)), resident((d, d)),
                 pl.BlockSpec((1, 1, d), lambda bb, i: (bb, 0, 0)),
                 resident((1, d)),
                 pl.BlockSpec((1, 1, d), lambda bb, i: (bb, 0, 0)),
                 pl.BlockSpec((1, 1, d), lambda bb, i: (bb, 0, 0)),
                 resident((d, 2 * n_exp)), resident((1, n_exp))]
    args += [conv_w, w_na, w_sc, w_wa, w_out, gt, g_ffn.reshape(depth, 1, d), sh_f, sc_f, w_r2,
             b_r.reshape(depth, 1, n_exp)]
    out_specs = [pl.BlockSpec((1, tm, d), lambda bb, i: (bb, i, 0)),
                 pl.BlockSpec((1, tm * sub, LANES), lambda bb, i: (bb, i, 0)),
                 pl.BlockSpec((1, tm, LANES), lambda bb, i: (bb, i, 0)),
                 pl.BlockSpec((1, tm, LANES), lambda bb, i: (bb, i, 0))]
    out_shape = [jax.ShapeDtypeStruct((b, n_tok, d), F32), jax.ShapeDtypeStruct((b, n_tok * sub, LANES), U32),
                 jax.ShapeDtypeStruct((b, n_tok, LANES), I32), jax.ShapeDtypeStruct((b, n_tok, LANES), F32)]
    vmem = (4 * tm * d * 4 + 2 * tm * (d // 2) * 4 + 4 * tm * LANES * 4
            + 2 * tm * (NA_DIM + WA_Q_DIM + 3 * SC_WIDTH + N_BRANCHES * d) * 2
            + (NA_DIM + SC_WIDTH + WA_Q_DIM + d) * d * 2 + d * 2 * n_exp * 4 + 5 * tm * d * 4)
    return pl.pallas_call(
        functools.partial(_merge_kernel, n_exp=n_exp, has_halo=has_halo, tn=tn),
        grid=(b, nt),
        in_specs=in_specs,
        out_specs=out_specs,
        out_shape=out_shape,
        compiler_params=_params(("arbitrary", "arbitrary"), vmem),
        name="merge",
    )(*args)


def _moe_kernel(be_ref, nused_ref, tok_cur_ref, tok_nxt_ref, tok_far_ref, f_hbm, w1_ref, b1_ref, w2_ref, b2_ref,
                y_ref, xbuf, w1b, w2b, gsem, *, d_exp, sub):
    i = pl.program_id(0)
    n_used = nused_ref[0]
    slot = i % MOE_BUFS
    nxt = (i + 1) % MOE_BUFS
    far = (i + 2) % MOE_BUFS
    rows = MOE_BLOCK * sub

    def gather(tok_ref, r, dst_slot):
        src = pl.multiple_of(tok_ref[0, 0, r] * sub, sub)
        return pltpu.make_async_copy(f_hbm.at[pl.ds(src, sub), :], xbuf.at[dst_slot, pl.ds(r * sub, sub), :],
                                     gsem.at[dst_slot])

    def wait_gather(s):
        pltpu.make_async_copy(f_hbm.at[pl.ds(0, rows), :], xbuf.at[s], gsem.at[s]).wait()

    @pl.when(i < n_used)
    def _():
        @pl.when(i == 0)
        def _():
            def body(r, carry):
                gather(tok_cur_ref, r, 0).start()
                gather(tok_nxt_ref, r, 1).start()
                return carry
            lax.fori_loop(0, MOE_BLOCK, body, 0)

        @pl.when(jnp.logical_or(i == 0, be_ref[i] != be_ref[jnp.maximum(i - 1, 0)]))
        def _():
            w1b[...] = w1_ref[0, 0].astype(BF16)
            w2b[...] = w2_ref[0, 0].astype(BF16)

        wait_gather(slot)
        x = _load_token_tiles(xbuf.at[slot], MOE_BLOCK, BF16)
        n_chunks = 2
        cw = d_exp // n_chunks
        per = MOE_BLOCK // n_chunks
        y = None
        for c in range(n_chunks):
            for r in range(c * per, (c + 1) * per):
                gather(tok_far_ref, r, far).start()
            glu = _dot(x, w1b[:, c * cw:(c + 1) * cw]) + b1_ref[0, 0, :, c * cw:(c + 1) * cw]
            lin = _dot(x, w1b[:, d_exp + c * cw:d_exp + (c + 1) * cw]) + b1_ref[0, 0, :, d_exp + c * cw:d_exp + (c + 1) * cw]
            glu = jnp.minimum(glu, SWIGLU_LIMIT)
            lin = jnp.clip(lin, -SWIGLU_LIMIT, SWIGLU_LIMIT)
            act = glu * jax.nn.sigmoid(SWIGLU_ALPHA * glu) * (lin + 1.0)
            yc = _dot(act.astype(BF16), w2b[c * cw:(c + 1) * cw, :])
            y = yc if y is None else y + yc
        _store_token_tiles(y_ref, _pack_bf16_pairs(y + b2_ref[0, 0]))

        @pl.when(i == n_used - 1)
        def _():
            wait_gather(nxt)
            wait_gather(far)

    @pl.when(i >= n_used)
    def _():
        y_ref[...] = jnp.zeros_like(y_ref)


def _moe_call(layer, block_e, n_used, slot_tok, f_all, w1, b1, w2, b2):
    n_blocks = block_e.shape[0]
    depth, n_exp, d, two_de = w1.shape
    d_exp = two_de // 2
    sub = d // 2 // LANES
    assert d_exp % (2 * LANES) == 0
    rows = MOE_BLOCK * sub

    def smem(index):
        return pl.BlockSpec((1, 1, MOE_BLOCK), index, memory_space=pltpu.SMEM)

    grid_spec = pltpu.PrefetchScalarGridSpec(
        num_scalar_prefetch=2,
        grid=(n_blocks,),
        in_specs=[smem(lambda i, be, nu: (i, 0, 0)),
                  smem(lambda i, be, nu: (jnp.minimum(i + 1, n_blocks - 1), 0, 0)),
                  smem(lambda i, be, nu: (jnp.minimum(i + 2, n_blocks - 1), 0, 0)),
                  pl.BlockSpec(memory_space=pl.ANY),
                  pl.BlockSpec((1, 1, d, two_de), lambda i, be, nu: (layer, be[i], 0, 0)),
                  pl.BlockSpec((1, 1, 1, two_de), lambda i, be, nu: (layer, be[i], 0, 0)),
                  pl.BlockSpec((1, 1, d_exp, d), lambda i, be, nu: (layer, be[i], 0, 0)),
                  pl.BlockSpec((1, 1, 1, d), lambda i, be, nu: (layer, be[i], 0, 0))],
        out_specs=pl.BlockSpec((rows, LANES), lambda i, be, nu: (i, 0)),
        scratch_shapes=[pltpu.VMEM((MOE_BUFS, rows, LANES), U32),
                        pltpu.VMEM((d, two_de), BF16), pltpu.VMEM((d_exp, d), BF16),
                        pltpu.SemaphoreType.DMA((MOE_BUFS,))],
    )
    vmem = (2 * (d * two_de + d_exp * d) * 4 + (d * two_de + d_exp * d) * 2 + 4 * rows * LANES * 4
            + 6 * MOE_BLOCK * (d + two_de) * 4)
    tok3 = slot_tok.reshape(n_blocks, 1, MOE_BLOCK)
    return pl.pallas_call(
        functools.partial(_moe_kernel, d_exp=d_exp, sub=sub),
        grid_spec=grid_spec,
        out_shape=jax.ShapeDtypeStruct((n_blocks * rows, LANES), U32),
        compiler_params=pltpu.CompilerParams(dimension_semantics=("arbitrary",),
                                             vmem_limit_bytes=int(min(vmem + (4 << 20), VMEM_BUDGET)),
                                             disable_bounds_checks=True),
        name="experts",
    )(block_e, n_used, tok3, tok3, tok3, f_all, w1, b1.reshape(depth, n_exp, 1, two_de), w2,
      b2.reshape(depth, n_exp, 1, d))


def _final_kernel(slot_cur_ref, slot_nxt_ref, x_ref, gate_ref, gt_ref, gfin_ref, y_hbm, o_ref, ybuf, sem):
    t = pl.program_id(0) * pl.num_programs(1) + pl.program_id(1)
    n_t = pl.num_programs(0) * pl.num_programs(1)
    tm, d = x_ref.shape[1], x_ref.shape[2]
    sub = d // 2 // LANES
    cur = t % 2

    def issue(slot_ref, buf):
        def body(r, carry):
            _gather_rows(y_hbm, ybuf, sem, slot_ref, r, buf, sub).start()
            return carry
        lax.fori_loop(0, TOP_K * tm, body, 0)

    @pl.when(t == 0)
    def _():
        issue(slot_cur_ref, 0)

    @pl.when(t + 1 < n_t)
    def _():
        issue(slot_nxt_ref, 1 - cur)

    _wait_rows(y_hbm, ybuf, sem, cur)
    xn = x_ref[0] + gt_ref[0] * _combine_rows(ybuf, cur, gate_ref[0], tm, sub)
    o_ref[0] = xn * lax.rsqrt(jnp.mean(xn * xn, axis=-1, keepdims=True) + RMS_EPS) * gfin_ref[...]


def _final_call(x, slots, gates, gt, g_final, y_sorted, tm):
    b, n_tok, d = x.shape
    nt = n_tok // tm
    sub = d // 2 // LANES
    n_rows = TOP_K * tm
    slots_t = _tile_slots(slots, tm, n_rows)
    n_t = b * nt

    def smem(index):
        return pl.BlockSpec((1, 1, n_rows), index, memory_space=pltpu.SMEM)

    vmem = 4 * tm * d * 4 + 2 * n_rows * sub * LANES * 4 + 2 * tm * LANES * 4 + 4 * tm * d * 4
    return pl.pallas_call(
        _final_kernel,
        grid=(b, nt),
        in_specs=[smem(lambda bb, i: (bb * nt + i, 0, 0)),
                  smem(lambda bb, i: (jnp.minimum(bb * nt + i + 1, n_t - 1), 0, 0)),
                  pl.BlockSpec((1, tm, d), lambda bb, i: (bb, i, 0)),
                  pl.BlockSpec((1, tm, LANES), lambda bb, i: (bb, i, 0)),
                  pl.BlockSpec((1, 1, d), lambda bb, i: (bb, 0, 0)),
                  pl.BlockSpec((1, d), lambda bb, i: (0, 0)),
                  pl.BlockSpec(memory_space=pl.ANY)],
        out_specs=pl.BlockSpec((1, tm, d), lambda bb, i: (bb, i, 0)),
        out_shape=jax.ShapeDtypeStruct((b, n_tok, d), F32),
        scratch_shapes=[pltpu.VMEM((2, n_rows * sub, LANES), U32), pltpu.SemaphoreType.DMA((2,))],
        compiler_params=pltpu.CompilerParams(dimension_semantics=("arbitrary", "arbitrary"),
                                             vmem_limit_bytes=int(min(vmem * 5 // 4 + (4 << 20), VMEM_BUDGET)),
                                             disable_bounds_checks=True),
        name="final",
    )(slots_t, slots_t, x, gates, gt, g_final.reshape(1, d), y_sorted)


def _route(top_e, n_exp):
    n_tok = top_e.shape[0]
    n_asg = n_tok * TOP_K
    flat_e = top_e.reshape(n_asg)
    onehot = (flat_e[:, None] == jnp.arange(n_exp, dtype=I32)[None, :]).astype(I32)
    csum = jnp.cumsum(onehot, axis=0)
    rank = jnp.sum(csum * onehot, axis=1) - 1
    counts = csum[-1]
    padded = (counts + MOE_BLOCK - 1) // MOE_BLOCK * MOE_BLOCK
    pad_ends = jnp.cumsum(padded)
    pad_starts = pad_ends - padded
    slot = jnp.sum(jnp.where(onehot > 0, pad_starts[None, :], 0), axis=1) + rank
    n_blocks = -(-(n_asg + n_exp * (MOE_BLOCK - 1)) // MOE_BLOCK)
    n_slots = n_blocks * MOE_BLOCK
    slot_tok = jnp.zeros((n_slots,), I32).at[slot].set(jnp.arange(n_asg, dtype=I32) // TOP_K)
    block_start = jnp.arange(n_blocks, dtype=I32) * MOE_BLOCK
    block_e = jnp.minimum(jnp.sum((pad_ends[None, :] <= block_start[:, None]).astype(I32), axis=1), n_exp - 1)
    n_used = (pad_ends[-1] // MOE_BLOCK).astype(I32).reshape(1)
    return slot.reshape(n_tok, TOP_K), slot_tok, block_e, n_used


def _rope_tables(seq):
    t = np.arange(seq)
    quarter = HEAD_DIM // 4
    inv = jnp.asarray(ROPE_BASE, F32) ** (-jnp.arange(quarter, dtype=F32) / quarter)
    ang_r = jnp.asarray(t // GRID_W, F32)[:, None] * inv[None, :]
    ang_c = jnp.asarray(t % GRID_W, F32)[:, None] * inv[None, :]
    cos = jnp.concatenate([jnp.cos(ang_r)] * 2 + [jnp.cos(ang_c)] * 2, axis=-1)
    sin = jnp.concatenate([-jnp.sin(ang_r), jnp.sin(ang_r), -jnp.sin(ang_c), jnp.sin(ang_c)], axis=-1)
    reps = LANES // HEAD_DIM
    return jnp.tile(cos, (1, reps)), jnp.tile(sin, (1, reps))


def kernel(x, c, ctx, c_ctx, w_ada, b_ada, g_mix, w_in, na_rpb, sc_conv, wa_sinks, w_na_out, w_sc_out, w_wa_out,
           w_out, g_ffn, w_router, b_router, w_exp_in, b_exp_in, w_exp_out, b_exp_out, g_final):
    b, s, d = x.shape
    l = ctx.shape[1]
    depth = w_ada.shape[0]
    n_exp = w_router.shape[-1]
    rows = s // GRID_W
    off_kv = OFF_GATES + N_BRANCHES * d
    n_cols = off_kv + N_KV_COLS
    assert w_in.shape[-1] == n_cols and s % NA_TQ == 0

    cvec = jnp.zeros((8, d), F32).at[:b].set(c).at[b].set(c_ctx)
    ada = _ada_call(cvec, w_ada, b_ada).reshape(depth, 8, N_ADA, d)
    rope = _rope_tables(s)
    q_scale = HEAD_DIM ** -0.5

    tm_lat = _pick(s, (1024, 512, 256))
    tm_lat_moe = _pick(s, (512, 256))
    tn_proj = _pick(n_cols, (1280, 768, 512, 256))
    tn_kv = _pick(np.gcd(N_KV_COLS, off_kv), (640, 256, 128))
    tm_merge = _pick(s, (256,))
    tn_merge = _pick(d, (512, 256))
    tm_comb = _pick(s, (256,))
    rope_cols = ((OFF_WA_Q, WA_Q_DIM), (off_kv + KV_WA_K, WA_KV_DIM))

    w_perm = jnp.concatenate(
        [w_in[..., REF_OFF_SC:REF_OFF_GATES], w_in[..., REF_OFF_WA_Q:REF_OFF_SC] * q_scale,
         w_in[..., REF_OFF_NA_Q:REF_OFF_WA_Q] * q_scale, w_in[..., REF_OFF_GATES:], w_in[..., :REF_OFF_NA_Q]],
        axis=-1).astype(BF16)
    wr_hi = w_router.astype(BF16)
    wr_lo = (w_router - wr_hi.astype(F32)).astype(BF16)
    w_r2 = jnp.concatenate([wr_hi, wr_lo], axis=-1)
    merge_w = (sc_conv, w_na_out.astype(BF16), w_sc_out.astype(BF16), w_wa_out.astype(BF16), w_out.astype(BF16))
    sub = d // 2 // LANES

    x_lat, x_ctx = x, ctx
    moe_lat = moe_ctx = None
    for layer in range(depth):
        last = layer == depth - 1
        mod_lat = [ada[layer, :b, k][:, None, :] for k in range(N_ADA)]
        mod_ctx = [jnp.broadcast_to(ada[layer, b, k][None, None, :], (b, 1, d)) for k in range(N_ADA)]

        p_lat = _proj_call(x_lat, g_mix, mod_lat[0], mod_lat[1], w_perm, layer, 0, n_cols, rope,
                           tm_lat if moe_lat is None else tm_lat_moe, tn_proj, rope_cols, moe_lat)
        if last:
            p_ctx = _proj_call(x_ctx, g_mix, mod_ctx[0], mod_ctx[1], w_perm, layer, off_kv, N_KV_COLS, None, l,
                               tn_kv, (), moe_ctx)
            ctx_kv = 0
        else:
            p_ctx = _proj_call(x_ctx, g_mix, mod_ctx[0], mod_ctx[1], w_perm, layer, 0, n_cols, None, l, tn_proj, (),
                               moe_ctx)
            ctx_kv = off_kv
        if moe_lat is not None:
            p_lat, x_lat = p_lat
            p_ctx, x_ctx = p_ctx

        bias = _na_bias_tables(na_rpb[layer], rows)
        o_na = _na_call(p_lat, p_ctx, bias, off_kv, ctx_kv)
        o_wa = _wa_call(p_lat, p_ctx, wa_sinks[layer], off_kv, ctx_kv)

        x_lat, f_lat, te_lat, tg_lat = _merge_call(
            x_lat, o_na, o_wa, p_lat, layer, *merge_w, mod_lat[2], g_ffn, mod_lat[3], mod_lat[4],
            w_r2, b_router, tm_merge, tn_merge)
        f_all = f_lat.reshape(b * s * sub, LANES)
        te_all = te_lat.reshape(b * s, LANES)[:, :TOP_K]
        if not last:
            o_na_c, o_wa_c = _ctx_attn_call(p_ctx, wa_sinks[layer], off_kv)
            x_ctx, f_ctx, te_ctx, tg_ctx = _merge_call(
                x_ctx, o_na_c, o_wa_c, p_ctx, layer, *merge_w, mod_ctx[2], g_ffn, mod_ctx[3], mod_ctx[4],
                w_r2, b_router, l, tn_merge)
            f_all = jnp.concatenate([f_all, f_ctx.reshape(b * l * sub, LANES)], axis=0)
            te_all = jnp.concatenate([te_all, te_ctx.reshape(b * l, LANES)[:, :TOP_K]], axis=0)

        slots, slot_tok, block_e, n_used = _route(te_all, n_exp)
        y_sorted = _moe_call(layer, block_e, n_used, slot_tok, f_all, w_exp_in, b_exp_in, w_exp_out, b_exp_out)
        moe_lat = (slots[:b * s].reshape(b, s, TOP_K), tg_lat, mod_lat[5], y_sorted)
        if not last:
            moe_ctx = (slots[b * s:].reshape(b, l, TOP_K), tg_ctx, mod_ctx[5], y_sorted)
    return _final_call(x_lat, *moe_lat[:3], g_final, moe_lat[3], tm_comb)
```

```python
import functools

import numpy as np
import jax
import jax.numpy as jnp
from jax import lax
from jax.experimental import pallas as pl
from jax.experimental.pallas import tpu as pltpu

F32 = jnp.float32
BF16 = jnp.bfloat16
U32 = jnp.uint32
I32 = jnp.int32

GRID_W = 64
HEAD_DIM = 64
NA_HEADS = 8
NA_WIN_ROWS = 8
NA_WIN_COLS = 16
SC_WIDTH = 1024
SC_CONV_WIDTH = 3
WA_Q_HEADS = 8
WA_KV_HEADS = 2
WA_GROUP = WA_Q_HEADS // WA_KV_HEADS
WA_WINDOW = 128
ROPE_BASE = 10000.0
N_BRANCHES = 3
TOP_K = 4
SWIGLU_LIMIT = 7.0
SWIGLU_ALPHA = 1.702
N_ADA = 6
RMS_EPS = 1e-6
NEG_INF = -1e30

NA_DIM = NA_HEADS * HEAD_DIM
WA_Q_DIM = WA_Q_HEADS * HEAD_DIM
WA_KV_DIM = WA_KV_HEADS * HEAD_DIM

LANES = 128
VMEM_BUDGET = 56 * 1024 * 1024

REF_OFF_NA_Q = 2 * NA_DIM + 2 * WA_KV_DIM
REF_OFF_WA_Q = REF_OFF_NA_Q + NA_DIM
REF_OFF_SC = REF_OFF_WA_Q + WA_Q_DIM
REF_OFF_GATES = REF_OFF_SC + 3 * SC_WIDTH

OFF_SC_B = 0
OFF_SC_C = OFF_SC_B + SC_WIDTH
OFF_SC_H = OFF_SC_C + SC_WIDTH
OFF_WA_Q = OFF_SC_H + SC_WIDTH
OFF_NA_Q = OFF_WA_Q + WA_Q_DIM
OFF_GATES = OFF_NA_Q + NA_DIM
KV_NA_K = 0
KV_NA_V = KV_NA_K + NA_DIM
KV_WA_K = KV_NA_V + NA_DIM
KV_WA_V = KV_WA_K + WA_KV_DIM
N_KV_COLS = KV_WA_V + WA_KV_DIM

NA_QROWS = 4
NA_SLAB_ROWS = NA_QROWS + NA_WIN_ROWS - 1
NA_TQ = NA_QROWS * GRID_W
NA_TK = NA_SLAB_ROWS * GRID_W
NA_STEP_HEADS = 4
WA_TQ = 256
WA_TK = WA_TQ + 2 * WA_WINDOW

MOE_BLOCK = 512
MOE_BUFS = 3


def _params(semantics, vmem_bytes):
    limit = int(min(max(vmem_bytes * 5 // 4 + (4 << 20), 32 << 20), VMEM_BUDGET))
    return pltpu.CompilerParams(dimension_semantics=semantics, vmem_limit_bytes=limit)


def _pick(n, candidates):
    for c in candidates:
        if n % c == 0:
            return c
    return n


def _dot(a, b):
    return jnp.dot(a, b, preferred_element_type=F32)


def _dot_nt(a, b):
    return lax.dot_general(a, b, (((1,), (1,)), ((), ())), preferred_element_type=F32)


def _rms_mod(x, g, shift, scale):
    y = x * lax.rsqrt(jnp.mean(x * x, axis=-1, keepdims=True) + RMS_EPS)
    return (y * g) * (1.0 + scale) + shift


def _ada_kernel(c_ref, w_ref, b_ref, o_ref):
    c = c_ref[...]
    s = c * jax.nn.sigmoid(c)
    o_ref[0] = jnp.dot(s, w_ref[0], preferred_element_type=F32,
                       precision=lax.Precision.HIGHEST) + b_ref[0]


def _ada_call(cvec, w_ada, b_ada):
    depth, d, n = w_ada.shape
    tn = _pick(n, (1024, 768, 512, 256, 128))
    return pl.pallas_call(
        _ada_kernel,
        grid=(depth, n // tn),
        in_specs=[pl.BlockSpec((8, d), lambda l, j: (0, 0)),
                  pl.BlockSpec((1, d, tn), lambda l, j: (l, 0, j)),
                  pl.BlockSpec((1, 1, tn), lambda l, j: (l, 0, j))],
        out_specs=pl.BlockSpec((1, 8, tn), lambda l, j: (l, 0, j)),
        out_shape=jax.ShapeDtypeStruct((depth, 8, n), F32),
        compiler_params=_params(("arbitrary", "arbitrary"), 2 * d * tn * 4),
        name="ada",
    )(cvec, w_ada, b_ada.reshape(depth, 1, n))


def _rope128(x, cos, sin):
    lane = lax.broadcasted_iota(I32, x.shape, 1)
    fwd = pltpu.roll(x, LANES - HEAD_DIM // 4, 1)
    bwd = pltpu.roll(x, HEAD_DIM // 4, 1)
    partner = jnp.where((lane & (HEAD_DIM // 4)) == 0, fwd, bwd)
    return x * cos + partner * sin


def _gather_rows(y_hbm, ybuf, sem, slot_ref, r, buf, sub):
    src = pl.multiple_of(slot_ref[0, 0, r] * sub, sub)
    dst = pl.multiple_of(r * sub, sub)
    return pltpu.make_async_copy(y_hbm.at[pl.ds(src, sub), :], ybuf.at[buf, pl.ds(dst, sub), :], sem.at[buf])


def _wait_rows(y_hbm, ybuf, sem, buf):
    pltpu.make_async_copy(y_hbm.at[pl.ds(0, ybuf.shape[1]), :], ybuf.at[buf], sem.at[buf]).wait()


def _combine_rows(ybuf, buf, gates, tm, sub):
    acc = None
    for k in range(TOP_K):
        yk = _load_token_tiles(ybuf.at[buf, pl.ds(k * tm * sub, tm * sub), :], tm, F32) * gates[:, k:k + 1]
        acc = yk if acc is None else acc + yk
    return acc


def _proj_kernel(*refs, rope_tiles, combine, per):
    refs = list(refs)
    if combine:
        slot_cur_ref, slot_nxt_ref, gate_ref, gtf_ref, y_hbm = refs[:5]
        refs = refs[5:]
    x_ref, g_ref, sh_ref, sc_ref = refs[:4]
    refs = refs[4:]
    if rope_tiles:
        cos_ref, sin_ref = refs[:2]
        refs = refs[2:]
    w_ref, o_ref = refs[:2]
    refs = refs[2:]
    if combine:
        xo_ref, h_ref, ybuf, sem = refs
    else:
        (h_ref,) = refs
    j = pl.program_id(2)
    tm, d = x_ref.shape[1], x_ref.shape[2]
    sub = d // 2 // LANES
    if combine:
        t = pl.program_id(0) * pl.num_programs(1) + pl.program_id(1)
        cur = t % 2
        last_step = jnp.logical_and(t == pl.num_programs(0) * pl.num_programs(1) - 1, j == pl.num_programs(2) - 1)

    @pl.when(j == 0)
    def _():
        if combine:
            @pl.when(t == 0)
            def _():
                def body(r, carry):
                    _gather_rows(y_hbm, ybuf, sem, slot_cur_ref, r, 0, sub).start()
                    return carry
                lax.fori_loop(0, ybuf.shape[1] // sub, body, 0)

            _wait_rows(y_hbm, ybuf, sem, cur)
            x = x_ref[0] + gtf_ref[0] * _combine_rows(ybuf, cur, gate_ref[0], tm, sub)
            xo_ref[0] = x
        else:
            x = x_ref[0]
        h_ref[...] = _rms_mod(x, g_ref[0], sh_ref[0], sc_ref[0]).astype(BF16)

    if combine:
        for q in range(per):
            _gather_rows(y_hbm, ybuf, sem, slot_nxt_ref, j * per + q, 1 - cur, sub).start()
    acc = _dot(h_ref[...], w_ref[0])
    o_ref[0] = acc.astype(BF16)
    for jt, lo, width in rope_tiles:
        @pl.when(j == jt)
        def _(lo=lo, width=width):
            cos, sin = cos_ref[...], sin_ref[...]
            for c0 in range(lo, lo + width, LANES):
                o_ref[0, :, c0:c0 + LANES] = _rope128(acc[:, c0:c0 + LANES], cos, sin).astype(BF16)
    if combine:
        @pl.when(last_step)
        def _():
            _wait_rows(y_hbm, ybuf, sem, 1 - cur)


def _tile_slots(slots, tm, n_rows):
    b, n, _ = slots.shape
    st = slots.reshape(b, n // tm, tm, TOP_K).transpose(0, 1, 3, 2).reshape(b * (n // tm), 1, TOP_K * tm)
    return jnp.pad(st, ((0, 0), (0, 0), (0, n_rows - TOP_K * tm)))


def _proj_call(x, g, shift, scale, w_all, layer, col0, nc, rope, tm, tn, rope_cols, moe=None):
    b, n, d = x.shape
    assert col0 % tn == 0 and nc % tn == 0 and n % tm == 0
    jb = col0 // tn
    ni, nj = n // tm, nc // tn
    sub = d // 2 // LANES
    rope_tiles = []
    if rope is not None:
        for off, width in rope_cols:
            assert off // tn == (off + width - 1) // tn and off % LANES == 0 and width % LANES == 0
            rope_tiles.append((off // tn, off % tn, width))
    in_specs, args, per = [], [], 0
    vmem = 2 * tm * d * 4 + tm * d * 2 + 2 * d * tn * 2 + 2 * tm * tn * 2 + tm * tn * 4 + 4 * tm * LANES * 4
    if moe is not None:
        slots, gates, gt, y_sorted = moe
        per = -(-TOP_K * tm // nj)
        n_rows = per * nj
        slots_t = _tile_slots(slots, tm, n_rows)
        n_t = b * ni

        def smem(index):
            return pl.BlockSpec((1, 1, n_rows), index, memory_space=pltpu.SMEM)

        in_specs += [smem(lambda bb, i, j: (bb * ni + i, 0, 0)),
                     smem(lambda bb, i, j: (jnp.minimum(bb * ni + i + 1, n_t - 1), 0, 0)),
                     pl.BlockSpec((1, tm, LANES), lambda bb, i, j: (bb, i, 0)),
                     pl.BlockSpec((1, 1, d), lambda bb, i, j: (bb, 0, 0)),
                     pl.BlockSpec(memory_space=pl.ANY)]
        args += [slots_t, slots_t, gates, gt, y_sorted]
        vmem += 2 * tm * d * 4 + 2 * n_rows * sub * LANES * 4 + 3 * tm * d * 4
    in_specs += [pl.BlockSpec((1, tm, d), lambda bb, i, j: (bb, i, 0)),
                 pl.BlockSpec((1, 1, d), lambda bb, i, j: (layer, 0, 0)),
                 pl.BlockSpec((1, 1, d), lambda bb, i, j: (bb, 0, 0)),
                 pl.BlockSpec((1, 1, d), lambda bb, i, j: (bb, 0, 0))]
    args += [x, g.reshape(g.shape[0], 1, d), shift, scale]
    if rope is not None:
        in_specs += [pl.BlockSpec((tm, LANES), lambda bb, i, j: (i, 0))] * 2
        args += list(rope)
    in_specs.append(pl.BlockSpec((1, d, tn), lambda bb, i, j: (layer, 0, jb + j)))
    args.append(w_all)
    out_specs = [pl.BlockSpec((1, tm, tn), lambda bb, i, j: (bb, i, j))]
    out_shape = [jax.ShapeDtypeStruct((b, n, nc), BF16)]
    scratch = [pltpu.VMEM((tm, d), BF16)]
    if moe is not None:
        out_specs.append(pl.BlockSpec((1, tm, d), lambda bb, i, j: (bb, i, 0)))
        out_shape.append(jax.ShapeDtypeStruct((b, n, d), F32))
        scratch += [pltpu.VMEM((2, n_rows * sub, LANES), U32), pltpu.SemaphoreType.DMA((2,))]
    limit = int(min(max(vmem * 5 // 4 + (4 << 20), 32 << 20), VMEM_BUDGET))
    out = pl.pallas_call(
        functools.partial(_proj_kernel, rope_tiles=tuple(rope_tiles), combine=moe is not None, per=per),
        grid=(b, ni, nj),
        in_specs=in_specs,
        out_specs=out_specs,
        out_shape=out_shape,
        scratch_shapes=scratch,
        compiler_params=pltpu.CompilerParams(dimension_semantics=("arbitrary", "arbitrary", "arbitrary"),
                                             vmem_limit_bytes=limit, disable_bounds_checks=moe is not None),
        name="proj",
    )(*args)
    return out if moe is not None else out[0]


def _softmax_av(parts, extra=None):
    m = functools.reduce(jnp.maximum, [jnp.max(s, axis=-1, keepdims=True) for s, _ in parts])
    if extra is not None:
        m = jnp.maximum(m, extra)
    l = 0.0 if extra is None else jnp.exp(extra - m)
    o = None
    for s, v in parts:
        p = jnp.exp(s - m)
        l = l + jnp.sum(p, axis=-1, keepdims=True)
        pv = _dot(p.astype(BF16), v)
        o = pv if o is None else o + pv
    return o / l


def _na_kernel(q_ref, k_ref, v_ref, kc_ref, vc_ref, bias_ref, o_ref, *, rows):
    i = pl.program_id(2)
    r_start = jnp.clip(NA_QROWS * i - NA_WIN_ROWS // 2, 0, rows - NA_SLAB_ROWS)
    start = pl.multiple_of(r_start * GRID_W, GRID_W)
    kslab = k_ref[0, pl.ds(start, NA_TK), :]
    vslab = v_ref[0, pl.ds(start, NA_TK), :]
    q, kc, vc = q_ref[0], kc_ref[0], vc_ref[0]
    outs = []
    for hh in range(NA_STEP_HEADS):
        sl = slice(hh * HEAD_DIM, (hh + 1) * HEAD_DIM)
        qh = q[:, sl]
        s_nb = _dot_nt(qh, kslab[:, sl]) + bias_ref[0, hh]
        s_ctx = _dot_nt(qh, kc[:, sl])
        outs.append(_softmax_av([(s_nb, vslab[:, sl]), (s_ctx, vc[:, sl])]))
    o_ref[0] = jnp.concatenate(outs, axis=1).astype(BF16)


def _na_bias_tables(rpb, rows):
    nblk = rows // NA_QROWS
    assert nblk >= 3 and rows >= NA_SLAB_ROWS
    n_heads, n_dr, n_dc = rpb.shape
    half = NA_WIN_COLS - 1
    vec = jnp.zeros((n_heads, n_dr, LANES), F32)
    vec = vec.at[..., :n_dc - half].set(rpb[..., half:]).at[..., LANES - half:].set(rpb[..., :half])
    toep = jnp.tile(vec, (1, 1, GRID_W))[..., :GRID_W * (LANES - 1)]
    toep = toep.reshape(n_heads, n_dr, GRID_W, LANES - 1)[..., :GRID_W]
    col = np.arange(GRID_W)
    c0 = np.clip(col - NA_WIN_COLS // 2, 0, GRID_W - NA_WIN_COLS)
    col_ok = (col[None, :] >= c0[:, None]) & (col[None, :] < c0[:, None] + NA_WIN_COLS)
    toep = jnp.where(col_ok[None, None], toep, NEG_INF)
    masked = jnp.full((n_heads, GRID_W, GRID_W), NEG_INF, F32)
    cases = []
    for i in (0, 1, nblk - 1):
        rs = int(np.clip(NA_QROWS * i - NA_WIN_ROWS // 2, 0, rows - NA_SLAB_ROWS))
        q_rows = []
        for rl in range(NA_QROWS):
            r = NA_QROWS * i + rl
            r0 = int(np.clip(r - NA_WIN_ROWS // 2, 0, rows - NA_WIN_ROWS))
            tiles = []
            for kl in range(NA_SLAB_ROWS):
                kr = rs + kl
                tiles.append(toep[:, kr - r + NA_WIN_ROWS - 1] if r0 <= kr < r0 + NA_WIN_ROWS else masked)
            q_rows.append(jnp.concatenate(tiles, axis=2))
        cases.append(jnp.concatenate(q_rows, axis=1))
    return jnp.stack(cases)


def _na_call(p_lat, p_ctx, bias, kv_off_lat, kv_off_ctx):
    b, s, _ = p_lat.shape
    l = p_ctx.shape[1]
    rows = s // GRID_W
    nblk = rows // NA_QROWS
    w = NA_STEP_HEADS * HEAD_DIM
    hp = NA_DIM // w
    assert OFF_NA_Q % w == 0 and kv_off_lat % w == 0 and kv_off_ctx % w == 0
    qb, kb, vb = OFF_NA_Q // w, (kv_off_lat + KV_NA_K) // w, (kv_off_lat + KV_NA_V) // w
    kcb, vcb = (kv_off_ctx + KV_NA_K) // w, (kv_off_ctx + KV_NA_V) // w

    def case(i):
        return jnp.where(i == 0, 0, jnp.where(i == nblk - 1, 2, 1))

    vmem = 4 * s * w * 2 + 2 * NA_STEP_HEADS * NA_TQ * NA_TK * 4 + 8 * NA_TQ * (NA_TK + l) * 4
    return pl.pallas_call(
        functools.partial(_na_kernel, rows=rows),
        grid=(b, hp, nblk),
        in_specs=[pl.BlockSpec((1, NA_TQ, w), lambda bb, h, i: (bb, i, qb + h)),
                  pl.BlockSpec((1, s, w), lambda bb, h, i: (bb, 0, kb + h)),
                  pl.BlockSpec((1, s, w), lambda bb, h, i: (bb, 0, vb + h)),
                  pl.BlockSpec((1, l, w), lambda bb, h, i: (bb, 0, kcb + h)),
                  pl.BlockSpec((1, l, w), lambda bb, h, i: (bb, 0, vcb + h)),
                  pl.BlockSpec((1, NA_STEP_HEADS, NA_TQ, NA_TK), lambda bb, h, i: (case(i), h, 0, 0))],
        out_specs=pl.BlockSpec((1, NA_TQ, w), lambda bb, h, i: (bb, i, h)),
        out_shape=jax.ShapeDtypeStruct((b, s, NA_DIM), BF16),
        compiler_params=_params(("arbitrary", "arbitrary", "arbitrary"), vmem),
        name="natten",
    )(p_lat, p_lat, p_lat, p_ctx, p_ctx, bias)


def _wa_kernel(sink_ref, q_ref, k_ref, v_ref, kc_ref, vc_ref, o_ref, *, seq):
    i = pl.program_id(1)
    start = pl.multiple_of(jnp.clip(WA_TQ * i - WA_WINDOW, 0, seq - WA_TK), WA_WINDOW)
    kslab = k_ref[0, pl.ds(start, WA_TK), :]
    vslab = v_ref[0, pl.ds(start, WA_TK), :]
    qpos = WA_TQ * i + lax.broadcasted_iota(I32, (WA_TQ, WA_TK), 0)
    kpos = start + lax.broadcasted_iota(I32, (WA_TQ, WA_TK), 1)
    band = jnp.where(jnp.abs(kpos - qpos) <= WA_WINDOW, 0.0, NEG_INF).astype(F32)
    q, kc, vc = q_ref[0], kc_ref[0], vc_ref[0]
    outs = []
    for hq in range(WA_Q_HEADS):
        hk = hq // WA_GROUP
        sk = slice(hk * HEAD_DIM, (hk + 1) * HEAD_DIM)
        qh = q[:, hq * HEAD_DIM:(hq + 1) * HEAD_DIM]
        s_loc = _dot_nt(qh, kslab[:, sk]) + band
        s_ctx = _dot_nt(qh, kc[:, sk])
        outs.append(_softmax_av([(s_loc, vslab[:, sk]), (s_ctx, vc[:, sk])], extra=sink_ref[hq]))
    o_ref[0] = jnp.concatenate(outs, axis=1).astype(BF16)


def _wa_call(p_lat, p_ctx, sinks, kv_off_lat, kv_off_ctx):
    b, s, _ = p_lat.shape
    l = p_ctx.shape[1]
    assert s % WA_TQ == 0 and s >= WA_TK
    qb = OFF_WA_Q // WA_Q_DIM
    kb, vb = (kv_off_lat + KV_WA_K) // LANES, (kv_off_lat + KV_WA_V) // LANES
    kcb, vcb = (kv_off_ctx + KV_WA_K) // LANES, (kv_off_ctx + KV_WA_V) // LANES
    vmem = 4 * s * LANES * 2 + 4 * WA_TQ * WA_Q_DIM * 2 + 10 * WA_TQ * (WA_TK + l) * 4
    return pl.pallas_call(
        functools.partial(_wa_kernel, seq=s),
        grid=(b, s // WA_TQ),
        in_specs=[pl.BlockSpec(memory_space=pltpu.SMEM),
                  pl.BlockSpec((1, WA_TQ, WA_Q_DIM), lambda bb, i: (bb, i, qb)),
                  pl.BlockSpec((1, s, LANES), lambda bb, i: (bb, 0, kb)),
                  pl.BlockSpec((1, s, LANES), lambda bb, i: (bb, 0, vb)),
                  pl.BlockSpec((1, l, LANES), lambda bb, i: (bb, 0, kcb)),
                  pl.BlockSpec((1, l, LANES), lambda bb, i: (bb, 0, vcb))],
        out_specs=pl.BlockSpec((1, WA_TQ, WA_Q_DIM), lambda bb, i: (bb, i, 0)),
        out_shape=jax.ShapeDtypeStruct((b, s, WA_Q_DIM), BF16),
        compiler_params=_params(("arbitrary", "arbitrary"), vmem),
        name="winattn",
    )(sinks, p_lat, p_lat, p_lat, p_ctx, p_ctx)


def _ctx_attn_kernel(sink_ref, naq_ref, nak_ref, nav_ref, waq_ref, wak_ref, wav_ref, ona_ref, owa_ref):
    q, k, v = naq_ref[0], nak_ref[0], nav_ref[0]
    outs = []
    for h in range(NA_HEADS):
        sl = slice(h * HEAD_DIM, (h + 1) * HEAD_DIM)
        outs.append(_softmax_av([(_dot_nt(q[:, sl], k[:, sl]), v[:, sl])]))
    ona_ref[0] = jnp.concatenate(outs, axis=1).astype(BF16)
    q, k, v = waq_ref[0], wak_ref[0], wav_ref[0]
    outs = []
    for hq in range(WA_Q_HEADS):
        sk = slice((hq // WA_GROUP) * HEAD_DIM, (hq // WA_GROUP + 1) * HEAD_DIM)
        qh = q[:, hq * HEAD_DIM:(hq + 1) * HEAD_DIM]
        outs.append(_softmax_av([(_dot_nt(qh, k[:, sk]), v[:, sk])], extra=sink_ref[hq]))
    owa_ref[0] = jnp.concatenate(outs, axis=1).astype(BF16)


def _ctx_attn_call(p_ctx, sinks, kv_off):
    b, l, _ = p_ctx.shape

    def spec(width, off):
        return pl.BlockSpec((1, l, width), lambda bb: (bb, 0, off // width))

    return pl.pallas_call(
        _ctx_attn_kernel,
        grid=(b,),
        in_specs=[pl.BlockSpec(memory_space=pltpu.SMEM),
                  spec(NA_DIM, OFF_NA_Q), spec(NA_DIM, kv_off + KV_NA_K), spec(NA_DIM, kv_off + KV_NA_V),
                  spec(WA_Q_DIM, OFF_WA_Q), spec(WA_KV_DIM, kv_off + KV_WA_K), spec(WA_KV_DIM, kv_off + KV_WA_V)],
        out_specs=[pl.BlockSpec((1, l, NA_DIM), lambda bb: (bb, 0, 0)),
                   pl.BlockSpec((1, l, WA_Q_DIM), lambda bb: (bb, 0, 0))],
        out_shape=[jax.ShapeDtypeStruct((b, l, NA_DIM), BF16), jax.ShapeDtypeStruct((b, l, WA_Q_DIM), BF16)],
        compiler_params=_params(("arbitrary",), 16 * l * NA_DIM * 4),
        name="ctxattn",
    )(sinks, p_ctx, p_ctx, p_ctx, p_ctx, p_ctx, p_ctx)


def _pack_bf16_pairs(x):
    n = x.shape[1] // 2
    lo = pltpu.bitcast(x[:, :n].astype(BF16).astype(F32), U32)
    hi = pltpu.bitcast(x[:, n:].astype(BF16).astype(F32), U32)
    return (hi & jnp.uint32(0xFFFF0000)) | (lo >> 16)


def _store_token_tiles(ref, packed):
    m, n = packed.shape
    sub = n // LANES
    for s in range(sub):
        ref[pl.ds(s, m, stride=sub), :] = packed[:, s * LANES:(s + 1) * LANES]


def _load_token_tiles(ref, m, dtype):
    sub = ref.shape[0] // m
    chunks = [ref[pl.ds(s, m, stride=sub), :] for s in range(sub)]
    lo = [pltpu.bitcast(p << 16, F32).astype(dtype) for p in chunks]
    hi = [pltpu.bitcast(p & jnp.uint32(0xFFFF0000), F32).astype(dtype) for p in chunks]
    return jnp.concatenate(lo + hi, axis=1)


def _top_k_lanes(logits, n_exp):
    m_rows = logits.shape[0]
    col = lax.broadcasted_iota(I32, (m_rows, n_exp), 1).astype(F32)
    lane = lax.broadcasted_iota(I32, (m_rows, LANES), 1)
    work = logits
    vals = jnp.zeros((m_rows, LANES), F32)
    idxs = jnp.zeros((m_rows, LANES), F32)
    top = None
    denom = 0.0
    for k in range(TOP_K):
        mx = jnp.max(work, axis=-1, keepdims=True)
        ix = jnp.min(jnp.where(work == mx, col, float(n_exp)), axis=-1, keepdims=True)
        work = jnp.where(col == ix, -jnp.inf, work)
        top = mx if top is None else top
        e = jnp.exp(mx - top)
        denom = denom + e
        vals = jnp.where(lane == k, e, vals)
        idxs = jnp.where(lane == k, ix, idxs)
    return vals / denom, idxs.astype(I32)


def _merge_kernel(*refs, has_halo, tn):
    (ona_ref, owa_ref, scb_ref, scc_ref, sch_ref) = refs[:5]
    refs = refs[5:]
    if has_halo:
        cprev_ref, hprev_ref, cnext_ref, hnext_ref = refs[:4]
        refs = refs[4:]
    gna_ref, gsc_ref, gwa_ref, conv_ref, wna_ref, wsc_ref, wwa_ref, m_ref = refs
    i = pl.program_id(1)
    tm, d = m_ref.shape[1], m_ref.shape[2]

    u = scc_ref[0].astype(F32) * sch_ref[0].astype(F32)
    zero = jnp.zeros((1, SC_WIDTH), F32)
    if has_halo:
        u_prev = jnp.where(i == 0, zero, cprev_ref[0, 7:8, :].astype(F32) * hprev_ref[0, 7:8, :].astype(F32))
        u_next = jnp.where(i == pl.num_programs(1) - 1, zero,
                           cnext_ref[0, 0:1, :].astype(F32) * hnext_ref[0, 0:1, :].astype(F32))
    else:
        u_prev = u_next = zero
    row = lax.broadcasted_iota(I32, u.shape, 0)
    u_m1 = jnp.where(row == 0, u_prev, pltpu.roll(u, 1, 0))
    u_p1 = jnp.where(row == tm - 1, u_next, pltpu.roll(u, tm - 1, 0))
    y = u_m1 * conv_ref[0, 0:1, :] + u * conv_ref[0, 1:2, :] + u_p1 * conv_ref[0, 2:3, :]
    o_sc = (scb_ref[0].astype(F32) * y).astype(BF16)
    o_na, o_wa = ona_ref[0], owa_ref[0]

    def gate(ref, cs):
        return jax.nn.sigmoid(ref[0, :, cs].astype(F32))

    for n in range(d // tn):
        cs = slice(n * tn, (n + 1) * tn)
        merged = (gate(gna_ref, cs) * _dot(o_na, wna_ref[0, :, cs])
                  + gate(gsc_ref, cs) * _dot(o_sc, wsc_ref[0, :, cs])
                  + gate(gwa_ref, cs) * _dot(o_wa, wwa_ref[0, :, cs]))
        m_ref[0, :, cs] = merged.astype(BF16)


def _outproj_kernel(x_ref, m_ref, wout_ref, gt_ref, gffn_ref, shf_ref, scf_ref, wr_ref, br_ref,
                    xo_ref, f_ref, te_ref, tg_ref, *, n_exp):
    xn = x_ref[0] + gt_ref[0] * _dot(m_ref[0], wout_ref[0])
    xo_ref[0] = xn
    f = _rms_mod(xn, gffn_ref[0], shf_ref[0], scf_ref[0])
    _store_token_tiles(f_ref.at[0], _pack_bf16_pairs(f))
    f_hi = f.astype(BF16)
    f_lo = (f - f_hi.astype(F32)).astype(BF16)
    wr = wr_ref[0]
    hi_both = _dot(f_hi, wr)
    logits = (hi_both[:, :n_exp] + _dot(f_lo, wr[:, :n_exp]) + hi_both[:, n_exp:]) + br_ref[0]
    gates, idx = _top_k_lanes(logits, n_exp)
    tg_ref[0] = gates
    te_ref[0] = idx


def _merge_call(x, o_na, o_wa, p, layer, conv_w, w_na, w_sc, w_wa, w_out, gt, g_ffn, sh_f, sc_f, w_r2, b_r, tm, tn):
    b, n_tok, d = x.shape
    depth = w_out.shape[0]
    n_exp = b_r.shape[-1]
    sub = d // 2 // LANES
    nt = n_tok // tm
    has_halo = nt > 1
    assert OFF_GATES % d == 0 and d % tn == 0 and n_tok % tm == 0 and tm % 8 == 0
    gate_b = OFF_GATES // d
    rb = tm // 8

    def rows(width, off):
        return pl.BlockSpec((1, tm, width), lambda bb, i: (bb, i, off // width))

    def resident(shape):
        return pl.BlockSpec((1,) + shape, lambda bb, i: (layer, 0, 0), pipeline_mode=pl.Buffered(1))

    in_specs = [pl.BlockSpec((1, tm, NA_DIM), lambda bb, i: (bb, i, 0)),
                pl.BlockSpec((1, tm, WA_Q_DIM), lambda bb, i: (bb, i, 0)),
                rows(SC_WIDTH, OFF_SC_B), rows(SC_WIDTH, OFF_SC_C), rows(SC_WIDTH, OFF_SC_H)]
    args = [o_na, o_wa, p, p, p]
    if has_halo:
        last8 = n_tok // 8 - 1
        for off in (OFF_SC_C, OFF_SC_H):
            in_specs.append(pl.BlockSpec((1, 8, SC_WIDTH),
                                         lambda bb, i, off=off: (bb, jnp.maximum(i * rb - 1, 0), off // SC_WIDTH)))
        for off in (OFF_SC_C, OFF_SC_H):
            in_specs.append(pl.BlockSpec((1, 8, SC_WIDTH),
                                         lambda bb, i, off=off: (bb, jnp.minimum((i + 1) * rb, last8), off // SC_WIDTH)))
        args += [p, p, p, p]
    for br in range(N_BRANCHES):
        in_specs.append(pl.BlockSpec((1, tm, d), lambda bb, i, br=br: (bb, i, gate_b + br)))
        args.append(p)
    in_specs += [resident((SC_CONV_WIDTH, SC_WIDTH)),
                 resident((NA_DIM, d)), resident((SC_WIDTH, d)), resident((WA_Q_DIM, d))]
    args += [conv_w, w_na, w_sc, w_wa]
    vmem = (2 * tm * (NA_DIM + WA_Q_DIM + 3 * SC_WIDTH + N_BRANCHES * d + d) * 2
            + (NA_DIM + SC_WIDTH + WA_Q_DIM) * d * 2 + 4 * tm * SC_WIDTH * 4 + 3 * tm * tn * 4)
    merged = pl.pallas_call(
        functools.partial(_merge_kernel, has_halo=has_halo, tn=tn),
        grid=(b, nt),
        in_specs=in_specs,
        out_specs=pl.BlockSpec((1, tm, d), lambda bb, i: (bb, i, 0)),
        out_shape=jax.ShapeDtypeStruct((b, n_tok, d), BF16),
        compiler_params=_params(("arbitrary", "arbitrary"), vmem),
        name="merge",
    )(*args)

    def per_batch():
        return pl.BlockSpec((1, 1, d), lambda bb, i: (bb, 0, 0))

    out_specs = [pl.BlockSpec((1, tm, d), lambda bb, i: (bb, i, 0)),
                 pl.BlockSpec((1, tm * sub, LANES), lambda bb, i: (bb, i, 0)),
                 pl.BlockSpec((1, tm, LANES), lambda bb, i: (bb, i, 0)),
                 pl.BlockSpec((1, tm, LANES), lambda bb, i: (bb, i, 0))]
    out_shape = [jax.ShapeDtypeStruct((b, n_tok, d), F32), jax.ShapeDtypeStruct((b, n_tok * sub, LANES), U32),
                 jax.ShapeDtypeStruct((b, n_tok, LANES), I32), jax.ShapeDtypeStruct((b, n_tok, LANES), F32)]
    vmem = (4 * tm * d * 4 + 2 * tm * d * 2 + 2 * tm * (d // 2) * 4 + 4 * tm * LANES * 4
            + d * d * 2 + d * 2 * n_exp * 4 + 4 * tm * d * 4)
    return pl.pallas_call(
        functools.partial(_outproj_kernel, n_exp=n_exp),
        grid=(b, nt),
        in_specs=[pl.BlockSpec((1, tm, d), lambda bb, i: (bb, i, 0)),
                  pl.BlockSpec((1, tm, d), lambda bb, i: (bb, i, 0)),
                  resident((d, d)), per_batch(), resident((1, d)), per_batch(), per_batch(),
                  resident((d, 2 * n_exp)), resident((1, n_exp))],
        out_specs=out_specs,
        out_shape=out_shape,
        compiler_params=_params(("arbitrary", "arbitrary"), vmem),
        name="outproj",
    )(x, merged, w_out, gt, g_ffn.reshape(depth, 1, d), sh_f, sc_f, w_r2, b_r.reshape(depth, 1, n_exp))


def _moe_kernel(be_ref, nused_ref, tok_cur_ref, tok_nxt_ref, tok_far_ref, f_hbm, w1_ref, b1_ref, w2_ref, b2_ref,
                y_ref, xbuf, w1b, w2b, gsem, *, d_exp, sub):
    i = pl.program_id(0)
    n_used = nused_ref[0]
    slot = i % MOE_BUFS
    nxt = (i + 1) % MOE_BUFS
    far = (i + 2) % MOE_BUFS
    rows = MOE_BLOCK * sub

    def gather(tok_ref, r, dst_slot):
        src = pl.multiple_of(tok_ref[0, 0, r] * sub, sub)
        return pltpu.make_async_copy(f_hbm.at[pl.ds(src, sub), :], xbuf.at[dst_slot, pl.ds(r * sub, sub), :],
                                     gsem.at[dst_slot])

    def wait_gather(s):
        pltpu.make_async_copy(f_hbm.at[pl.ds(0, rows), :], xbuf.at[s], gsem.at[s]).wait()

    @pl.when(i < n_used)
    def _():
        @pl.when(i == 0)
        def _():
            def body(r, carry):
                gather(tok_cur_ref, r, 0).start()
                gather(tok_nxt_ref, r, 1).start()
                return carry
            lax.fori_loop(0, MOE_BLOCK, body, 0)

        @pl.when(jnp.logical_or(i == 0, be_ref[i] != be_ref[jnp.maximum(i - 1, 0)]))
        def _():
            w1b[...] = w1_ref[0, 0].astype(BF16)
            w2b[...] = w2_ref[0, 0].astype(BF16)

        wait_gather(slot)
        x = _load_token_tiles(xbuf.at[slot], MOE_BLOCK, BF16)
        n_chunks = 2
        cw = d_exp // n_chunks
        per = MOE_BLOCK // n_chunks
        y = None
        for c in range(n_chunks):
            for r in range(c * per, (c + 1) * per):
                gather(tok_far_ref, r, far).start()
            glu = _dot(x, w1b[:, c * cw:(c + 1) * cw]) + b1_ref[0, 0, :, c * cw:(c + 1) * cw]
            lin = _dot(x, w1b[:, d_exp + c * cw:d_exp + (c + 1) * cw]) + b1_ref[0, 0, :, d_exp + c * cw:d_exp + (c + 1) * cw]
            glu = jnp.minimum(glu, SWIGLU_LIMIT)
            lin = jnp.clip(lin, -SWIGLU_LIMIT, SWIGLU_LIMIT)
            act = glu * jax.nn.sigmoid(SWIGLU_ALPHA * glu) * (lin + 1.0)
            yc = _dot(act.astype(BF16), w2b[c * cw:(c + 1) * cw, :])
            y = yc if y is None else y + yc
        _store_token_tiles(y_ref, _pack_bf16_pairs(y + b2_ref[0, 0]))

        @pl.when(i == n_used - 1)
        def _():
            wait_gather(nxt)
            wait_gather(far)

    @pl.when(i >= n_used)
    def _():
        y_ref[...] = jnp.zeros_like(y_ref)


def _moe_call(layer, block_e, n_used, slot_tok, f_all, w1, b1, w2, b2):
    n_blocks = block_e.shape[0]
    depth, n_exp, d, two_de = w1.shape
    d_exp = two_de // 2
    sub = d // 2 // LANES
    assert d_exp % (2 * LANES) == 0
    rows = MOE_BLOCK * sub

    def smem(index):
        return pl.BlockSpec((1, 1, MOE_BLOCK), index, memory_space=pltpu.SMEM)

    grid_spec = pltpu.PrefetchScalarGridSpec(
        num_scalar_prefetch=2,
        grid=(n_blocks,),
        in_specs=[smem(lambda i, be, nu: (i, 0, 0)),
                  smem(lambda i, be, nu: (jnp.minimum(i + 1, n_blocks - 1), 0, 0)),
                  smem(lambda i, be, nu: (jnp.minimum(i + 2, n_blocks - 1), 0, 0)),
                  pl.BlockSpec(memory_space=pl.ANY),
                  pl.BlockSpec((1, 1, d, two_de), lambda i, be, nu: (layer, be[i], 0, 0)),
                  pl.BlockSpec((1, 1, 1, two_de), lambda i, be, nu: (layer, be[i], 0, 0)),
                  pl.BlockSpec((1, 1, d_exp, d), lambda i, be, nu: (layer, be[i], 0, 0)),
                  pl.BlockSpec((1, 1, 1, d), lambda i, be, nu: (layer, be[i], 0, 0))],
        out_specs=pl.BlockSpec((rows, LANES), lambda i, be, nu: (i, 0)),
        scratch_shapes=[pltpu.VMEM((MOE_BUFS, rows, LANES), U32),
                        pltpu.VMEM((d, two_de), BF16), pltpu.VMEM((d_exp, d), BF16),
                        pltpu.SemaphoreType.DMA((MOE_BUFS,))],
    )
    vmem = (2 * (d * two_de + d_exp * d) * 4 + (d * two_de + d_exp * d) * 2 + 4 * rows * LANES * 4
            + 6 * MOE_BLOCK * (d + two_de) * 4)
    tok3 = slot_tok.reshape(n_blocks, 1, MOE_BLOCK)
    return pl.pallas_call(
        functools.partial(_moe_kernel, d_exp=d_exp, sub=sub),
        grid_spec=grid_spec,
        out_shape=jax.ShapeDtypeStruct((n_blocks * rows, LANES), U32),
        compiler_params=pltpu.CompilerParams(dimension_semantics=("arbitrary",),
                                             vmem_limit_bytes=int(min(vmem + (4 << 20), VMEM_BUDGET)),
                                             disable_bounds_checks=True),
        name="experts",
    )(block_e, n_used, tok3, tok3, tok3, f_all, w1, b1.reshape(depth, n_exp, 1, two_de), w2,
      b2.reshape(depth, n_exp, 1, d))


def _final_kernel(slot_cur_ref, slot_nxt_ref, x_ref, gate_ref, gt_ref, gfin_ref, y_hbm, o_ref, ybuf, sem):
    t = pl.program_id(0) * pl.num_programs(1) + pl.program_id(1)
    n_t = pl.num_programs(0) * pl.num_programs(1)
    tm, d = x_ref.shape[1], x_ref.shape[2]
    sub = d // 2 // LANES
    cur = t % 2

    def issue(slot_ref, buf):
        def body(r, carry):
            _gather_rows(y_hbm, ybuf, sem, slot_ref, r, buf, sub).start()
            return carry
        lax.fori_loop(0, TOP_K * tm, body, 0)

    @pl.when(t == 0)
    def _():
        issue(slot_cur_ref, 0)

    @pl.when(t + 1 < n_t)
    def _():
        issue(slot_nxt_ref, 1 - cur)

    _wait_rows(y_hbm, ybuf, sem, cur)
    xn = x_ref[0] + gt_ref[0] * _combine_rows(ybuf, cur, gate_ref[0], tm, sub)
    o_ref[0] = xn * lax.rsqrt(jnp.mean(xn * xn, axis=-1, keepdims=True) + RMS_EPS) * gfin_ref[...]


def _final_call(x, slots, gates, gt, g_final, y_sorted, tm):
    b, n_tok, d = x.shape
    nt = n_tok // tm
    sub = d // 2 // LANES
    n_rows = TOP_K * tm
    slots_t = _tile_slots(slots, tm, n_rows)
    n_t = b * nt

    def smem(index):
        return pl.BlockSpec((1, 1, n_rows), index, memory_space=pltpu.SMEM)

    vmem = 4 * tm * d * 4 + 2 * n_rows * sub * LANES * 4 + 2 * tm * LANES * 4 + 4 * tm * d * 4
    return pl.pallas_call(
        _final_kernel,
        grid=(b, nt),
        in_specs=[smem(lambda bb, i: (bb * nt + i, 0, 0)),
                  smem(lambda bb, i: (jnp.minimum(bb * nt + i + 1, n_t - 1), 0, 0)),
                  pl.BlockSpec((1, tm, d), lambda bb, i: (bb, i, 0)),
                  pl.BlockSpec((1, tm, LANES), lambda bb, i: (bb, i, 0)),
                  pl.BlockSpec((1, 1, d), lambda bb, i: (bb, 0, 0)),
                  pl.BlockSpec((1, d), lambda bb, i: (0, 0)),
                  pl.BlockSpec(memory_space=pl.ANY)],
        out_specs=pl.BlockSpec((1, tm, d), lambda bb, i: (bb, i, 0)),
        out_shape=jax.ShapeDtypeStruct((b, n_tok, d), F32),
        scratch_shapes=[pltpu.VMEM((2, n_rows * sub, LANES), U32), pltpu.SemaphoreType.DMA((2,))],
        compiler_params=pltpu.CompilerParams(dimension_semantics=("arbitrary", "arbitrary"),
                                             vmem_limit_bytes=int(min(vmem * 5 // 4 + (4 << 20), VMEM_BUDGET)),
                                             disable_bounds_checks=True),
        name="final",
    )(slots_t, slots_t, x, gates, gt, g_final.reshape(1, d), y_sorted)


def _route(top_e, n_exp):
    n_tok = top_e.shape[0]
    n_asg = n_tok * TOP_K
    flat_e = top_e.reshape(n_asg)
    onehot = (flat_e[:, None] == jnp.arange(n_exp, dtype=I32)[None, :]).astype(I32)
    csum = jnp.cumsum(onehot, axis=0)
    rank = jnp.sum(csum * onehot, axis=1) - 1
    counts = csum[-1]
    padded = (counts + MOE_BLOCK - 1) // MOE_BLOCK * MOE_BLOCK
    pad_ends = jnp.cumsum(padded)
    pad_starts = pad_ends - padded
    slot = jnp.sum(jnp.where(onehot > 0, pad_starts[None, :], 0), axis=1) + rank
    n_blocks = -(-(n_asg + n_exp * (MOE_BLOCK - 1)) // MOE_BLOCK)
    n_slots = n_blocks * MOE_BLOCK
    slot_tok = jnp.zeros((n_slots,), I32).at[slot].set(jnp.arange(n_asg, dtype=I32) // TOP_K)
    block_start = jnp.arange(n_blocks, dtype=I32) * MOE_BLOCK
    block_e = jnp.minimum(jnp.sum((pad_ends[None, :] <= block_start[:, None]).astype(I32), axis=1), n_exp - 1)
    n_used = (pad_ends[-1] // MOE_BLOCK).astype(I32).reshape(1)
    return slot.reshape(n_tok, TOP_K), slot_tok, block_e, n_used


def _rope_tables(seq):
    t = np.arange(seq)
    quarter = HEAD_DIM // 4
    inv = jnp.asarray(ROPE_BASE, F32) ** (-jnp.arange(quarter, dtype=F32) / quarter)
    ang_r = jnp.asarray(t // GRID_W, F32)[:, None] * inv[None, :]
    ang_c = jnp.asarray(t % GRID_W, F32)[:, None] * inv[None, :]
    cos = jnp.concatenate([jnp.cos(ang_r)] * 2 + [jnp.cos(ang_c)] * 2, axis=-1)
    sin = jnp.concatenate([-jnp.sin(ang_r), jnp.sin(ang_r), -jnp.sin(ang_c), jnp.sin(ang_c)], axis=-1)
    reps = LANES // HEAD_DIM
    return jnp.tile(cos, (1, reps)), jnp.tile(sin, (1, reps))


def kernel(x, c, ctx, c_ctx, w_ada, b_ada, g_mix, w_in, na_rpb, sc_conv, wa_sinks, w_na_out, w_sc_out, w_wa_out,
           w_out, g_ffn, w_router, b_router, w_exp_in, b_exp_in, w_exp_out, b_exp_out, g_final):
    b, s, d = x.shape
    l = ctx.shape[1]
    depth = w_ada.shape[0]
    n_exp = w_router.shape[-1]
    rows = s // GRID_W
    off_kv = OFF_GATES + N_BRANCHES * d
    n_cols = off_kv + N_KV_COLS
    assert w_in.shape[-1] == n_cols and s % NA_TQ == 0

    cvec = jnp.zeros((8, d), F32).at[:b].set(c).at[b].set(c_ctx)
    ada = _ada_call(cvec, w_ada, b_ada).reshape(depth, 8, N_ADA, d)
    rope = _rope_tables(s)
    q_scale = HEAD_DIM ** -0.5

    tm_lat = _pick(s, (1024, 512, 256))
    tm_lat_moe = _pick(s, (512, 256))
    tn_proj = _pick(n_cols, (1280, 768, 512, 256))
    tn_kv = _pick(np.gcd(N_KV_COLS, off_kv), (640, 256, 128))
    tm_merge = _pick(s, (512, 256))
    tn_merge = _pick(d, (512, 256))
    tm_comb = _pick(s, (256,))
    rope_cols = ((OFF_WA_Q, WA_Q_DIM), (off_kv + KV_WA_K, WA_KV_DIM))

    w_perm = jnp.concatenate(
        [w_in[..., REF_OFF_SC:REF_OFF_GATES], w_in[..., REF_OFF_WA_Q:REF_OFF_SC] * q_scale,
         w_in[..., REF_OFF_NA_Q:REF_OFF_WA_Q] * q_scale, w_in[..., REF_OFF_GATES:], w_in[..., :REF_OFF_NA_Q]],
        axis=-1).astype(BF16)
    wr_hi = w_router.astype(BF16)
    wr_lo = (w_router - wr_hi.astype(F32)).astype(BF16)
    w_r2 = jnp.concatenate([wr_hi, wr_lo], axis=-1)
    merge_w = (sc_conv, w_na_out.astype(BF16), w_sc_out.astype(BF16), w_wa_out.astype(BF16), w_out.astype(BF16))
    sub = d // 2 // LANES

    x_lat, x_ctx = x, ctx
    moe_lat = moe_ctx = None
    for layer in range(depth):
        last = layer == depth - 1
        mod_lat = [ada[layer, :b, k][:, None, :] for k in range(N_ADA)]
        mod_ctx = [jnp.broadcast_to(ada[layer, b, k][None, None, :], (b, 1, d)) for k in range(N_ADA)]

        p_lat = _proj_call(x_lat, g_mix, mod_lat[0], mod_lat[1], w_perm, layer, 0, n_cols, rope,
                           tm_lat if moe_lat is None else tm_lat_moe, tn_proj, rope_cols, moe_lat)
        if last:
            p_ctx = _proj_call(x_ctx, g_mix, mod_ctx[0], mod_ctx[1], w_perm, layer, off_kv, N_KV_COLS, None, l,
                               tn_kv, (), moe_ctx)
            ctx_kv = 0
        else:
            p_ctx = _proj_call(x_ctx, g_mix, mod_ctx[0], mod_ctx[1], w_perm, layer, 0, n_cols, None, l, tn_proj, (),
                               moe_ctx)
            ctx_kv = off_kv
        if moe_lat is not None:
            p_lat, x_lat = p_lat
            p_ctx, x_ctx = p_ctx

        bias = _na_bias_tables(na_rpb[layer], rows)
        o_na = _na_call(p_lat, p_ctx, bias, off_kv, ctx_kv)
        o_wa = _wa_call(p_lat, p_ctx, wa_sinks[layer], off_kv, ctx_kv)

        x_lat, f_lat, te_lat, tg_lat = _merge_call(
            x_lat, o_na, o_wa, p_lat, layer, *merge_w, mod_lat[2], g_ffn, mod_lat[3], mod_lat[4],
            w_r2, b_router, tm_merge, tn_merge)
        f_all = f_lat.reshape(b * s * sub, LANES)
        te_all = te_lat.reshape(b * s, LANES)[:, :TOP_K]
        if not last:
            o_na_c, o_wa_c = _ctx_attn_call(p_ctx, wa_sinks[layer], off_kv)
            x_ctx, f_ctx, te_ctx, tg_ctx = _merge_call(
                x_ctx, o_na_c, o_wa_c, p_ctx, layer, *merge_w, mod_ctx[2], g_ffn, mod_ctx[3], mod_ctx[4],
                w_r2, b_router, l, tn_merge)
            f_all = jnp.concatenate([f_all, f_ctx.reshape(b * l * sub, LANES)], axis=0)
            te_all = jnp.concatenate([te_all, te_ctx.reshape(b * l, LANES)[:, :TOP_K]], axis=0)

        slots, slot_tok, block_e, n_used = _route(te_all, n_exp)
        y_sorted = _moe_call(layer, block_e, n_used, slot_tok, f_all, w_exp_in, b_exp_in, w_exp_out, b_exp_out)
        moe_lat = (slots[:b * s].reshape(b, s, TOP_K), tg_lat, mod_lat[5], y_sorted)
        if not last:
            moe_ctx = (slots[b * s:].reshape(b, l, TOP_K), tg_ctx, mod_ctx[5], y_sorted)
    return _final_call(x_lat, *moe_lat[:3], g_final, moe_lat[3], tm_comb)
```

```python
import functools

import numpy as np
import jax
import jax.numpy as jnp
from jax import lax
from jax.experimental import pallas as pl
from jax.experimental.pallas import tpu as pltpu

F32 = jnp.float32
BF16 = jnp.bfloat16
U32 = jnp.uint32
I32 = jnp.int32

GRID_W = 64
HEAD_DIM = 64
NA_HEADS = 8
NA_WIN_ROWS = 8
NA_WIN_COLS = 16
SC_WIDTH = 1024
SC_CONV_WIDTH = 3
WA_Q_HEADS = 8
WA_KV_HEADS = 2
WA_GROUP = WA_Q_HEADS // WA_KV_HEADS
WA_WINDOW = 128
ROPE_BASE = 10000.0
N_BRANCHES = 3
TOP_K = 4
SWIGLU_LIMIT = 7.0
SWIGLU_ALPHA = 1.702
N_ADA = 6
RMS_EPS = 1e-6
NEG_INF = -1e30

NA_DIM = NA_HEADS * HEAD_DIM
WA_Q_DIM = WA_Q_HEADS * HEAD_DIM
WA_KV_DIM = WA_KV_HEADS * HEAD_DIM

LANES = 128
VMEM_BUDGET = 56 * 1024 * 1024

REF_OFF_NA_Q = 2 * NA_DIM + 2 * WA_KV_DIM
REF_OFF_WA_Q = REF_OFF_NA_Q + NA_DIM
REF_OFF_SC = REF_OFF_WA_Q + WA_Q_DIM
REF_OFF_GATES = REF_OFF_SC + 3 * SC_WIDTH

OFF_SC_B = 0
OFF_SC_C = OFF_SC_B + SC_WIDTH
OFF_SC_H = OFF_SC_C + SC_WIDTH
OFF_WA_Q = OFF_SC_H + SC_WIDTH
OFF_NA_Q = OFF_WA_Q + WA_Q_DIM
OFF_GATES = OFF_NA_Q + NA_DIM
KV_NA_K = 0
KV_NA_V = KV_NA_K + NA_DIM
KV_WA_K = KV_NA_V + NA_DIM
KV_WA_V = KV_WA_K + WA_KV_DIM
N_KV_COLS = KV_WA_V + WA_KV_DIM

NA_QROWS = 4
NA_SLAB_ROWS = NA_QROWS + NA_WIN_ROWS - 1
NA_TQ = NA_QROWS * GRID_W
NA_TK = NA_SLAB_ROWS * GRID_W
NA_STEP_HEADS = 4
WA_TQ = 256
WA_TK = WA_TQ + 2 * WA_WINDOW

MOE_BLOCK = 512
MOE_BUFS = 3
DMA_QUEUES = 2
LOG2E = 1.4426950408889634


def _params(semantics, vmem_bytes):
    limit = int(min(max(vmem_bytes * 5 // 4 + (4 << 20), 32 << 20), VMEM_BUDGET))
    return pltpu.CompilerParams(dimension_semantics=semantics, vmem_limit_bytes=limit)


def _pick(n, candidates):
    for c in candidates:
        if n % c == 0:
            return c
    return n


def _dot(a, b):
    return jnp.dot(a, b, preferred_element_type=F32)


def _dot_nt(a, b):
    return lax.dot_general(a, b, (((1,), (1,)), ((), ())), preferred_element_type=F32)


def _rms_mod(x, g, shift, scale):
    y = x * lax.rsqrt(jnp.mean(x * x, axis=-1, keepdims=True) + RMS_EPS)
    return (y * g) * (1.0 + scale) + shift


def _ada_kernel(c_ref, w_ref, b_ref, o_ref):
    c = c_ref[...]
    s = c * jax.nn.sigmoid(c)
    o_ref[0] = jnp.dot(s, w_ref[0], preferred_element_type=F32,
                       precision=lax.Precision.HIGHEST) + b_ref[0]


def _ada_call(cvec, w_ada, b_ada):
    depth, d, n = w_ada.shape
    tn = _pick(n, (1024, 768, 512, 256, 128))
    return pl.pallas_call(
        _ada_kernel,
        grid=(depth, n // tn),
        in_specs=[pl.BlockSpec((8, d), lambda l, j: (0, 0)),
                  pl.BlockSpec((1, d, tn), lambda l, j: (l, 0, j)),
                  pl.BlockSpec((1, 1, tn), lambda l, j: (l, 0, j))],
        out_specs=pl.BlockSpec((1, 8, tn), lambda l, j: (l, 0, j)),
        out_shape=jax.ShapeDtypeStruct((depth, 8, n), F32),
        compiler_params=_params(("arbitrary", "arbitrary"), 2 * d * tn * 4),
        name="ada",
    )(cvec, w_ada, b_ada.reshape(depth, 1, n))


def _rope128(x, cos, sin):
    lane = lax.broadcasted_iota(I32, x.shape, 1)
    fwd = pltpu.roll(x, LANES - HEAD_DIM // 4, 1)
    bwd = pltpu.roll(x, HEAD_DIM // 4, 1)
    partner = jnp.where((lane & (HEAD_DIM // 4)) == 0, fwd, bwd)
    return x * cos + partner * sin


def _gather_rows(y_hbm, ybuf, sem, slot_ref, r, buf, sub):
    src = pl.multiple_of(slot_ref[0, 0, r] * sub, sub)
    dst = pl.multiple_of(r * sub, sub)
    return pltpu.make_async_copy(y_hbm.at[pl.ds(src, sub), :], ybuf.at[buf, pl.ds(dst, sub), :], sem.at[buf])


def _wait_rows(y_hbm, ybuf, sem, buf):
    pltpu.make_async_copy(y_hbm.at[pl.ds(0, ybuf.shape[1]), :], ybuf.at[buf], sem.at[buf]).wait()


def _combine_rows(ybuf, buf, gates, tm, sub):
    acc = None
    for k in range(TOP_K):
        yk = _load_token_tiles(ybuf.at[buf, pl.ds(k * tm * sub, tm * sub), :], tm, F32) * gates[:, k:k + 1]
        acc = yk if acc is None else acc + yk
    return acc


def _proj_kernel(*refs, rope_tiles, combine, per):
    refs = list(refs)
    if combine:
        slot_cur_ref, slot_nxt_ref, gate_ref, gtf_ref, y_hbm = refs[:5]
        refs = refs[5:]
    x_ref, g_ref, sh_ref, sc_ref = refs[:4]
    refs = refs[4:]
    if rope_tiles:
        cos_ref, sin_ref = refs[:2]
        refs = refs[2:]
    w_ref, o_ref = refs[:2]
    refs = refs[2:]
    if combine:
        xo_ref, h_ref, ybuf, sem = refs
    else:
        (h_ref,) = refs
    j = pl.program_id(2)
    tm, d = x_ref.shape[1], x_ref.shape[2]
    sub = d // 2 // LANES
    if combine:
        t = pl.program_id(0) * pl.num_programs(1) + pl.program_id(1)
        cur = t % 2
        last_step = jnp.logical_and(t == pl.num_programs(0) * pl.num_programs(1) - 1, j == pl.num_programs(2) - 1)

    @pl.when(j == 0)
    def _():
        if combine:
            @pl.when(t == 0)
            def _():
                def body(r, carry):
                    _gather_rows(y_hbm, ybuf, sem, slot_cur_ref, r, 0, sub).start()
                    return carry
                lax.fori_loop(0, ybuf.shape[1] // sub, body, 0)

            _wait_rows(y_hbm, ybuf, sem, cur)
            x = x_ref[0] + gtf_ref[0] * _combine_rows(ybuf, cur, gate_ref[0], tm, sub)
            xo_ref[0] = x
        else:
            x = x_ref[0]
        h_ref[...] = _rms_mod(x, g_ref[0], sh_ref[0], sc_ref[0]).astype(BF16)

    if combine:
        for q in range(per):
            _gather_rows(y_hbm, ybuf, sem, slot_nxt_ref, j * per + q, 1 - cur, sub).start(priority=q % DMA_QUEUES)
    acc = _dot(h_ref[...], w_ref[0])
    o_ref[0] = acc.astype(BF16)
    for jt, lo, width in rope_tiles:
        @pl.when(j == jt)
        def _(lo=lo, width=width):
            cos, sin = cos_ref[...], sin_ref[...]
            for c0 in range(lo, lo + width, LANES):
                o_ref[0, :, c0:c0 + LANES] = _rope128(acc[:, c0:c0 + LANES], cos, sin).astype(BF16)
    if combine:
        @pl.when(last_step)
        def _():
            _wait_rows(y_hbm, ybuf, sem, 1 - cur)


def _tile_slots(slots, tm, n_rows):
    b, n, _ = slots.shape
    st = slots.reshape(b, n // tm, tm, TOP_K).transpose(0, 1, 3, 2).reshape(b * (n // tm), 1, TOP_K * tm)
    return jnp.pad(st, ((0, 0), (0, 0), (0, n_rows - TOP_K * tm)))


def _proj_call(x, g, shift, scale, w_all, layer, col0, nc, rope, tm, tn, rope_cols, moe=None):
    b, n, d = x.shape
    assert col0 % tn == 0 and nc % tn == 0 and n % tm == 0
    jb = col0 // tn
    ni, nj = n // tm, nc // tn
    sub = d // 2 // LANES
    rope_tiles = []
    if rope is not None:
        for off, width in rope_cols:
            assert off // tn == (off + width - 1) // tn and off % LANES == 0 and width % LANES == 0
            rope_tiles.append((off // tn, off % tn, width))
    in_specs, args, per = [], [], 0
    vmem = 2 * tm * d * 4 + tm * d * 2 + 2 * d * tn * 2 + 2 * tm * tn * 2 + tm * tn * 4 + 4 * tm * LANES * 4
    if moe is not None:
        slots, gates, gt, y_sorted = moe
        per = -(-TOP_K * tm // nj)
        n_rows = per * nj
        slots_t = _tile_slots(slots, tm, n_rows)
        n_t = b * ni

        def smem(index):
            return pl.BlockSpec((1, 1, n_rows), index, memory_space=pltpu.SMEM)

        in_specs += [smem(lambda bb, i, j: (bb * ni + i, 0, 0)),
                     smem(lambda bb, i, j: (jnp.minimum(bb * ni + i + 1, n_t - 1), 0, 0)),
                     pl.BlockSpec((1, tm, LANES), lambda bb, i, j: (bb, i, 0)),
                     pl.BlockSpec((1, 1, d), lambda bb, i, j: (bb, 0, 0)),
                     pl.BlockSpec(memory_space=pl.ANY)]
        args += [slots_t, slots_t, gates, gt, y_sorted]
        vmem += 2 * tm * d * 4 + 2 * n_rows * sub * LANES * 4 + 3 * tm * d * 4
    in_specs += [pl.BlockSpec((1, tm, d), lambda bb, i, j: (bb, i, 0)),
                 pl.BlockSpec((1, 1, d), lambda bb, i, j: (layer, 0, 0)),
                 pl.BlockSpec((1, 1, d), lambda bb, i, j: (bb, 0, 0)),
                 pl.BlockSpec((1, 1, d), lambda bb, i, j: (bb, 0, 0))]
    args += [x, g.reshape(g.shape[0], 1, d), shift, scale]
    if rope is not None:
        in_specs += [pl.BlockSpec((tm, LANES), lambda bb, i, j: (i, 0))] * 2
        args += list(rope)
    in_specs.append(pl.BlockSpec((1, d, tn), lambda bb, i, j: (layer, 0, jb + j)))
    args.append(w_all)
    out_specs = [pl.BlockSpec((1, tm, tn), lambda bb, i, j: (bb, i, j))]
    out_shape = [jax.ShapeDtypeStruct((b, n, nc), BF16)]
    scratch = [pltpu.VMEM((tm, d), BF16)]
    if moe is not None:
        out_specs.append(pl.BlockSpec((1, tm, d), lambda bb, i, j: (bb, i, 0)))
        out_shape.append(jax.ShapeDtypeStruct((b, n, d), F32))
        scratch += [pltpu.VMEM((2, n_rows * sub, LANES), U32), pltpu.SemaphoreType.DMA((2,))]
    limit = int(min(max(vmem * 5 // 4 + (4 << 20), 32 << 20), VMEM_BUDGET))
    out = pl.pallas_call(
        functools.partial(_proj_kernel, rope_tiles=tuple(rope_tiles), combine=moe is not None, per=per),
        grid=(b, ni, nj),
        in_specs=in_specs,
        out_specs=out_specs,
        out_shape=out_shape,
        scratch_shapes=scratch,
        compiler_params=pltpu.CompilerParams(dimension_semantics=("arbitrary", "arbitrary", "arbitrary"),
                                             vmem_limit_bytes=limit, disable_bounds_checks=moe is not None),
        name="proj",
    )(*args)
    return out if moe is not None else out[0]


def _softmax_av(parts, extra=None):
    m = functools.reduce(jnp.maximum, [jnp.max(s, axis=-1, keepdims=True) for s, _ in parts])
    if extra is not None:
        m = jnp.maximum(m, extra)
    l = 0.0 if extra is None else jnp.exp2(extra - m)
    o = None
    for s, v in parts:
        p = jnp.exp2(s - m)
        l = l + jnp.sum(p, axis=-1, keepdims=True)
        pv = _dot(p.astype(BF16), v)
        o = pv if o is None else o + pv
    return o / l


def _na_kernel(q_ref, k_ref, v_ref, kc_ref, vc_ref, bias_ref, o_ref, *, rows):
    i = pl.program_id(2)
    r_start = jnp.clip(NA_QROWS * i - NA_WIN_ROWS // 2, 0, rows - NA_SLAB_ROWS)
    start = pl.multiple_of(r_start * GRID_W, GRID_W)
    kslab = k_ref[0, pl.ds(start, NA_TK), :]
    vslab = v_ref[0, pl.ds(start, NA_TK), :]
    q, kc, vc = q_ref[0], kc_ref[0], vc_ref[0]
    outs = []
    for hh in range(NA_STEP_HEADS):
        sl = slice(hh * HEAD_DIM, (hh + 1) * HEAD_DIM)
        qh = q[:, sl]
        s_nb = _dot_nt(qh, kslab[:, sl]) + bias_ref[0, hh]
        s_ctx = _dot_nt(qh, kc[:, sl])
        outs.append(_softmax_av([(s_nb, vslab[:, sl]), (s_ctx, vc[:, sl])]))
    o_ref[0] = jnp.concatenate(outs, axis=1).astype(BF16)


def _na_bias_tables(rpb, rows):
    nblk = rows // NA_QROWS
    assert nblk >= 3 and rows >= NA_SLAB_ROWS
    n_heads, n_dr, n_dc = rpb.shape
    half = NA_WIN_COLS - 1
    vec = jnp.zeros((n_heads, n_dr, LANES), F32)
    vec = vec.at[..., :n_dc - half].set(rpb[..., half:]).at[..., LANES - half:].set(rpb[..., :half])
    toep = jnp.tile(vec, (1, 1, GRID_W))[..., :GRID_W * (LANES - 1)]
    toep = toep.reshape(n_heads, n_dr, GRID_W, LANES - 1)[..., :GRID_W]
    col = np.arange(GRID_W)
    c0 = np.clip(col - NA_WIN_COLS // 2, 0, GRID_W - NA_WIN_COLS)
    col_ok = (col[None, :] >= c0[:, None]) & (col[None, :] < c0[:, None] + NA_WIN_COLS)
    toep = jnp.where(col_ok[None, None], toep, NEG_INF)
    masked = jnp.full((n_heads, GRID_W, GRID_W), NEG_INF, F32)
    cases = []
    for i in (0, 1, nblk - 1):
        rs = int(np.clip(NA_QROWS * i - NA_WIN_ROWS // 2, 0, rows - NA_SLAB_ROWS))
        q_rows = []
        for rl in range(NA_QROWS):
            r = NA_QROWS * i + rl
            r0 = int(np.clip(r - NA_WIN_ROWS // 2, 0, rows - NA_WIN_ROWS))
            tiles = []
            for kl in range(NA_SLAB_ROWS):
                kr = rs + kl
                tiles.append(toep[:, kr - r + NA_WIN_ROWS - 1] if r0 <= kr < r0 + NA_WIN_ROWS else masked)
            q_rows.append(jnp.concatenate(tiles, axis=2))
        cases.append(jnp.concatenate(q_rows, axis=1))
    return jnp.stack(cases)


def _na_call(p_lat, p_ctx, bias, kv_off_lat, kv_off_ctx):
    b, s, _ = p_lat.shape
    l = p_ctx.shape[1]
    rows = s // GRID_W
    nblk = rows // NA_QROWS
    w = NA_STEP_HEADS * HEAD_DIM
    hp = NA_DIM // w
    assert OFF_NA_Q % w == 0 and kv_off_lat % w == 0 and kv_off_ctx % w == 0
    qb, kb, vb = OFF_NA_Q // w, (kv_off_lat + KV_NA_K) // w, (kv_off_lat + KV_NA_V) // w
    kcb, vcb = (kv_off_ctx + KV_NA_K) // w, (kv_off_ctx + KV_NA_V) // w

    def case(i):
        return jnp.where(i == 0, 0, jnp.where(i == nblk - 1, 2, 1))

    vmem = 4 * s * w * 2 + 2 * NA_STEP_HEADS * NA_TQ * NA_TK * 4 + 8 * NA_TQ * (NA_TK + l) * 4
    return pl.pallas_call(
        functools.partial(_na_kernel, rows=rows),
        grid=(b, hp, nblk),
        in_specs=[pl.BlockSpec((1, NA_TQ, w), lambda bb, h, i: (bb, i, qb + h)),
                  pl.BlockSpec((1, s, w), lambda bb, h, i: (bb, 0, kb + h)),
                  pl.BlockSpec((1, s, w), lambda bb, h, i: (bb, 0, vb + h)),
                  pl.BlockSpec((1, l, w), lambda bb, h, i: (bb, 0, kcb + h)),
                  pl.BlockSpec((1, l, w), lambda bb, h, i: (bb, 0, vcb + h)),
                  pl.BlockSpec((1, NA_STEP_HEADS, NA_TQ, NA_TK), lambda bb, h, i: (case(i), h, 0, 0))],
        out_specs=pl.BlockSpec((1, NA_TQ, w), lambda bb, h, i: (bb, i, h)),
        out_shape=jax.ShapeDtypeStruct((b, s, NA_DIM), BF16),
        compiler_params=_params(("arbitrary", "arbitrary", "arbitrary"), vmem),
        name="natten",
    )(p_lat, p_lat, p_lat, p_ctx, p_ctx, bias)


def _wa_kernel(sink_ref, q_ref, k_ref, v_ref, kc_ref, vc_ref, o_ref, *, seq):
    i = pl.program_id(1)
    start = pl.multiple_of(jnp.clip(WA_TQ * i - WA_WINDOW, 0, seq - WA_TK), WA_WINDOW)
    kslab = k_ref[0, pl.ds(start, WA_TK), :]
    vslab = v_ref[0, pl.ds(start, WA_TK), :]
    qpos = WA_TQ * i + lax.broadcasted_iota(I32, (WA_TQ, WA_TK), 0)
    kpos = start + lax.broadcasted_iota(I32, (WA_TQ, WA_TK), 1)
    band = jnp.where(jnp.abs(kpos - qpos) <= WA_WINDOW, 0.0, NEG_INF).astype(F32)
    q, kc, vc = q_ref[0], kc_ref[0], vc_ref[0]
    outs = []
    for hq in range(WA_Q_HEADS):
        hk = hq // WA_GROUP
        sk = slice(hk * HEAD_DIM, (hk + 1) * HEAD_DIM)
        qh = q[:, hq * HEAD_DIM:(hq + 1) * HEAD_DIM]
        s_loc = _dot_nt(qh, kslab[:, sk]) + band
        s_ctx = _dot_nt(qh, kc[:, sk])
        outs.append(_softmax_av([(s_loc, vslab[:, sk]), (s_ctx, vc[:, sk])], extra=sink_ref[hq]))
    o_ref[0] = jnp.concatenate(outs, axis=1).astype(BF16)


def _wa_call(p_lat, p_ctx, sinks, kv_off_lat, kv_off_ctx):
    b, s, _ = p_lat.shape
    l = p_ctx.shape[1]
    assert s % WA_TQ == 0 and s >= WA_TK
    qb = OFF_WA_Q // WA_Q_DIM
    kb, vb = (kv_off_lat + KV_WA_K) // LANES, (kv_off_lat + KV_WA_V) // LANES
    kcb, vcb = (kv_off_ctx + KV_WA_K) // LANES, (kv_off_ctx + KV_WA_V) // LANES
    vmem = 4 * s * LANES * 2 + 4 * WA_TQ * WA_Q_DIM * 2 + 10 * WA_TQ * (WA_TK + l) * 4
    return pl.pallas_call(
        functools.partial(_wa_kernel, seq=s),
        grid=(b, s // WA_TQ),
        in_specs=[pl.BlockSpec(memory_space=pltpu.SMEM),
                  pl.BlockSpec((1, WA_TQ, WA_Q_DIM), lambda bb, i: (bb, i, qb)),
                  pl.BlockSpec((1, s, LANES), lambda bb, i: (bb, 0, kb)),
                  pl.BlockSpec((1, s, LANES), lambda bb, i: (bb, 0, vb)),
                  pl.BlockSpec((1, l, LANES), lambda bb, i: (bb, 0, kcb)),
                  pl.BlockSpec((1, l, LANES), lambda bb, i: (bb, 0, vcb))],
        out_specs=pl.BlockSpec((1, WA_TQ, WA_Q_DIM), lambda bb, i: (bb, i, 0)),
        out_shape=jax.ShapeDtypeStruct((b, s, WA_Q_DIM), BF16),
        compiler_params=_params(("arbitrary", "arbitrary"), vmem),
        name="winattn",
    )(sinks, p_lat, p_lat, p_lat, p_ctx, p_ctx)


def _ctx_attn_kernel(sink_ref, naq_ref, nak_ref, nav_ref, waq_ref, wak_ref, wav_ref, ona_ref, owa_ref):
    q, k, v = naq_ref[0], nak_ref[0], nav_ref[0]
    outs = []
    for h in range(NA_HEADS):
        sl = slice(h * HEAD_DIM, (h + 1) * HEAD_DIM)
        outs.append(_softmax_av([(_dot_nt(q[:, sl], k[:, sl]), v[:, sl])]))
    ona_ref[0] = jnp.concatenate(outs, axis=1).astype(BF16)
    q, k, v = waq_ref[0], wak_ref[0], wav_ref[0]
    outs = []
    for hq in range(WA_Q_HEADS):
        sk = slice((hq // WA_GROUP) * HEAD_DIM, (hq // WA_GROUP + 1) * HEAD_DIM)
        qh = q[:, hq * HEAD_DIM:(hq + 1) * HEAD_DIM]
        outs.append(_softmax_av([(_dot_nt(qh, k[:, sk]), v[:, sk])], extra=sink_ref[hq]))
    owa_ref[0] = jnp.concatenate(outs, axis=1).astype(BF16)


def _ctx_attn_call(p_ctx, sinks, kv_off):
    b, l, _ = p_ctx.shape

    def spec(width, off):
        return pl.BlockSpec((1, l, width), lambda bb: (bb, 0, off // width))

    return pl.pallas_call(
        _ctx_attn_kernel,
        grid=(b,),
        in_specs=[pl.BlockSpec(memory_space=pltpu.SMEM),
                  spec(NA_DIM, OFF_NA_Q), spec(NA_DIM, kv_off + KV_NA_K), spec(NA_DIM, kv_off + KV_NA_V),
                  spec(WA_Q_DIM, OFF_WA_Q), spec(WA_KV_DIM, kv_off + KV_WA_K), spec(WA_KV_DIM, kv_off + KV_WA_V)],
        out_specs=[pl.BlockSpec((1, l, NA_DIM), lambda bb: (bb, 0, 0)),
                   pl.BlockSpec((1, l, WA_Q_DIM), lambda bb: (bb, 0, 0))],
        out_shape=[jax.ShapeDtypeStruct((b, l, NA_DIM), BF16), jax.ShapeDtypeStruct((b, l, WA_Q_DIM), BF16)],
        compiler_params=_params(("arbitrary",), 16 * l * NA_DIM * 4),
        name="ctxattn",
    )(sinks, p_ctx, p_ctx, p_ctx, p_ctx, p_ctx, p_ctx)


def _pack_bf16_pairs(x):
    n = x.shape[1] // 2
    lo = pltpu.bitcast(x[:, :n].astype(BF16).astype(F32), U32)
    hi = pltpu.bitcast(x[:, n:].astype(BF16).astype(F32), U32)
    return (hi & jnp.uint32(0xFFFF0000)) | (lo >> 16)


def _store_token_tiles(ref, packed):
    m, n = packed.shape
    sub = n // LANES
    for s in range(sub):
        ref[pl.ds(s, m, stride=sub), :] = packed[:, s * LANES:(s + 1) * LANES]


def _load_token_tiles(ref, m, dtype):
    sub = ref.shape[0] // m
    chunks = [ref[pl.ds(s, m, stride=sub), :] for s in range(sub)]
    lo = [pltpu.bitcast(p << 16, F32).astype(dtype) for p in chunks]
    hi = [pltpu.bitcast(p & jnp.uint32(0xFFFF0000), F32).astype(dtype) for p in chunks]
    return jnp.concatenate(lo + hi, axis=1)


def _top_k_lanes(logits, n_exp):
    m_rows = logits.shape[0]
    col = lax.broadcasted_iota(I32, (m_rows, n_exp), 1).astype(F32)
    lane = lax.broadcasted_iota(I32, (m_rows, LANES), 1)
    work = logits
    vals = jnp.zeros((m_rows, LANES), F32)
    idxs = jnp.zeros((m_rows, LANES), F32)
    top = None
    denom = 0.0
    for k in range(TOP_K):
        mx = jnp.max(work, axis=-1, keepdims=True)
        ix = jnp.min(jnp.where(work == mx, col, float(n_exp)), axis=-1, keepdims=True)
        work = jnp.where(col == ix, -jnp.inf, work)
        top = mx if top is None else top
        e = jnp.exp(mx - top)
        denom = denom + e
        vals = jnp.where(lane == k, e, vals)
        idxs = jnp.where(lane == k, ix, idxs)
    return vals / denom, idxs.astype(I32)


def _merge_kernel(*refs, has_halo, tn):
    (ona_ref, owa_ref, scb_ref, scc_ref, sch_ref) = refs[:5]
    refs = refs[5:]
    if has_halo:
        cprev_ref, hprev_ref, cnext_ref, hnext_ref = refs[:4]
        refs = refs[4:]
    gna_ref, gsc_ref, gwa_ref, conv_ref, wna_ref, wsc_ref, wwa_ref, m_ref = refs
    i = pl.program_id(1)
    tm, d = m_ref.shape[1], m_ref.shape[2]

    u = scc_ref[0].astype(F32) * sch_ref[0].astype(F32)
    zero = jnp.zeros((1, SC_WIDTH), F32)
    if has_halo:
        u_prev = jnp.where(i == 0, zero, cprev_ref[0, 7:8, :].astype(F32) * hprev_ref[0, 7:8, :].astype(F32))
        u_next = jnp.where(i == pl.num_programs(1) - 1, zero,
                           cnext_ref[0, 0:1, :].astype(F32) * hnext_ref[0, 0:1, :].astype(F32))
    else:
        u_prev = u_next = zero
    row = lax.broadcasted_iota(I32, u.shape, 0)
    u_m1 = jnp.where(row == 0, u_prev, pltpu.roll(u, 1, 0))
    u_p1 = jnp.where(row == tm - 1, u_next, pltpu.roll(u, tm - 1, 0))
    y = u_m1 * conv_ref[0, 0:1, :] + u * conv_ref[0, 1:2, :] + u_p1 * conv_ref[0, 2:3, :]
    o_sc = (scb_ref[0].astype(F32) * y).astype(BF16)
    o_na, o_wa = ona_ref[0], owa_ref[0]

    def gate(ref, cs):
        return jax.nn.sigmoid(ref[0, :, cs].astype(F32))

    for n in range(d // tn):
        cs = slice(n * tn, (n + 1) * tn)
        merged = (gate(gna_ref, cs) * _dot(o_na, wna_ref[0, :, cs])
                  + gate(gsc_ref, cs) * _dot(o_sc, wsc_ref[0, :, cs])
                  + gate(gwa_ref, cs) * _dot(o_wa, wwa_ref[0, :, cs]))
        m_ref[0, :, cs] = merged.astype(BF16)


def _outproj_kernel(x_ref, m_ref, wout_ref, gt_ref, gffn_ref, shf_ref, scf_ref, wr_ref, br_ref,
                    xo_ref, f_ref, te_ref, tg_ref, *, n_exp):
    xn = x_ref[0] + gt_ref[0] * _dot(m_ref[0], wout_ref[0])
    xo_ref[0] = xn
    f = _rms_mod(xn, gffn_ref[0], shf_ref[0], scf_ref[0])
    _store_token_tiles(f_ref.at[0], _pack_bf16_pairs(f))
    f_hi = f.astype(BF16)
    f_lo = (f - f_hi.astype(F32)).astype(BF16)
    wr = wr_ref[0]
    hi_both = _dot(f_hi, wr)
    logits = (hi_both[:, :n_exp] + _dot(f_lo, wr[:, :n_exp]) + hi_both[:, n_exp:]) + br_ref[0]
    gates, idx = _top_k_lanes(logits, n_exp)
    tg_ref[0] = gates
    te_ref[0] = idx


def _merge_call(x, o_na, o_wa, p, layer, conv_w, w_na, w_sc, w_wa, w_out, gt, g_ffn, sh_f, sc_f, w_r2, b_r, tm, tn):
    b, n_tok, d = x.shape
    depth = w_out.shape[0]
    n_exp = b_r.shape[-1]
    sub = d // 2 // LANES
    nt = n_tok // tm
    has_halo = nt > 1
    assert OFF_GATES % d == 0 and d % tn == 0 and n_tok % tm == 0 and tm % 8 == 0
    gate_b = OFF_GATES // d
    rb = tm // 8

    def rows(width, off):
        return pl.BlockSpec((1, tm, width), lambda bb, i: (bb, i, off // width))

    def resident(shape):
        return pl.BlockSpec((1,) + shape, lambda bb, i: (layer, 0, 0), pipeline_mode=pl.Buffered(1))

    in_specs = [pl.BlockSpec((1, tm, NA_DIM), lambda bb, i: (bb, i, 0)),
                pl.BlockSpec((1, tm, WA_Q_DIM), lambda bb, i: (bb, i, 0)),
                rows(SC_WIDTH, OFF_SC_B), rows(SC_WIDTH, OFF_SC_C), rows(SC_WIDTH, OFF_SC_H)]
    args = [o_na, o_wa, p, p, p]
    if has_halo:
        last8 = n_tok // 8 - 1
        for off in (OFF_SC_C, OFF_SC_H):
            in_specs.append(pl.BlockSpec((1, 8, SC_WIDTH),
                                         lambda bb, i, off=off: (bb, jnp.maximum(i * rb - 1, 0), off // SC_WIDTH)))
        for off in (OFF_SC_C, OFF_SC_H):
            in_specs.append(pl.BlockSpec((1, 8, SC_WIDTH),
                                         lambda bb, i, off=off: (bb, jnp.minimum((i + 1) * rb, last8), off // SC_WIDTH)))
        args += [p, p, p, p]
    for br in range(N_BRANCHES):
        in_specs.append(pl.BlockSpec((1, tm, d), lambda bb, i, br=br: (bb, i, gate_b + br)))
        args.append(p)
    in_specs += [resident((SC_CONV_WIDTH, SC_WIDTH)),
                 resident((NA_DIM, d)), resident((SC_WIDTH, d)), resident((WA_Q_DIM, d))]
    args += [conv_w, w_na, w_sc, w_wa]
    vmem = (2 * tm * (NA_DIM + WA_Q_DIM + 3 * SC_WIDTH + N_BRANCHES * d + d) * 2
            + (NA_DIM + SC_WIDTH + WA_Q_DIM) * d * 2 + 4 * tm * SC_WIDTH * 4 + 3 * tm * tn * 4)
    merged = pl.pallas_call(
        functools.partial(_merge_kernel, has_halo=has_halo, tn=tn),
        grid=(b, nt),
        in_specs=in_specs,
        out_specs=pl.BlockSpec((1, tm, d), lambda bb, i: (bb, i, 0)),
        out_shape=jax.ShapeDtypeStruct((b, n_tok, d), BF16),
        compiler_params=_params(("arbitrary", "arbitrary"), vmem),
        name="merge",
    )(*args)

    def per_batch():
        return pl.BlockSpec((1, 1, d), lambda bb, i: (bb, 0, 0))

    out_specs = [pl.BlockSpec((1, tm, d), lambda bb, i: (bb, i, 0)),
                 pl.BlockSpec((1, tm * sub, LANES), lambda bb, i: (bb, i, 0)),
                 pl.BlockSpec((1, tm, LANES), lambda bb, i: (bb, i, 0)),
                 pl.BlockSpec((1, tm, LANES), lambda bb, i: (bb, i, 0))]
    out_shape = [jax.ShapeDtypeStruct((b, n_tok, d), F32), jax.ShapeDtypeStruct((b, n_tok * sub, LANES), U32),
                 jax.ShapeDtypeStruct((b, n_tok, LANES), I32), jax.ShapeDtypeStruct((b, n_tok, LANES), F32)]
    vmem = (4 * tm * d * 4 + 2 * tm * d * 2 + 2 * tm * (d // 2) * 4 + 4 * tm * LANES * 4
            + d * d * 2 + d * 2 * n_exp * 4 + 4 * tm * d * 4)
    return pl.pallas_call(
        functools.partial(_outproj_kernel, n_exp=n_exp),
        grid=(b, nt),
        in_specs=[pl.BlockSpec((1, tm, d), lambda bb, i: (bb, i, 0)),
                  pl.BlockSpec((1, tm, d), lambda bb, i: (bb, i, 0)),
                  resident((d, d)), per_batch(), resident((1, d)), per_batch(), per_batch(),
                  resident((d, 2 * n_exp)), resident((1, n_exp))],
        out_specs=out_specs,
        out_shape=out_shape,
        compiler_params=_params(("arbitrary", "arbitrary"), vmem),
        name="outproj",
    )(x, merged, w_out, gt, g_ffn.reshape(depth, 1, d), sh_f, sc_f, w_r2, b_r.reshape(depth, 1, n_exp))


def _moe_kernel(be_ref, nused_ref, tok_cur_ref, tok_nxt_ref, tok_far_ref, f_hbm, w1_ref, b1_ref, w2_ref, b2_ref,
                y_ref, xbuf, w1b, w2b, gsem, *, d_exp, sub):
    i = pl.program_id(0)
    n_used = nused_ref[0]
    slot = i % MOE_BUFS
    nxt = (i + 1) % MOE_BUFS
    far = (i + 2) % MOE_BUFS
    rows = MOE_BLOCK * sub

    def gather(tok_ref, r, dst_slot):
        src = pl.multiple_of(tok_ref[0, 0, r] * sub, sub)
        return pltpu.make_async_copy(f_hbm.at[pl.ds(src, sub), :], xbuf.at[dst_slot, pl.ds(r * sub, sub), :],
                                     gsem.at[dst_slot])

    def wait_gather(s):
        pltpu.make_async_copy(f_hbm.at[pl.ds(0, rows), :], xbuf.at[s], gsem.at[s]).wait()

    @pl.when(i < n_used)
    def _():
        @pl.when(i == 0)
        def _():
            def body(r, carry):
                gather(tok_cur_ref, r, 0).start()
                gather(tok_nxt_ref, r, 1).start()
                return carry
            lax.fori_loop(0, MOE_BLOCK, body, 0)

        @pl.when(jnp.logical_or(i == 0, be_ref[i] != be_ref[jnp.maximum(i - 1, 0)]))
        def _():
            w1b[...] = w1_ref[0, 0].astype(BF16)
            w2b[...] = w2_ref[0, 0].astype(BF16)

        wait_gather(slot)
        x = _load_token_tiles(xbuf.at[slot], MOE_BLOCK, BF16)
        n_chunks = 2
        cw = d_exp // n_chunks
        per = MOE_BLOCK // n_chunks
        y = None
        for c in range(n_chunks):
            for r in range(c * per, (c + 1) * per):
                gather(tok_far_ref, r, far).start(priority=r % DMA_QUEUES)
            glu = _dot(x, w1b[:, c * cw:(c + 1) * cw]) + b1_ref[0, 0, :, c * cw:(c + 1) * cw]
            lin = _dot(x, w1b[:, d_exp + c * cw:d_exp + (c + 1) * cw]) + b1_ref[0, 0, :, d_exp + c * cw:d_exp + (c + 1) * cw]
            glu = jnp.minimum(glu, SWIGLU_LIMIT)
            lin = jnp.clip(lin, -SWIGLU_LIMIT, SWIGLU_LIMIT)
            act = glu * jax.nn.sigmoid(SWIGLU_ALPHA * glu) * (lin + 1.0)
            yc = _dot(act.astype(BF16), w2b[c * cw:(c + 1) * cw, :])
            y = yc if y is None else y + yc
        _store_token_tiles(y_ref, _pack_bf16_pairs(y + b2_ref[0, 0]))

        @pl.when(i == n_used - 1)
        def _():
            wait_gather(nxt)
            wait_gather(far)

    @pl.when(i >= n_used)
    def _():
        y_ref[...] = jnp.zeros_like(y_ref)


def _moe_call(layer, block_e, n_used, slot_tok, f_all, w1, b1, w2, b2):
    n_blocks = block_e.shape[0]
    depth, n_exp, d, two_de = w1.shape
    d_exp = two_de // 2
    sub = d // 2 // LANES
    assert d_exp % (2 * LANES) == 0
    rows = MOE_BLOCK * sub

    def smem(index):
        return pl.BlockSpec((1, 1, MOE_BLOCK), index, memory_space=pltpu.SMEM)

    grid_spec = pltpu.PrefetchScalarGridSpec(
        num_scalar_prefetch=2,
        grid=(n_blocks,),
        in_specs=[smem(lambda i, be, nu: (i, 0, 0)),
                  smem(lambda i, be, nu: (jnp.minimum(i + 1, n_blocks - 1), 0, 0)),
                  smem(lambda i, be, nu: (jnp.minimum(i + 2, n_blocks - 1), 0, 0)),
                  pl.BlockSpec(memory_space=pl.ANY),
                  pl.BlockSpec((1, 1, d, two_de), lambda i, be, nu: (layer, be[i], 0, 0)),
                  pl.BlockSpec((1, 1, 1, two_de), lambda i, be, nu: (layer, be[i], 0, 0)),
                  pl.BlockSpec((1, 1, d_exp, d), lambda i, be, nu: (layer, be[i], 0, 0)),
                  pl.BlockSpec((1, 1, 1, d), lambda i, be, nu: (layer, be[i], 0, 0))],
        out_specs=pl.BlockSpec((rows, LANES), lambda i, be, nu: (i, 0)),
        scratch_shapes=[pltpu.VMEM((MOE_BUFS, rows, LANES), U32),
                        pltpu.VMEM((d, two_de), BF16), pltpu.VMEM((d_exp, d), BF16),
                        pltpu.SemaphoreType.DMA((MOE_BUFS,))],
    )
    vmem = (2 * (d * two_de + d_exp * d) * 4 + (d * two_de + d_exp * d) * 2 + 4 * rows * LANES * 4
            + 6 * MOE_BLOCK * (d + two_de) * 4)
    tok3 = slot_tok.reshape(n_blocks, 1, MOE_BLOCK)
    return pl.pallas_call(
        functools.partial(_moe_kernel, d_exp=d_exp, sub=sub),
        grid_spec=grid_spec,
        out_shape=jax.ShapeDtypeStruct((n_blocks * rows, LANES), U32),
        compiler_params=pltpu.CompilerParams(dimension_semantics=("arbitrary",),
                                             vmem_limit_bytes=int(min(vmem + (4 << 20), VMEM_BUDGET)),
                                             disable_bounds_checks=True),
        name="experts",
    )(block_e, n_used, tok3, tok3, tok3, f_all, w1, b1.reshape(depth, n_exp, 1, two_de), w2,
      b2.reshape(depth, n_exp, 1, d))


def _final_kernel(slot_cur_ref, slot_nxt_ref, x_ref, gate_ref, gt_ref, gfin_ref, y_hbm, o_ref, ybuf, sem):
    t = pl.program_id(0) * pl.num_programs(1) + pl.program_id(1)
    n_t = pl.num_programs(0) * pl.num_programs(1)
    tm, d = x_ref.shape[1], x_ref.shape[2]
    sub = d // 2 // LANES
    cur = t % 2

    def issue(slot_ref, buf):
        def body(r, carry):
            for q in range(DMA_QUEUES):
                _gather_rows(y_hbm, ybuf, sem, slot_ref, r * DMA_QUEUES + q, buf, sub).start(priority=q)
            return carry
        lax.fori_loop(0, TOP_K * tm // DMA_QUEUES, body, 0)

    @pl.when(t == 0)
    def _():
        issue(slot_cur_ref, 0)

    @pl.when(t + 1 < n_t)
    def _():
        issue(slot_nxt_ref, 1 - cur)

    _wait_rows(y_hbm, ybuf, sem, cur)
    xn = x_ref[0] + gt_ref[0] * _combine_rows(ybuf, cur, gate_ref[0], tm, sub)
    o_ref[0] = xn * lax.rsqrt(jnp.mean(xn * xn, axis=-1, keepdims=True) + RMS_EPS) * gfin_ref[...]


def _final_call(x, slots, gates, gt, g_final, y_sorted, tm):
    b, n_tok, d = x.shape
    nt = n_tok // tm
    sub = d // 2 // LANES
    n_rows = TOP_K * tm
    slots_t = _tile_slots(slots, tm, n_rows)
    n_t = b * nt

    def smem(index):
        return pl.BlockSpec((1, 1, n_rows), index, memory_space=pltpu.SMEM)

    vmem = 4 * tm * d * 4 + 2 * n_rows * sub * LANES * 4 + 2 * tm * LANES * 4 + 4 * tm * d * 4
    return pl.pallas_call(
        _final_kernel,
        grid=(b, nt),
        in_specs=[smem(lambda bb, i: (bb * nt + i, 0, 0)),
                  smem(lambda bb, i: (jnp.minimum(bb * nt + i + 1, n_t - 1), 0, 0)),
                  pl.BlockSpec((1, tm, d), lambda bb, i: (bb, i, 0)),
                  pl.BlockSpec((1, tm, LANES), lambda bb, i: (bb, i, 0)),
                  pl.BlockSpec((1, 1, d), lambda bb, i: (bb, 0, 0)),
                  pl.BlockSpec((1, d), lambda bb, i: (0, 0)),
                  pl.BlockSpec(memory_space=pl.ANY)],
        out_specs=pl.BlockSpec((1, tm, d), lambda bb, i: (bb, i, 0)),
        out_shape=jax.ShapeDtypeStruct((b, n_tok, d), F32),
        scratch_shapes=[pltpu.VMEM((2, n_rows * sub, LANES), U32), pltpu.SemaphoreType.DMA((2,))],
        compiler_params=pltpu.CompilerParams(dimension_semantics=("arbitrary", "arbitrary"),
                                             vmem_limit_bytes=int(min(vmem * 5 // 4 + (4 << 20), VMEM_BUDGET)),
                                             disable_bounds_checks=True),
        name="final",
    )(slots_t, slots_t, x, gates, gt, g_final.reshape(1, d), y_sorted)


def _route(top_e, n_exp):
    n_tok = top_e.shape[0]
    n_asg = n_tok * TOP_K
    flat_e = top_e.reshape(n_asg)
    onehot = (flat_e[:, None] == jnp.arange(n_exp, dtype=I32)[None, :]).astype(I32)
    csum = jnp.cumsum(onehot, axis=0)
    rank = jnp.sum(csum * onehot, axis=1) - 1
    counts = csum[-1]
    padded = (counts + MOE_BLOCK - 1) // MOE_BLOCK * MOE_BLOCK
    pad_ends = jnp.cumsum(padded)
    pad_starts = pad_ends - padded
    slot = jnp.sum(jnp.where(onehot > 0, pad_starts[None, :], 0), axis=1) + rank
    n_blocks = -(-(n_asg + n_exp * (MOE_BLOCK - 1)) // MOE_BLOCK)
    n_slots = n_blocks * MOE_BLOCK
    slot_tok = jnp.zeros((n_slots,), I32).at[slot].set(jnp.arange(n_asg, dtype=I32) // TOP_K)
    block_start = jnp.arange(n_blocks, dtype=I32) * MOE_BLOCK
    block_e = jnp.minimum(jnp.sum((pad_ends[None, :] <= block_start[:, None]).astype(I32), axis=1), n_exp - 1)
    n_used = (pad_ends[-1] // MOE_BLOCK).astype(I32).reshape(1)
    return slot.reshape(n_tok, TOP_K), slot_tok, block_e, n_used


def _rope_tables(seq):
    t = np.arange(seq)
    quarter = HEAD_DIM // 4
    inv = jnp.asarray(ROPE_BASE, F32) ** (-jnp.arange(quarter, dtype=F32) / quarter)
    ang_r = jnp.asarray(t // GRID_W, F32)[:, None] * inv[None, :]
    ang_c = jnp.asarray(t % GRID_W, F32)[:, None] * inv[None, :]
    cos = jnp.concatenate([jnp.cos(ang_r)] * 2 + [jnp.cos(ang_c)] * 2, axis=-1)
    sin = jnp.concatenate([-jnp.sin(ang_r), jnp.sin(ang_r), -jnp.sin(ang_c), jnp.sin(ang_c)], axis=-1)
    reps = LANES // HEAD_DIM
    return jnp.tile(cos, (1, reps)), jnp.tile(sin, (1, reps))


def kernel(x, c, ctx, c_ctx, w_ada, b_ada, g_mix, w_in, na_rpb, sc_conv, wa_sinks, w_na_out, w_sc_out, w_wa_out,
           w_out, g_ffn, w_router, b_router, w_exp_in, b_exp_in, w_exp_out, b_exp_out, g_final):
    b, s, d = x.shape
    l = ctx.shape[1]
    depth = w_ada.shape[0]
    n_exp = w_router.shape[-1]
    rows = s // GRID_W
    off_kv = OFF_GATES + N_BRANCHES * d
    n_cols = off_kv + N_KV_COLS
    assert w_in.shape[-1] == n_cols and s % NA_TQ == 0

    cvec = jnp.zeros((8, d), F32).at[:b].set(c).at[b].set(c_ctx)
    ada = _ada_call(cvec, w_ada, b_ada).reshape(depth, 8, N_ADA, d)
    rope = _rope_tables(s)
    q_scale = HEAD_DIM ** -0.5 * LOG2E
    sinks2 = wa_sinks * LOG2E

    tm_lat = _pick(s, (1024, 512, 256))
    tm_lat_moe = _pick(s, (512, 256))
    tn_proj = _pick(n_cols, (1280, 768, 512, 256))
    tn_kv = _pick(np.gcd(N_KV_COLS, off_kv), (640, 256, 128))
    tm_merge = _pick(s, (512, 256))
    tn_merge = _pick(d, (512, 256))
    tm_comb = _pick(s, (256,))
    rope_cols = ((OFF_WA_Q, WA_Q_DIM), (off_kv + KV_WA_K, WA_KV_DIM))

    w_perm = jnp.concatenate(
        [w_in[..., REF_OFF_SC:REF_OFF_GATES], w_in[..., REF_OFF_WA_Q:REF_OFF_SC] * q_scale,
         w_in[..., REF_OFF_NA_Q:REF_OFF_WA_Q] * q_scale, w_in[..., REF_OFF_GATES:], w_in[..., :REF_OFF_NA_Q]],
        axis=-1).astype(BF16)
    wr_hi = w_router.astype(BF16)
    wr_lo = (w_router - wr_hi.astype(F32)).astype(BF16)
    w_r2 = jnp.concatenate([wr_hi, wr_lo], axis=-1)
    merge_w = (sc_conv, w_na_out.astype(BF16), w_sc_out.astype(BF16), w_wa_out.astype(BF16), w_out.astype(BF16))
    sub = d // 2 // LANES

    x_lat, x_ctx = x, ctx
    moe_lat = moe_ctx = None
    for layer in range(depth):
        last = layer == depth - 1
        mod_lat = [ada[layer, :b, k][:, None, :] for k in range(N_ADA)]
        mod_ctx = [jnp.broadcast_to(ada[layer, b, k][None, None, :], (b, 1, d)) for k in range(N_ADA)]

        p_lat = _proj_call(x_lat, g_mix, mod_lat[0], mod_lat[1], w_perm, layer, 0, n_cols, rope,
                           tm_lat if moe_lat is None else tm_lat_moe, tn_proj, rope_cols, moe_lat)
        if last:
            p_ctx = _proj_call(x_ctx, g_mix, mod_ctx[0], mod_ctx[1], w_perm, layer, off_kv, N_KV_COLS, None, l,
                               tn_kv, (), moe_ctx)
            ctx_kv = 0
        else:
            p_ctx = _proj_call(x_ctx, g_mix, mod_ctx[0], mod_ctx[1], w_perm, layer, 0, n_cols, None, l, tn_proj, (),
                               moe_ctx)
            ctx_kv = off_kv
        if moe_lat is not None:
            p_lat, x_lat = p_lat
            p_ctx, x_ctx = p_ctx

        bias = _na_bias_tables(na_rpb[layer] * LOG2E, rows)
        o_na = _na_call(p_lat, p_ctx, bias, off_kv, ctx_kv)
        o_wa = _wa_call(p_lat, p_ctx, sinks2[layer], off_kv, ctx_kv)

        x_lat, f_lat, te_lat, tg_lat = _merge_call(
            x_lat, o_na, o_wa, p_lat, layer, *merge_w, mod_lat[2], g_ffn, mod_lat[3], mod_lat[4],
            w_r2, b_router, tm_merge, tn_merge)
        f_all = f_lat.reshape(b * s * sub, LANES)
        te_all = te_lat.reshape(b * s, LANES)[:, :TOP_K]
        if not last:
            o_na_c, o_wa_c = _ctx_attn_call(p_ctx, sinks2[layer], off_kv)
            x_ctx, f_ctx, te_ctx, tg_ctx = _merge_call(
                x_ctx, o_na_c, o_wa_c, p_ctx, layer, *merge_w, mod_ctx[2], g_ffn, mod_ctx[3], mod_ctx[4],
                w_r2, b_router, l, tn_merge)
            f_all = jnp.concatenate([f_all, f_ctx.reshape(b * l * sub, LANES)], axis=0)
            te_all = jnp.concatenate([te_all, te_ctx.reshape(b * l, LANES)[:, :TOP_K]], axis=0)

        slots, slot_tok, block_e, n_used = _route(te_all, n_exp)
        y_sorted = _moe_call(layer, block_e, n_used, slot_tok, f_all, w_exp_in, b_exp_in, w_exp_out, b_exp_out)
        moe_lat = (slots[:b * s].reshape(b, s, TOP_K), tg_lat, mod_lat[5], y_sorted)
        if not last:
            moe_ctx = (slots[b * s:].reshape(b, l, TOP_K), tg_ctx, mod_ctx[5], y_sorted)
    return _final_call(x_lat, *moe_lat[:3], g_final, moe_lat[3], tm_comb)
```

```python
import functools

import numpy as np
import jax
import jax.numpy as jnp
from jax import lax
from jax.experimental import pallas as pl
from jax.experimental.pallas import tpu as pltpu

F32 = jnp.float32
BF16 = jnp.bfloat16
U32 = jnp.uint32
I32 = jnp.int32

GRID_W = 64
HEAD_DIM = 64
NA_HEADS = 8
NA_WIN_ROWS = 8
NA_WIN_COLS = 16
SC_WIDTH = 1024
SC_CONV_WIDTH = 3
WA_Q_HEADS = 8
WA_KV_HEADS = 2
WA_GROUP = WA_Q_HEADS // WA_KV_HEADS
WA_WINDOW = 128
ROPE_BASE = 10000.0
N_BRANCHES = 3
TOP_K = 4
SWIGLU_LIMIT = 7.0
SWIGLU_ALPHA = 1.702
N_ADA = 6
RMS_EPS = 1e-6
NEG_INF = -1e30

NA_DIM = NA_HEADS * HEAD_DIM
WA_Q_DIM = WA_Q_HEADS * HEAD_DIM
WA_KV_DIM = WA_KV_HEADS * HEAD_DIM

LANES = 128
VMEM_BUDGET = 56 * 1024 * 1024

REF_OFF_NA_Q = 2 * NA_DIM + 2 * WA_KV_DIM
REF_OFF_WA_Q = REF_OFF_NA_Q + NA_DIM
REF_OFF_SC = REF_OFF_WA_Q + WA_Q_DIM
REF_OFF_GATES = REF_OFF_SC + 3 * SC_WIDTH

OFF_SC_B = 0
OFF_SC_C = OFF_SC_B + SC_WIDTH
OFF_SC_H = OFF_SC_C + SC_WIDTH
OFF_WA_Q = OFF_SC_H + SC_WIDTH
OFF_NA_Q = OFF_WA_Q + WA_Q_DIM
OFF_GATES = OFF_NA_Q + NA_DIM
KV_NA_K = 0
KV_NA_V = KV_NA_K + NA_DIM
KV_WA_K = KV_NA_V + NA_DIM
KV_WA_V = KV_WA_K + WA_KV_DIM
N_KV_COLS = KV_WA_V + WA_KV_DIM

NA_QROWS = 4
NA_SLAB_ROWS = NA_QROWS + NA_WIN_ROWS - 1
NA_TQ = NA_QROWS * GRID_W
NA_TK = NA_SLAB_ROWS * GRID_W
NA_STEP_HEADS = 4
WA_TQ = 256
WA_TK = WA_TQ + 2 * WA_WINDOW

MOE_BLOCK = 512
MOE_BUFS = 3
DMA_QUEUES = 2
LOG2E = 1.4426950408889634


def _params(semantics, vmem_bytes):
    limit = int(min(max(vmem_bytes * 5 // 4 + (4 << 20), 32 << 20), VMEM_BUDGET))
    return pltpu.CompilerParams(dimension_semantics=semantics, vmem_limit_bytes=limit)


def _pick(n, candidates):
    for c in candidates:
        if n % c == 0:
            return c
    return n


def _dot(a, b):
    return jnp.dot(a, b, preferred_element_type=F32)


def _dot_nt(a, b):
    return lax.dot_general(a, b, (((1,), (1,)), ((), ())), preferred_element_type=F32)


def _rms_mod(x, g, shift, scale):
    y = x * lax.rsqrt(jnp.mean(x * x, axis=-1, keepdims=True) + RMS_EPS)
    return (y * g) * (1.0 + scale) + shift


def _ada_kernel(c_ref, w_ref, b_ref, o_ref):
    c = c_ref[...]
    s = c * jax.nn.sigmoid(c)
    o_ref[0] = jnp.dot(s, w_ref[0], preferred_element_type=F32,
                       precision=lax.Precision.HIGHEST) + b_ref[0]


def _ada_call(cvec, w_ada, b_ada):
    depth, d, n = w_ada.shape
    tn = _pick(n, (1024, 768, 512, 256, 128))
    return pl.pallas_call(
        _ada_kernel,
        grid=(depth, n // tn),
        in_specs=[pl.BlockSpec((8, d), lambda l, j: (0, 0)),
                  pl.BlockSpec((1, d, tn), lambda l, j: (l, 0, j)),
                  pl.BlockSpec((1, 1, tn), lambda l, j: (l, 0, j))],
        out_specs=pl.BlockSpec((1, 8, tn), lambda l, j: (l, 0, j)),
        out_shape=jax.ShapeDtypeStruct((depth, 8, n), F32),
        compiler_params=_params(("arbitrary", "arbitrary"), 2 * d * tn * 4),
        name="ada",
    )(cvec, w_ada, b_ada.reshape(depth, 1, n))


def _rope128(x, cos, sin):
    lane = lax.broadcasted_iota(I32, x.shape, 1)
    fwd = pltpu.roll(x, LANES - HEAD_DIM // 4, 1)
    bwd = pltpu.roll(x, HEAD_DIM // 4, 1)
    partner = jnp.where((lane & (HEAD_DIM // 4)) == 0, fwd, bwd)
    return x * cos + partner * sin


def _gather_rows(y_hbm, ybuf, sem, slot_ref, r, buf, sub):
    src = pl.multiple_of(slot_ref[0, 0, r] * sub, sub)
    dst = pl.multiple_of(r * sub, sub)
    return pltpu.make_async_copy(y_hbm.at[pl.ds(src, sub), :], ybuf.at[buf, pl.ds(dst, sub), :], sem.at[buf])


def _wait_rows(y_hbm, ybuf, sem, buf):
    pltpu.make_async_copy(y_hbm.at[pl.ds(0, ybuf.shape[1]), :], ybuf.at[buf], sem.at[buf]).wait()


def _combine_rows(ybuf, buf, gates, tm, sub):
    acc = None
    for k in range(TOP_K):
        yk = _load_token_tiles(ybuf.at[buf, pl.ds(k * tm * sub, tm * sub), :], tm, F32) * gates[:, k:k + 1]
        acc = yk if acc is None else acc + yk
    return acc


def _proj_kernel(*refs, rope_tiles, combine, per):
    refs = list(refs)
    if combine:
        slot_cur_ref, slot_nxt_ref, gate_ref, gtf_ref, y_hbm = refs[:5]
        refs = refs[5:]
    x_ref, g_ref, sh_ref, sc_ref = refs[:4]
    refs = refs[4:]
    if rope_tiles:
        cos_ref, sin_ref = refs[:2]
        refs = refs[2:]
    w_ref, o_ref = refs[:2]
    refs = refs[2:]
    if combine:
        xo_ref, h_ref, ybuf, sem = refs
    else:
        (h_ref,) = refs
    j = pl.program_id(2)
    tm, d = x_ref.shape[1], x_ref.shape[2]
    sub = d // 2 // LANES
    if combine:
        t = pl.program_id(0) * pl.num_programs(1) + pl.program_id(1)
        cur = t % 2
        last_step = jnp.logical_and(t == pl.num_programs(0) * pl.num_programs(1) - 1, j == pl.num_programs(2) - 1)

    @pl.when(j == 0)
    def _():
        if combine:
            @pl.when(t == 0)
            def _():
                def body(r, carry):
                    _gather_rows(y_hbm, ybuf, sem, slot_cur_ref, r, 0, sub).start()
                    return carry
                lax.fori_loop(0, ybuf.shape[1] // sub, body, 0)

            _wait_rows(y_hbm, ybuf, sem, cur)
            x = x_ref[0] + gtf_ref[0] * _combine_rows(ybuf, cur, gate_ref[0], tm, sub)
            xo_ref[0] = x
        else:
            x = x_ref[0]
        h_ref[...] = _rms_mod(x, g_ref[0], sh_ref[0], sc_ref[0]).astype(BF16)

    if combine:
        for q in range(per):
            _gather_rows(y_hbm, ybuf, sem, slot_nxt_ref, j * per + q, 1 - cur, sub).start(priority=q % DMA_QUEUES)
    acc = _dot(h_ref[...], w_ref[0])
    o_ref[0] = acc.astype(BF16)
    for jt, lo, width in rope_tiles:
        @pl.when(j == jt)
        def _(lo=lo, width=width):
            cos, sin = cos_ref[...], sin_ref[...]
            for c0 in range(lo, lo + width, LANES):
                o_ref[0, :, c0:c0 + LANES] = _rope128(acc[:, c0:c0 + LANES], cos, sin).astype(BF16)
    if combine:
        @pl.when(last_step)
        def _():
            _wait_rows(y_hbm, ybuf, sem, 1 - cur)


def _tile_slots(slots, tm, n_rows):
    b, n, _ = slots.shape
    st = slots.reshape(b, n // tm, tm, TOP_K).transpose(0, 1, 3, 2).reshape(b * (n // tm), 1, TOP_K * tm)
    return jnp.pad(st, ((0, 0), (0, 0), (0, n_rows - TOP_K * tm)))


def _proj_call(x, g, shift, scale, w_all, layer, col0, nc, rope, tm, tn, rope_cols, moe=None):
    b, n, d = x.shape
    assert col0 % tn == 0 and nc % tn == 0 and n % tm == 0
    jb = col0 // tn
    ni, nj = n // tm, nc // tn
    sub = d // 2 // LANES
    rope_tiles = []
    if rope is not None:
        for off, width in rope_cols:
            assert off // tn == (off + width - 1) // tn and off % LANES == 0 and width % LANES == 0
            rope_tiles.append((off // tn, off % tn, width))
    in_specs, args, per = [], [], 0
    vmem = 2 * tm * d * 4 + tm * d * 2 + 2 * d * tn * 2 + 2 * tm * tn * 2 + tm * tn * 4 + 4 * tm * LANES * 4
    if moe is not None:
        slots, gates, gt, y_sorted = moe
        per = -(-TOP_K * tm // nj)
        n_rows = per * nj
        slots_t = _tile_slots(slots, tm, n_rows)
        n_t = b * ni

        def smem(index):
            return pl.BlockSpec((1, 1, n_rows), index, memory_space=pltpu.SMEM)

        in_specs += [smem(lambda bb, i, j: (bb * ni + i, 0, 0)),
                     smem(lambda bb, i, j: (jnp.minimum(bb * ni + i + 1, n_t - 1), 0, 0)),
                     pl.BlockSpec((1, tm, LANES), lambda bb, i, j: (bb, i, 0)),
                     pl.BlockSpec((1, 1, d), lambda bb, i, j: (bb, 0, 0)),
                     pl.BlockSpec(memory_space=pl.ANY)]
        args += [slots_t, slots_t, gates, gt, y_sorted]
        vmem += 2 * tm * d * 4 + 2 * n_rows * sub * LANES * 4 + 3 * tm * d * 4
    in_specs += [pl.BlockSpec((1, tm, d), lambda bb, i, j: (bb, i, 0)),
                 pl.BlockSpec((1, 1, d), lambda bb, i, j: (layer, 0, 0)),
                 pl.BlockSpec((1, 1, d), lambda bb, i, j: (bb, 0, 0)),
                 pl.BlockSpec((1, 1, d), lambda bb, i, j: (bb, 0, 0))]
    args += [x, g.reshape(g.shape[0], 1, d), shift, scale]
    if rope is not None:
        in_specs += [pl.BlockSpec((tm, LANES), lambda bb, i, j: (i, 0))] * 2
        args += list(rope)
    in_specs.append(pl.BlockSpec((1, d, tn), lambda bb, i, j: (layer, 0, jb + j)))
    args.append(w_all)
    out_specs = [pl.BlockSpec((1, tm, tn), lambda bb, i, j: (bb, i, j))]
    out_shape = [jax.ShapeDtypeStruct((b, n, nc), BF16)]
    scratch = [pltpu.VMEM((tm, d), BF16)]
    if moe is not None:
        out_specs.append(pl.BlockSpec((1, tm, d), lambda bb, i, j: (bb, i, 0)))
        out_shape.append(jax.ShapeDtypeStruct((b, n, d), F32))
        scratch += [pltpu.VMEM((2, n_rows * sub, LANES), U32), pltpu.SemaphoreType.DMA((2,))]
    limit = int(min(max(vmem * 5 // 4 + (4 << 20), 32 << 20), VMEM_BUDGET))
    out = pl.pallas_call(
        functools.partial(_proj_kernel, rope_tiles=tuple(rope_tiles), combine=moe is not None, per=per),
        grid=(b, ni, nj),
        in_specs=in_specs,
        out_specs=out_specs,
        out_shape=out_shape,
        scratch_shapes=scratch,
        compiler_params=pltpu.CompilerParams(dimension_semantics=("arbitrary", "arbitrary", "arbitrary"),
                                             vmem_limit_bytes=limit, disable_bounds_checks=moe is not None),
        name="proj",
    )(*args)
    return out if moe is not None else out[0]


def _softmax_av(parts, extra=None):
    m = functools.reduce(jnp.maximum, [jnp.max(s, axis=-1, keepdims=True) for s, _ in parts])
    if extra is not None:
        m = jnp.maximum(m, extra)
    l = 0.0 if extra is None else jnp.exp2(extra - m)
    o = None
    for s, v in parts:
        p = jnp.exp2(s - m)
        l = l + jnp.sum(p, axis=-1, keepdims=True)
        pv = _dot(p.astype(BF16), v)
        o = pv if o is None else o + pv
    return o / l


def _na_kernel(q_ref, k_ref, v_ref, kc_ref, vc_ref, bias_ref, o_ref, *, rows):
    i = pl.program_id(2)
    r_start = jnp.clip(NA_QROWS * i - NA_WIN_ROWS // 2, 0, rows - NA_SLAB_ROWS)
    start = pl.multiple_of(r_start * GRID_W, GRID_W)
    kslab = k_ref[0, pl.ds(start, NA_TK), :]
    vslab = v_ref[0, pl.ds(start, NA_TK), :]
    q, kc, vc = q_ref[0], kc_ref[0], vc_ref[0]
    outs = []
    for hh in range(NA_STEP_HEADS):
        sl = slice(hh * HEAD_DIM, (hh + 1) * HEAD_DIM)
        qh = q[:, sl]
        s_nb = _dot_nt(qh, kslab[:, sl]) + bias_ref[0, hh]
        s_ctx = _dot_nt(qh, kc[:, sl])
        outs.append(_softmax_av([(s_nb, vslab[:, sl]), (s_ctx, vc[:, sl])]))
    o_ref[0] = jnp.concatenate(outs, axis=1).astype(BF16)


def _na_bias_tables(rpb, rows):
    nblk = rows // NA_QROWS
    assert nblk >= 3 and rows >= NA_SLAB_ROWS
    n_heads, n_dr, n_dc = rpb.shape
    half = NA_WIN_COLS - 1
    vec = jnp.zeros((n_heads, n_dr, LANES), F32)
    vec = vec.at[..., :n_dc - half].set(rpb[..., half:]).at[..., LANES - half:].set(rpb[..., :half])
    toep = jnp.tile(vec, (1, 1, GRID_W))[..., :GRID_W * (LANES - 1)]
    toep = toep.reshape(n_heads, n_dr, GRID_W, LANES - 1)[..., :GRID_W]
    col = np.arange(GRID_W)
    c0 = np.clip(col - NA_WIN_COLS // 2, 0, GRID_W - NA_WIN_COLS)
    col_ok = (col[None, :] >= c0[:, None]) & (col[None, :] < c0[:, None] + NA_WIN_COLS)
    toep = jnp.where(col_ok[None, None], toep, NEG_INF)
    masked = jnp.full((n_heads, GRID_W, GRID_W), NEG_INF, F32)
    cases = []
    for i in (0, 1, nblk - 1):
        rs = int(np.clip(NA_QROWS * i - NA_WIN_ROWS // 2, 0, rows - NA_SLAB_ROWS))
        q_rows = []
        for rl in range(NA_QROWS):
            r = NA_QROWS * i + rl
            r0 = int(np.clip(r - NA_WIN_ROWS // 2, 0, rows - NA_WIN_ROWS))
            tiles = []
            for kl in range(NA_SLAB_ROWS):
                kr = rs + kl
                tiles.append(toep[:, kr - r + NA_WIN_ROWS - 1] if r0 <= kr < r0 + NA_WIN_ROWS else masked)
            q_rows.append(jnp.concatenate(tiles, axis=2))
        cases.append(jnp.concatenate(q_rows, axis=1))
    return jnp.stack(cases)


def _na_call(p_lat, p_ctx, bias, kv_off_lat, kv_off_ctx):
    b, s, _ = p_lat.shape
    l = p_ctx.shape[1]
    rows = s // GRID_W
    nblk = rows // NA_QROWS
    w = NA_STEP_HEADS * HEAD_DIM
    hp = NA_DIM // w
    assert OFF_NA_Q % w == 0 and kv_off_lat % w == 0 and kv_off_ctx % w == 0
    qb, kb, vb = OFF_NA_Q // w, (kv_off_lat + KV_NA_K) // w, (kv_off_lat + KV_NA_V) // w
    kcb, vcb = (kv_off_ctx + KV_NA_K) // w, (kv_off_ctx + KV_NA_V) // w

    def case(i):
        return jnp.where(i == 0, 0, jnp.where(i == nblk - 1, 2, 1))

    vmem = 4 * s * w * 2 + 2 * NA_STEP_HEADS * NA_TQ * NA_TK * 4 + 8 * NA_TQ * (NA_TK + l) * 4
    return pl.pallas_call(
        functools.partial(_na_kernel, rows=rows),
        grid=(b, hp, nblk),
        in_specs=[pl.BlockSpec((1, NA_TQ, w), lambda bb, h, i: (bb, i, qb + h)),
                  pl.BlockSpec((1, s, w), lambda bb, h, i: (bb, 0, kb + h)),
                  pl.BlockSpec((1, s, w), lambda bb, h, i: (bb, 0, vb + h)),
                  pl.BlockSpec((1, l, w), lambda bb, h, i: (bb, 0, kcb + h)),
                  pl.BlockSpec((1, l, w), lambda bb, h, i: (bb, 0, vcb + h)),
                  pl.BlockSpec((1, NA_STEP_HEADS, NA_TQ, NA_TK), lambda bb, h, i: (case(i), h, 0, 0))],
        out_specs=pl.BlockSpec((1, NA_TQ, w), lambda bb, h, i: (bb, i, h)),
        out_shape=jax.ShapeDtypeStruct((b, s, NA_DIM), BF16),
        compiler_params=_params(("arbitrary", "arbitrary", "arbitrary"), vmem),
        name="natten",
    )(p_lat, p_lat, p_lat, p_ctx, p_ctx, bias)


def _wa_kernel(sink_ref, q_ref, k_ref, v_ref, kc_ref, vc_ref, o_ref, *, seq):
    i = pl.program_id(1)
    start = pl.multiple_of(jnp.clip(WA_TQ * i - WA_WINDOW, 0, seq - WA_TK), WA_WINDOW)
    kslab = k_ref[0, pl.ds(start, WA_TK), :]
    vslab = v_ref[0, pl.ds(start, WA_TK), :]
    qpos = WA_TQ * i + lax.broadcasted_iota(I32, (WA_TQ, WA_TK), 0)
    kpos = start + lax.broadcasted_iota(I32, (WA_TQ, WA_TK), 1)
    band = jnp.where(jnp.abs(kpos - qpos) <= WA_WINDOW, 0.0, NEG_INF).astype(F32)
    q, kc, vc = q_ref[0], kc_ref[0], vc_ref[0]
    outs = []
    for hq in range(WA_Q_HEADS):
        hk = hq // WA_GROUP
        sk = slice(hk * HEAD_DIM, (hk + 1) * HEAD_DIM)
        qh = q[:, hq * HEAD_DIM:(hq + 1) * HEAD_DIM]
        s_loc = _dot_nt(qh, kslab[:, sk]) + band
        s_ctx = _dot_nt(qh, kc[:, sk])
        outs.append(_softmax_av([(s_loc, vslab[:, sk]), (s_ctx, vc[:, sk])], extra=sink_ref[hq]))
    o_ref[0] = jnp.concatenate(outs, axis=1).astype(BF16)


def _wa_call(p_lat, p_ctx, sinks, kv_off_lat, kv_off_ctx):
    b, s, _ = p_lat.shape
    l = p_ctx.shape[1]
    assert s % WA_TQ == 0 and s >= WA_TK
    qb = OFF_WA_Q // WA_Q_DIM
    kb, vb = (kv_off_lat + KV_WA_K) // LANES, (kv_off_lat + KV_WA_V) // LANES
    kcb, vcb = (kv_off_ctx + KV_WA_K) // LANES, (kv_off_ctx + KV_WA_V) // LANES
    vmem = 4 * s * LANES * 2 + 4 * WA_TQ * WA_Q_DIM * 2 + 10 * WA_TQ * (WA_TK + l) * 4
    return pl.pallas_call(
        functools.partial(_wa_kernel, seq=s),
        grid=(b, s // WA_TQ),
        in_specs=[pl.BlockSpec(memory_space=pltpu.SMEM),
                  pl.BlockSpec((1, WA_TQ, WA_Q_DIM), lambda bb, i: (bb, i, qb)),
                  pl.BlockSpec((1, s, LANES), lambda bb, i: (bb, 0, kb)),
                  pl.BlockSpec((1, s, LANES), lambda bb, i: (bb, 0, vb)),
                  pl.BlockSpec((1, l, LANES), lambda bb, i: (bb, 0, kcb)),
                  pl.BlockSpec((1, l, LANES), lambda bb, i: (bb, 0, vcb))],
        out_specs=pl.BlockSpec((1, WA_TQ, WA_Q_DIM), lambda bb, i: (bb, i, 0)),
        out_shape=jax.ShapeDtypeStruct((b, s, WA_Q_DIM), BF16),
        compiler_params=_params(("arbitrary", "arbitrary"), vmem),
        name="winattn",
    )(sinks, p_lat, p_lat, p_lat, p_ctx, p_ctx)


def _ctx_attn_kernel(sink_ref, naq_ref, nak_ref, nav_ref, waq_ref, wak_ref, wav_ref, ona_ref, owa_ref):
    q, k, v = naq_ref[0], nak_ref[0], nav_ref[0]
    outs = []
    for h in range(NA_HEADS):
        sl = slice(h * HEAD_DIM, (h + 1) * HEAD_DIM)
        outs.append(_softmax_av([(_dot_nt(q[:, sl], k[:, sl]), v[:, sl])]))
    ona_ref[0] = jnp.concatenate(outs, axis=1).astype(BF16)
    q, k, v = waq_ref[0], wak_ref[0], wav_ref[0]
    outs = []
    for hq in range(WA_Q_HEADS):
        sk = slice((hq // WA_GROUP) * HEAD_DIM, (hq // WA_GROUP + 1) * HEAD_DIM)
        qh = q[:, hq * HEAD_DIM:(hq + 1) * HEAD_DIM]
        outs.append(_softmax_av([(_dot_nt(qh, k[:, sk]), v[:, sk])], extra=sink_ref[hq]))
    owa_ref[0] = jnp.concatenate(outs, axis=1).astype(BF16)


def _ctx_attn_call(p_ctx, sinks, kv_off):
    b, l, _ = p_ctx.shape

    def spec(width, off):
        return pl.BlockSpec((1, l, width), lambda bb: (bb, 0, off // width))

    return pl.pallas_call(
        _ctx_attn_kernel,
        grid=(b,),
        in_specs=[pl.BlockSpec(memory_space=pltpu.SMEM),
                  spec(NA_DIM, OFF_NA_Q), spec(NA_DIM, kv_off + KV_NA_K), spec(NA_DIM, kv_off + KV_NA_V),
                  spec(WA_Q_DIM, OFF_WA_Q), spec(WA_KV_DIM, kv_off + KV_WA_K), spec(WA_KV_DIM, kv_off + KV_WA_V)],
        out_specs=[pl.BlockSpec((1, l, NA_DIM), lambda bb: (bb, 0, 0)),
                   pl.BlockSpec((1, l, WA_Q_DIM), lambda bb: (bb, 0, 0))],
        out_shape=[jax.ShapeDtypeStruct((b, l, NA_DIM), BF16), jax.ShapeDtypeStruct((b, l, WA_Q_DIM), BF16)],
        compiler_params=_params(("arbitrary",), 16 * l * NA_DIM * 4),
        name="ctxattn",
    )(sinks, p_ctx, p_ctx, p_ctx, p_ctx, p_ctx, p_ctx)


def _pack_bf16_pairs(x):
    n = x.shape[1] // 2
    lo = pltpu.bitcast(x[:, :n].astype(BF16).astype(F32), U32)
    hi = pltpu.bitcast(x[:, n:].astype(BF16).astype(F32), U32)
    return (hi & jnp.uint32(0xFFFF0000)) | (lo >> 16)


def _store_token_tiles(ref, packed):
    m, n = packed.shape
    sub = n // LANES
    for s in range(sub):
        ref[pl.ds(s, m, stride=sub), :] = packed[:, s * LANES:(s + 1) * LANES]


def _load_token_tiles(ref, m, dtype):
    sub = ref.shape[0] // m
    chunks = [ref[pl.ds(s, m, stride=sub), :] for s in range(sub)]
    lo = [pltpu.bitcast(p << 16, F32).astype(dtype) for p in chunks]
    hi = [pltpu.bitcast(p & jnp.uint32(0xFFFF0000), F32).astype(dtype) for p in chunks]
    return jnp.concatenate(lo + hi, axis=1)


def _top_k_lanes(logits, n_exp):
    m_rows = logits.shape[0]
    col = lax.broadcasted_iota(I32, (m_rows, n_exp), 1).astype(F32)
    lane = lax.broadcasted_iota(I32, (m_rows, LANES), 1)
    work = logits
    vals = jnp.zeros((m_rows, LANES), F32)
    idxs = jnp.zeros((m_rows, LANES), F32)
    top = None
    denom = 0.0
    for k in range(TOP_K):
        mx = jnp.max(work, axis=-1, keepdims=True)
        ix = jnp.min(jnp.where(work == mx, col, float(n_exp)), axis=-1, keepdims=True)
        work = jnp.where(col == ix, -jnp.inf, work)
        top = mx if top is None else top
        e = jnp.exp(mx - top)
        denom = denom + e
        vals = jnp.where(lane == k, e, vals)
        idxs = jnp.where(lane == k, ix, idxs)
    return vals / denom, idxs.astype(I32)


def _merge_kernel(*refs, has_halo, tn):
    (ona_ref, owa_ref, scb_ref, scc_ref, sch_ref) = refs[:5]
    refs = refs[5:]
    if has_halo:
        cprev_ref, hprev_ref, cnext_ref, hnext_ref = refs[:4]
        refs = refs[4:]
    gna_ref, gsc_ref, gwa_ref, conv_ref, wna_ref, wsc_ref, wwa_ref, m_ref = refs
    i = pl.program_id(1)
    tm, d = m_ref.shape[1], m_ref.shape[2]

    u = scc_ref[0].astype(F32) * sch_ref[0].astype(F32)
    zero = jnp.zeros((1, SC_WIDTH), F32)
    if has_halo:
        u_prev = jnp.where(i == 0, zero, cprev_ref[0, 7:8, :].astype(F32) * hprev_ref[0, 7:8, :].astype(F32))
        u_next = jnp.where(i == pl.num_programs(1) - 1, zero,
                           cnext_ref[0, 0:1, :].astype(F32) * hnext_ref[0, 0:1, :].astype(F32))
    else:
        u_prev = u_next = zero
    row = lax.broadcasted_iota(I32, u.shape, 0)
    u_m1 = jnp.where(row == 0, u_prev, pltpu.roll(u, 1, 0))
    u_p1 = jnp.where(row == tm - 1, u_next, pltpu.roll(u, tm - 1, 0))
    y = u_m1 * conv_ref[0, 0:1, :] + u * conv_ref[0, 1:2, :] + u_p1 * conv_ref[0, 2:3, :]
    o_sc = (scb_ref[0].astype(F32) * y).astype(BF16)
    o_na, o_wa = ona_ref[0], owa_ref[0]

    def gate(ref, cs):
        return jax.nn.sigmoid(ref[0, :, cs].astype(F32))

    for n in range(d // tn):
        cs = slice(n * tn, (n + 1) * tn)
        merged = (gate(gna_ref, cs) * _dot(o_na, wna_ref[0, :, cs])
                  + gate(gsc_ref, cs) * _dot(o_sc, wsc_ref[0, :, cs])
                  + gate(gwa_ref, cs) * _dot(o_wa, wwa_ref[0, :, cs]))
        m_ref[0, :, cs] = merged.astype(BF16)


def _outproj_kernel(x_ref, m_ref, wout_ref, gt_ref, gffn_ref, shf_ref, scf_ref, wr_ref, br_ref,
                    xo_ref, f_ref, te_ref, tg_ref, *, n_exp):
    xn = x_ref[0] + gt_ref[0] * _dot(m_ref[0], wout_ref[0])
    xo_ref[0] = xn
    f = _rms_mod(xn, gffn_ref[0], shf_ref[0], scf_ref[0])
    _store_token_tiles(f_ref.at[0], _pack_bf16_pairs(f))
    f_hi = f.astype(BF16)
    f_lo = (f - f_hi.astype(F32)).astype(BF16)
    wr = wr_ref[0]
    hi_both = _dot(f_hi, wr)
    logits = (hi_both[:, :n_exp] + _dot(f_lo, wr[:, :n_exp]) + hi_both[:, n_exp:]) + br_ref[0]
    gates, idx = _top_k_lanes(logits, n_exp)
    tg_ref[0] = gates
    te_ref[0] = idx


def _merge_call(x, o_na, o_wa, p, layer, conv_w, w_na, w_sc, w_wa, w_out, gt, g_ffn, sh_f, sc_f, w_r2, b_r, tm, tn):
    b, n_tok, d = x.shape
    depth = w_out.shape[0]
    n_exp = b_r.shape[-1]
    sub = d // 2 // LANES
    nt = n_tok // tm
    has_halo = nt > 1
    assert OFF_GATES % d == 0 and d % tn == 0 and n_tok % tm == 0 and tm % 8 == 0
    gate_b = OFF_GATES // d
    rb = tm // 8

    def rows(width, off):
        return pl.BlockSpec((1, tm, width), lambda bb, i: (bb, i, off // width))

    def resident(shape):
        return pl.BlockSpec((1,) + shape, lambda bb, i: (layer, 0, 0), pipeline_mode=pl.Buffered(1))

    in_specs = [pl.BlockSpec((1, tm, NA_DIM), lambda bb, i: (bb, i, 0)),
                pl.BlockSpec((1, tm, WA_Q_DIM), lambda bb, i: (bb, i, 0)),
                rows(SC_WIDTH, OFF_SC_B), rows(SC_WIDTH, OFF_SC_C), rows(SC_WIDTH, OFF_SC_H)]
    args = [o_na, o_wa, p, p, p]
    if has_halo:
        last8 = n_tok // 8 - 1
        for off in (OFF_SC_C, OFF_SC_H):
            in_specs.append(pl.BlockSpec((1, 8, SC_WIDTH),
                                         lambda bb, i, off=off: (bb, jnp.maximum(i * rb - 1, 0), off // SC_WIDTH)))
        for off in (OFF_SC_C, OFF_SC_H):
            in_specs.append(pl.BlockSpec((1, 8, SC_WIDTH),
                                         lambda bb, i, off=off: (bb, jnp.minimum((i + 1) * rb, last8), off // SC_WIDTH)))
        args += [p, p, p, p]
    for br in range(N_BRANCHES):
        in_specs.append(pl.BlockSpec((1, tm, d), lambda bb, i, br=br: (bb, i, gate_b + br)))
        args.append(p)
    in_specs += [resident((SC_CONV_WIDTH, SC_WIDTH)),
                 resident((NA_DIM, d)), resident((SC_WIDTH, d)), resident((WA_Q_DIM, d))]
    args += [conv_w, w_na, w_sc, w_wa]
    vmem = (2 * tm * (NA_DIM + WA_Q_DIM + 3 * SC_WIDTH + N_BRANCHES * d + d) * 2
            + (NA_DIM + SC_WIDTH + WA_Q_DIM) * d * 2 + 4 * tm * SC_WIDTH * 4 + 3 * tm * tn * 4)
    merged = pl.pallas_call(
        functools.partial(_merge_kernel, has_halo=has_halo, tn=tn),
        grid=(b, nt),
        in_specs=in_specs,
        out_specs=pl.BlockSpec((1, tm, d), lambda bb, i: (bb, i, 0)),
        out_shape=jax.ShapeDtypeStruct((b, n_tok, d), BF16),
        compiler_params=_params(("arbitrary", "arbitrary"), vmem),
        name="merge",
    )(*args)

    def per_batch():
        return pl.BlockSpec((1, 1, d), lambda bb, i: (bb, 0, 0))

    out_specs = [pl.BlockSpec((1, tm, d), lambda bb, i: (bb, i, 0)),
                 pl.BlockSpec((1, tm * sub, LANES), lambda bb, i: (bb, i, 0)),
                 pl.BlockSpec((1, tm, LANES), lambda bb, i: (bb, i, 0)),
                 pl.BlockSpec((1, tm, LANES), lambda bb, i: (bb, i, 0))]
    out_shape = [jax.ShapeDtypeStruct((b, n_tok, d), F32), jax.ShapeDtypeStruct((b, n_tok * sub, LANES), U32),
                 jax.ShapeDtypeStruct((b, n_tok, LANES), I32), jax.ShapeDtypeStruct((b, n_tok, LANES), F32)]
    vmem = (4 * tm * d * 4 + 2 * tm * d * 2 + 2 * tm * (d // 2) * 4 + 4 * tm * LANES * 4
            + d * d * 2 + d * 2 * n_exp * 4 + 4 * tm * d * 4)
    return pl.pallas_call(
        functools.partial(_outproj_kernel, n_exp=n_exp),
        grid=(b, nt),
        in_specs=[pl.BlockSpec((1, tm, d), lambda bb, i: (bb, i, 0)),
                  pl.BlockSpec((1, tm, d), lambda bb, i: (bb, i, 0)),
                  resident((d, d)), per_batch(), resident((1, d)), per_batch(), per_batch(),
                  resident((d, 2 * n_exp)), resident((1, n_exp))],
        out_specs=out_specs,
        out_shape=out_shape,
        compiler_params=_params(("arbitrary", "arbitrary"), vmem),
        name="outproj",
    )(x, merged, w_out, gt, g_ffn.reshape(depth, 1, d), sh_f, sc_f, w_r2, b_r.reshape(depth, 1, n_exp))


def _moe_kernel(be_ref, nused_ref, tok_cur_ref, tok_nxt_ref, tok_far_ref, f_hbm, w1_ref, b1_ref, w2_ref, b2_ref,
                y_ref, xbuf, w1b, w2b, gsem, *, d_exp, sub):
    i = pl.program_id(0)
    n_used = nused_ref[0]
    slot = i % MOE_BUFS
    nxt = (i + 1) % MOE_BUFS
    far = (i + 2) % MOE_BUFS
    rows = MOE_BLOCK * sub

    def gather(tok_ref, r, dst_slot):
        src = pl.multiple_of(tok_ref[0, 0, r] * sub, sub)
        return pltpu.make_async_copy(f_hbm.at[pl.ds(src, sub), :], xbuf.at[dst_slot, pl.ds(r * sub, sub), :],
                                     gsem.at[dst_slot])

    def wait_gather(s):
        pltpu.make_async_copy(f_hbm.at[pl.ds(0, rows), :], xbuf.at[s], gsem.at[s]).wait()

    @pl.when(i < n_used)
    def _():
        @pl.when(i == 0)
        def _():
            def body(r, carry):
                gather(tok_cur_ref, r, 0).start()
                gather(tok_nxt_ref, r, 1).start()
                return carry
            lax.fori_loop(0, MOE_BLOCK, body, 0)

        @pl.when(jnp.logical_or(i == 0, be_ref[i] != be_ref[jnp.maximum(i - 1, 0)]))
        def _():
            w1b[...] = w1_ref[0, 0].astype(BF16)
            w2b[...] = w2_ref[0, 0].astype(BF16)

        wait_gather(slot)
        x = _load_token_tiles(xbuf.at[slot], MOE_BLOCK, BF16)
        n_chunks = 2
        cw = d_exp // n_chunks
        for r in range(MOE_BLOCK):
            gather(tok_far_ref, r, far).start(priority=r % DMA_QUEUES)
        y = None
        for c in range(n_chunks):
            glu = _dot(x, w1b[:, c * cw:(c + 1) * cw]) + b1_ref[0, 0, :, c * cw:(c + 1) * cw]
            lin = _dot(x, w1b[:, d_exp + c * cw:d_exp + (c + 1) * cw]) + b1_ref[0, 0, :, d_exp + c * cw:d_exp + (c + 1) * cw]
            glu = jnp.minimum(glu, SWIGLU_LIMIT)
            lin = jnp.clip(lin, -SWIGLU_LIMIT, SWIGLU_LIMIT)
            act = glu * jax.nn.sigmoid(SWIGLU_ALPHA * glu) * (lin + 1.0)
            yc = _dot(act.astype(BF16), w2b[c * cw:(c + 1) * cw, :])
            y = yc if y is None else y + yc
        _store_token_tiles(y_ref, _pack_bf16_pairs(y + b2_ref[0, 0]))

        @pl.when(i == n_used - 1)
        def _():
            wait_gather(nxt)
            wait_gather(far)

    @pl.when(i >= n_used)
    def _():
        y_ref[...] = jnp.zeros_like(y_ref)


def _moe_call(layer, block_e, n_used, slot_tok, f_all, w1, b1, w2, b2):
    n_blocks = block_e.shape[0]
    depth, n_exp, d, two_de = w1.shape
    d_exp = two_de // 2
    sub = d // 2 // LANES
    assert d_exp % (2 * LANES) == 0
    rows = MOE_BLOCK * sub

    def smem(index):
        return pl.BlockSpec((1, 1, MOE_BLOCK), index, memory_space=pltpu.SMEM)

    grid_spec = pltpu.PrefetchScalarGridSpec(
        num_scalar_prefetch=2,
        grid=(n_blocks,),
        in_specs=[smem(lambda i, be, nu: (i, 0, 0)),
                  smem(lambda i, be, nu: (jnp.minimum(i + 1, n_blocks - 1), 0, 0)),
                  smem(lambda i, be, nu: (jnp.minimum(i + 2, n_blocks - 1), 0, 0)),
                  pl.BlockSpec(memory_space=pl.ANY),
                  pl.BlockSpec((1, 1, d, two_de), lambda i, be, nu: (layer, be[i], 0, 0)),
                  pl.BlockSpec((1, 1, 1, two_de), lambda i, be, nu: (layer, be[i], 0, 0)),
                  pl.BlockSpec((1, 1, d_exp, d), lambda i, be, nu: (layer, be[i], 0, 0)),
                  pl.BlockSpec((1, 1, 1, d), lambda i, be, nu: (layer, be[i], 0, 0))],
        out_specs=pl.BlockSpec((rows, LANES), lambda i, be, nu: (i, 0)),
        scratch_shapes=[pltpu.VMEM((MOE_BUFS, rows, LANES), U32),
                        pltpu.VMEM((d, two_de), BF16), pltpu.VMEM((d_exp, d), BF16),
                        pltpu.SemaphoreType.DMA((MOE_BUFS,))],
    )
    vmem = (2 * (d * two_de + d_exp * d) * 4 + (d * two_de + d_exp * d) * 2 + 4 * rows * LANES * 4
            + 6 * MOE_BLOCK * (d + two_de) * 4)
    tok3 = slot_tok.reshape(n_blocks, 1, MOE_BLOCK)
    return pl.pallas_call(
        functools.partial(_moe_kernel, d_exp=d_exp, sub=sub),
        grid_spec=grid_spec,
        out_shape=jax.ShapeDtypeStruct((n_blocks * rows, LANES), U32),
        compiler_params=pltpu.CompilerParams(dimension_semantics=("arbitrary",),
                                             vmem_limit_bytes=int(min(vmem + (4 << 20), VMEM_BUDGET)),
                                             disable_bounds_checks=True),
        name="experts",
    )(block_e, n_used, tok3, tok3, tok3, f_all, w1, b1.reshape(depth, n_exp, 1, two_de), w2,
      b2.reshape(depth, n_exp, 1, d))


def _final_kernel(slot_cur_ref, slot_nxt_ref, x_ref, gate_ref, gt_ref, gfin_ref, y_hbm, o_ref, ybuf, sem):
    t = pl.program_id(0) * pl.num_programs(1) + pl.program_id(1)
    n_t = pl.num_programs(0) * pl.num_programs(1)
    tm, d = x_ref.shape[1], x_ref.shape[2]
    sub = d // 2 // LANES
    cur = t % 2

    def issue(slot_ref, buf):
        def body(r, carry):
            for q in range(DMA_QUEUES):
                _gather_rows(y_hbm, ybuf, sem, slot_ref, r * DMA_QUEUES + q, buf, sub).start(priority=q)
            return carry
        lax.fori_loop(0, TOP_K * tm // DMA_QUEUES, body, 0)

    @pl.when(t == 0)
    def _():
        issue(slot_cur_ref, 0)

    @pl.when(t + 1 < n_t)
    def _():
        issue(slot_nxt_ref, 1 - cur)

    _wait_rows(y_hbm, ybuf, sem, cur)
    xn = x_ref[0] + gt_ref[0] * _combine_rows(ybuf, cur, gate_ref[0], tm, sub)
    o_ref[0] = xn * lax.rsqrt(jnp.mean(xn * xn, axis=-1, keepdims=True) + RMS_EPS) * gfin_ref[...]


def _final_call(x, slots, gates, gt, g_final, y_sorted, tm):
    b, n_tok, d = x.shape
    nt = n_tok // tm
    sub = d // 2 // LANES
    n_rows = TOP_K * tm
    slots_t = _tile_slots(slots, tm, n_rows)
    n_t = b * nt

    def smem(index):
        return pl.BlockSpec((1, 1, n_rows), index, memory_space=pltpu.SMEM)

    vmem = 4 * tm * d * 4 + 2 * n_rows * sub * LANES * 4 + 2 * tm * LANES * 4 + 4 * tm * d * 4
    return pl.pallas_call(
        _final_kernel,
        grid=(b, nt),
        in_specs=[smem(lambda bb, i: (bb * nt + i, 0, 0)),
                  smem(lambda bb, i: (jnp.minimum(bb * nt + i + 1, n_t - 1), 0, 0)),
                  pl.BlockSpec((1, tm, d), lambda bb, i: (bb, i, 0)),
                  pl.BlockSpec((1, tm, LANES), lambda bb, i: (bb, i, 0)),
                  pl.BlockSpec((1, 1, d), lambda bb, i: (bb, 0, 0)),
                  pl.BlockSpec((1, d), lambda bb, i: (0, 0)),
                  pl.BlockSpec(memory_space=pl.ANY)],
        out_specs=pl.BlockSpec((1, tm, d), lambda bb, i: (bb, i, 0)),
        out_shape=jax.ShapeDtypeStruct((b, n_tok, d), F32),
        scratch_shapes=[pltpu.VMEM((2, n_rows * sub, LANES), U32), pltpu.SemaphoreType.DMA((2,))],
        compiler_params=pltpu.CompilerParams(dimension_semantics=("arbitrary", "arbitrary"),
                                             vmem_limit_bytes=int(min(vmem * 5 // 4 + (4 << 20), VMEM_BUDGET)),
                                             disable_bounds_checks=True),
        name="final",
    )(slots_t, slots_t, x, gates, gt, g_final.reshape(1, d), y_sorted)


def _route(top_e, n_exp):
    n_tok = top_e.shape[0]
    n_asg = n_tok * TOP_K
    flat_e = top_e.reshape(n_asg)
    onehot = (flat_e[:, None] == jnp.arange(n_exp, dtype=I32)[None, :]).astype(I32)
    csum = jnp.cumsum(onehot, axis=0)
    rank = jnp.sum(csum * onehot, axis=1) - 1
    counts = csum[-1]
    padded = (counts + MOE_BLOCK - 1) // MOE_BLOCK * MOE_BLOCK
    pad_ends = jnp.cumsum(padded)
    pad_starts = pad_ends - padded
    slot = jnp.sum(jnp.where(onehot > 0, pad_starts[None, :], 0), axis=1) + rank
    n_blocks = -(-(n_asg + n_exp * (MOE_BLOCK - 1)) // MOE_BLOCK)
    n_slots = n_blocks * MOE_BLOCK
    slot_tok = jnp.zeros((n_slots,), I32).at[slot].set(jnp.arange(n_asg, dtype=I32) // TOP_K, unique_indices=True,
                                                       mode='promise_in_bounds')
    block_start = jnp.arange(n_blocks, dtype=I32) * MOE_BLOCK
    block_e = jnp.minimum(jnp.sum((pad_ends[None, :] <= block_start[:, None]).astype(I32), axis=1), n_exp - 1)
    n_used = (pad_ends[-1] // MOE_BLOCK).astype(I32).reshape(1)
    return slot.reshape(n_tok, TOP_K), slot_tok, block_e, n_used


def _rope_tables(seq):
    t = np.arange(seq)
    quarter = HEAD_DIM // 4
    inv = jnp.asarray(ROPE_BASE, F32) ** (-jnp.arange(quarter, dtype=F32) / quarter)
    ang_r = jnp.asarray(t // GRID_W, F32)[:, None] * inv[None, :]
    ang_c = jnp.asarray(t % GRID_W, F32)[:, None] * inv[None, :]
    cos = jnp.concatenate([jnp.cos(ang_r)] * 2 + [jnp.cos(ang_c)] * 2, axis=-1)
    sin = jnp.concatenate([-jnp.sin(ang_r), jnp.sin(ang_r), -jnp.sin(ang_c), jnp.sin(ang_c)], axis=-1)
    reps = LANES // HEAD_DIM
    return jnp.tile(cos, (1, reps)), jnp.tile(sin, (1, reps))


def kernel(x, c, ctx, c_ctx, w_ada, b_ada, g_mix, w_in, na_rpb, sc_conv, wa_sinks, w_na_out, w_sc_out, w_wa_out,
           w_out, g_ffn, w_router, b_router, w_exp_in, b_exp_in, w_exp_out, b_exp_out, g_final):
    b, s, d = x.shape
    l = ctx.shape[1]
    depth = w_ada.shape[0]
    n_exp = w_router.shape[-1]
    rows = s // GRID_W
    off_kv = OFF_GATES + N_BRANCHES * d
    n_cols = off_kv + N_KV_COLS
    assert w_in.shape[-1] == n_cols and s % NA_TQ == 0

    cvec = jnp.zeros((8, d), F32).at[:b].set(c).at[b].set(c_ctx)
    ada = _ada_call(cvec, w_ada, b_ada).reshape(depth, 8, N_ADA, d)
    rope = _rope_tables(s)
    q_scale = HEAD_DIM ** -0.5 * LOG2E
    sinks2 = wa_sinks * LOG2E

    tm_lat = _pick(s, (1024, 512, 256))
    tm_lat_moe = _pick(s, (512, 256))
    tn_proj = _pick(n_cols, (1280, 768, 512, 256))
    tn_kv = _pick(np.gcd(N_KV_COLS, off_kv), (640, 256, 128))
    tm_merge = _pick(s, (512, 256))
    tn_merge = _pick(d, (512, 256))
    tm_comb = _pick(s, (256,))
    rope_cols = ((OFF_WA_Q, WA_Q_DIM), (off_kv + KV_WA_K, WA_KV_DIM))

    w_perm = jnp.concatenate(
        [w_in[..., REF_OFF_SC:REF_OFF_GATES], w_in[..., REF_OFF_WA_Q:REF_OFF_SC] * q_scale,
         w_in[..., REF_OFF_NA_Q:REF_OFF_WA_Q] * q_scale, w_in[..., REF_OFF_GATES:], w_in[..., :REF_OFF_NA_Q]],
        axis=-1).astype(BF16)
    wr_hi = w_router.astype(BF16)
    wr_lo = (w_router - wr_hi.astype(F32)).astype(BF16)
    w_r2 = jnp.concatenate([wr_hi, wr_lo], axis=-1)
    merge_w = (sc_conv, w_na_out.astype(BF16), w_sc_out.astype(BF16), w_wa_out.astype(BF16), w_out.astype(BF16))
    sub = d // 2 // LANES

    x_lat, x_ctx = x, ctx
    moe_lat = moe_ctx = None
    for layer in range(depth):
        last = layer == depth - 1
        mod_lat = [ada[layer, :b, k][:, None, :] for k in range(N_ADA)]
        mod_ctx = [jnp.broadcast_to(ada[layer, b, k][None, None, :], (b, 1, d)) for k in range(N_ADA)]

        p_lat = _proj_call(x_lat, g_mix, mod_lat[0], mod_lat[1], w_perm, layer, 0, n_cols, rope,
                           tm_lat if moe_lat is None else tm_lat_moe, tn_proj, rope_cols, moe_lat)
        if last:
            p_ctx = _proj_call(x_ctx, g_mix, mod_ctx[0], mod_ctx[1], w_perm, layer, off_kv, N_KV_COLS, None, l,
                               tn_kv, (), moe_ctx)
            ctx_kv = 0
        else:
            p_ctx = _proj_call(x_ctx, g_mix, mod_ctx[0], mod_ctx[1], w_perm, layer, 0, n_cols, None, l, tn_proj, (),
                               moe_ctx)
            ctx_kv = off_kv
        if moe_lat is not None:
            p_lat, x_lat = p_lat
            p_ctx, x_ctx = p_ctx

        bias = _na_bias_tables(na_rpb[layer] * LOG2E, rows)
        o_na = _na_call(p_lat, p_ctx, bias, off_kv, ctx_kv)
        o_wa = _wa_call(p_lat, p_ctx, sinks2[layer], off_kv, ctx_kv)

        x_lat, f_lat, te_lat, tg_lat = _merge_call(
            x_lat, o_na, o_wa, p_lat, layer, *merge_w, mod_lat[2], g_ffn, mod_lat[3], mod_lat[4],
            w_r2, b_router, tm_merge, tn_merge)
        f_all = f_lat.reshape(b * s * sub, LANES)
        te_all = te_lat.reshape(b * s, LANES)[:, :TOP_K]
        if not last:
            o_na_c, o_wa_c = _ctx_attn_call(p_ctx, sinks2[layer], off_kv)
            x_ctx, f_ctx, te_ctx, tg_ctx = _merge_call(
                x_ctx, o_na_c, o_wa_c, p_ctx, layer, *merge_w, mod_ctx[2], g_ffn, mod_ctx[3], mod_ctx[4],
                w_r2, b_router, l, tn_merge)
            f_all = jnp.concatenate([f_all, f_ctx.reshape(b * l * sub, LANES)], axis=0)
            te_all = jnp.concatenate([te_all, te_ctx.reshape(b * l, LANES)[:, :TOP_K]], axis=0)

        slots, slot_tok, block_e, n_used = _route(te_all, n_exp)
        y_sorted = _moe_call(layer, block_e, n_used, slot_tok, f_all, w_exp_in, b_exp_in, w_exp_out, b_exp_out)
        moe_lat = (slots[:b * s].reshape(b, s, TOP_K), tg_lat, mod_lat[5], y_sorted)
        if not last:
            moe_ctx = (slots[b * s:].reshape(b, l, TOP_K), tg_ctx, mod_ctx[5], y_sorted)
    return _final_call(x_lat, *moe_lat[:3], g_final, moe_lat[3], tm_comb)
```

```python
import functools

import numpy as np
import jax
import jax.numpy as jnp
from jax import lax
from jax.experimental import pallas as pl
from jax.experimental.pallas import tpu as pltpu

F32 = jnp.float32
BF16 = jnp.bfloat16
U32 = jnp.uint32
I32 = jnp.int32

GRID_W = 64
HEAD_DIM = 64
NA_HEADS = 8
NA_WIN_ROWS = 8
NA_WIN_COLS = 16
SC_WIDTH = 1024
SC_CONV_WIDTH = 3
WA_Q_HEADS = 8
WA_KV_HEADS = 2
WA_GROUP = WA_Q_HEADS // WA_KV_HEADS
WA_WINDOW = 128
ROPE_BASE = 10000.0
N_BRANCHES = 3
TOP_K = 4
SWIGLU_LIMIT = 7.0
SWIGLU_ALPHA = 1.702
N_ADA = 6
RMS_EPS = 1e-6
NEG_INF = -1e30

NA_DIM = NA_HEADS * HEAD_DIM
WA_Q_DIM = WA_Q_HEADS * HEAD_DIM
WA_KV_DIM = WA_KV_HEADS * HEAD_DIM

LANES = 128
VMEM_BUDGET = 56 * 1024 * 1024

REF_OFF_NA_Q = 2 * NA_DIM + 2 * WA_KV_DIM
REF_OFF_WA_Q = REF_OFF_NA_Q + NA_DIM
REF_OFF_SC = REF_OFF_WA_Q + WA_Q_DIM
REF_OFF_GATES = REF_OFF_SC + 3 * SC_WIDTH

OFF_SC_B = 0
OFF_SC_C = OFF_SC_B + SC_WIDTH
OFF_SC_H = OFF_SC_C + SC_WIDTH
OFF_WA_Q = OFF_SC_H + SC_WIDTH
OFF_NA_Q = OFF_WA_Q + WA_Q_DIM
OFF_GATES = OFF_NA_Q + NA_DIM
KV_NA_K = 0
KV_NA_V = KV_NA_K + NA_DIM
KV_WA_K = KV_NA_V + NA_DIM
KV_WA_V = KV_WA_K + WA_KV_DIM
N_KV_COLS = KV_WA_V + WA_KV_DIM

NA_QROWS = 4
NA_SLAB_ROWS = NA_QROWS + NA_WIN_ROWS - 1
NA_TQ = NA_QROWS * GRID_W
NA_TK = NA_SLAB_ROWS * GRID_W
NA_STEP_HEADS = 4
WA_TQ = 256
WA_TK = WA_TQ + 2 * WA_WINDOW

MOE_BLOCK = 512
MOE_BUFS = 3
DMA_QUEUES = 2
LOG2E = 1.4426950408889634


def _params(semantics, vmem_bytes):
    limit = int(min(max(vmem_bytes * 5 // 4 + (4 << 20), 32 << 20), VMEM_BUDGET))
    return pltpu.CompilerParams(dimension_semantics=semantics, vmem_limit_bytes=limit)


def _pick(n, candidates):
    for c in candidates:
        if n % c == 0:
            return c
    return n


def _dot(a, b):
    return jnp.dot(a, b, preferred_element_type=F32)


def _dot_nt(a, b):
    return lax.dot_general(a, b, (((1,), (1,)), ((), ())), preferred_element_type=F32)


def _rms_mod(x, g, shift, scale):
    y = x * lax.rsqrt(jnp.mean(x * x, axis=-1, keepdims=True) + RMS_EPS)
    return (y * g) * (1.0 + scale) + shift


def _ada_kernel(c_ref, w_ref, b_ref, o_ref):
    c = c_ref[...]
    s = c * jax.nn.sigmoid(c)
    o_ref[0] = jnp.dot(s, w_ref[0], preferred_element_type=F32,
                       precision=lax.Precision.HIGHEST) + b_ref[0]


def _ada_call(cvec, w_ada, b_ada):
    depth, d, n = w_ada.shape
    tn = _pick(n, (1024, 768, 512, 256, 128))
    return pl.pallas_call(
        _ada_kernel,
        grid=(depth, n // tn),
        in_specs=[pl.BlockSpec((8, d), lambda l, j: (0, 0)),
                  pl.BlockSpec((1, d, tn), lambda l, j: (l, 0, j)),
                  pl.BlockSpec((1, 1, tn), lambda l, j: (l, 0, j))],
        out_specs=pl.BlockSpec((1, 8, tn), lambda l, j: (l, 0, j)),
        out_shape=jax.ShapeDtypeStruct((depth, 8, n), F32),
        compiler_params=_params(("arbitrary", "arbitrary"), 2 * d * tn * 4),
        name="ada",
    )(cvec, w_ada, b_ada.reshape(depth, 1, n))


def _rope128(x, cos, sin):
    lane = lax.broadcasted_iota(I32, x.shape, 1)
    fwd = pltpu.roll(x, LANES - HEAD_DIM // 4, 1)
    bwd = pltpu.roll(x, HEAD_DIM // 4, 1)
    partner = jnp.where((lane & (HEAD_DIM // 4)) == 0, fwd, bwd)
    return x * cos + partner * sin


def _gather_rows(y_hbm, ybuf, sem, slot_ref, r, buf, sub):
    src = pl.multiple_of(slot_ref[0, 0, r] * sub, sub)
    dst = pl.multiple_of(r * sub, sub)
    return pltpu.make_async_copy(y_hbm.at[pl.ds(src, sub), :], ybuf.at[buf, pl.ds(dst, sub), :], sem.at[buf])


def _wait_rows(y_hbm, ybuf, sem, buf):
    pltpu.make_async_copy(y_hbm.at[pl.ds(0, ybuf.shape[1]), :], ybuf.at[buf], sem.at[buf]).wait()


def _combine_rows(ybuf, buf, gates, tm, sub):
    acc = None
    for k in range(TOP_K):
        yk = _load_token_tiles(ybuf.at[buf, pl.ds(k * tm * sub, tm * sub), :], tm, F32) * gates[:, k:k + 1]
        acc = yk if acc is None else acc + yk
    return acc


def _proj_kernel(*refs, rope_tiles, combine, per):
    refs = list(refs)
    if combine:
        slot_cur_ref, slot_nxt_ref, gate_ref, gtf_ref, y_hbm = refs[:5]
        refs = refs[5:]
    x_ref, g_ref, sh_ref, sc_ref = refs[:4]
    refs = refs[4:]
    if rope_tiles:
        cos_ref, sin_ref = refs[:2]
        refs = refs[2:]
    w_ref, o_ref = refs[:2]
    refs = refs[2:]
    if combine:
        xo_ref, h_ref, ybuf, sem = refs
    else:
        (h_ref,) = refs
    j = pl.program_id(2)
    tm, d = x_ref.shape[1], x_ref.shape[2]
    sub = d // 2 // LANES
    if combine:
        t = pl.program_id(0) * pl.num_programs(1) + pl.program_id(1)
        cur = t % 2
        last_step = jnp.logical_and(t == pl.num_programs(0) * pl.num_programs(1) - 1, j == pl.num_programs(2) - 1)

    @pl.when(j == 0)
    def _():
        if combine:
            @pl.when(t == 0)
            def _():
                def body(r, carry):
                    _gather_rows(y_hbm, ybuf, sem, slot_cur_ref, r, 0, sub).start()
                    return carry
                lax.fori_loop(0, ybuf.shape[1] // sub, body, 0)

            _wait_rows(y_hbm, ybuf, sem, cur)
            x = x_ref[0] + gtf_ref[0] * _combine_rows(ybuf, cur, gate_ref[0], tm, sub)
            xo_ref[0] = x
        else:
            x = x_ref[0]
        h_ref[...] = _rms_mod(x, g_ref[0], sh_ref[0], sc_ref[0]).astype(BF16)

    if combine:
        for q in range(per):
            _gather_rows(y_hbm, ybuf, sem, slot_nxt_ref, j * per + q, 1 - cur, sub).start(priority=q % DMA_QUEUES)
    acc = _dot(h_ref[...], w_ref[0])
    o_ref[0] = acc.astype(BF16)
    for jt, lo, width in rope_tiles:
        @pl.when(j == jt)
        def _(lo=lo, width=width):
            cos, sin = cos_ref[...], sin_ref[...]
            for c0 in range(lo, lo + width, LANES):
                o_ref[0, :, c0:c0 + LANES] = _rope128(acc[:, c0:c0 + LANES], cos, sin).astype(BF16)
    if combine:
        @pl.when(last_step)
        def _():
            _wait_rows(y_hbm, ybuf, sem, 1 - cur)


def _tile_slots(slots, tm, n_rows):
    b, n, _ = slots.shape
    st = slots.reshape(b, n // tm, tm, TOP_K).transpose(0, 1, 3, 2).reshape(b * (n // tm), 1, TOP_K * tm)
    return jnp.pad(st, ((0, 0), (0, 0), (0, n_rows - TOP_K * tm)))


def _proj_call(x, g, shift, scale, w_all, layer, col0, nc, rope, tm, tn, rope_cols, moe=None):
    b, n, d = x.shape
    assert col0 % tn == 0 and nc % tn == 0 and n % tm == 0
    jb = col0 // tn
    ni, nj = n // tm, nc // tn
    sub = d // 2 // LANES
    rope_tiles = []
    if rope is not None:
        for off, width in rope_cols:
            assert off // tn == (off + width - 1) // tn and off % LANES == 0 and width % LANES == 0
            rope_tiles.append((off // tn, off % tn, width))
    in_specs, args, per = [], [], 0
    vmem = 2 * tm * d * 4 + tm * d * 2 + 2 * d * tn * 2 + 2 * tm * tn * 2 + tm * tn * 4 + 4 * tm * LANES * 4
    if moe is not None:
        slots, gates, gt, y_sorted = moe
        per = -(-TOP_K * tm // nj)
        n_rows = per * nj
        slots_t = _tile_slots(slots, tm, n_rows)
        n_t = b * ni

        def smem(index):
            return pl.BlockSpec((1, 1, n_rows), index, memory_space=pltpu.SMEM)

        in_specs += [smem(lambda bb, i, j: (bb * ni + i, 0, 0)),
                     smem(lambda bb, i, j: (jnp.minimum(bb * ni + i + 1, n_t - 1), 0, 0)),
                     pl.BlockSpec((1, tm, LANES), lambda bb, i, j: (bb, i, 0)),
                     pl.BlockSpec((1, 1, d), lambda bb, i, j: (bb, 0, 0)),
                     pl.BlockSpec(memory_space=pl.ANY)]
        args += [slots_t, slots_t, gates, gt, y_sorted]
        vmem += 2 * tm * d * 4 + 2 * n_rows * sub * LANES * 4 + 3 * tm * d * 4
    in_specs += [pl.BlockSpec((1, tm, d), lambda bb, i, j: (bb, i, 0)),
                 pl.BlockSpec((1, 1, d), lambda bb, i, j: (layer, 0, 0)),
                 pl.BlockSpec((1, 1, d), lambda bb, i, j: (bb, 0, 0)),
                 pl.BlockSpec((1, 1, d), lambda bb, i, j: (bb, 0, 0))]
    args += [x, g.reshape(g.shape[0], 1, d), shift, scale]
    if rope is not None:
        in_specs += [pl.BlockSpec((tm, LANES), lambda bb, i, j: (i, 0))] * 2
        args += list(rope)
    in_specs.append(pl.BlockSpec((1, d, tn), lambda bb, i, j: (layer, 0, jb + j)))
    args.append(w_all)
    out_specs = [pl.BlockSpec((1, tm, tn), lambda bb, i, j: (bb, i, j))]
    out_shape = [jax.ShapeDtypeStruct((b, n, nc), BF16)]
    scratch = [pltpu.VMEM((tm, d), BF16)]
    if moe is not None:
        out_specs.append(pl.BlockSpec((1, tm, d), lambda bb, i, j: (bb, i, 0)))
        out_shape.append(jax.ShapeDtypeStruct((b, n, d), F32))
        scratch += [pltpu.VMEM((2, n_rows * sub, LANES), U32), pltpu.SemaphoreType.DMA((2,))]
    limit = int(min(max(vmem * 5 // 4 + (4 << 20), 32 << 20), VMEM_BUDGET))
    out = pl.pallas_call(
        functools.partial(_proj_kernel, rope_tiles=tuple(rope_tiles), combine=moe is not None, per=per),
        grid=(b, ni, nj),
        in_specs=in_specs,
        out_specs=out_specs,
        out_shape=out_shape,
        scratch_shapes=scratch,
        compiler_params=pltpu.CompilerParams(dimension_semantics=("arbitrary", "arbitrary", "arbitrary"),
                                             vmem_limit_bytes=limit, disable_bounds_checks=moe is not None),
        name="proj",
    )(*args)
    return out if moe is not None else out[0]


def _softmax_av(parts, extra=None):
    m = functools.reduce(jnp.maximum, [jnp.max(s, axis=-1, keepdims=True) for s, _ in parts])
    if extra is not None:
        m = jnp.maximum(m, extra)
    l = 0.0 if extra is None else jnp.exp2(extra - m)
    o = None
    for s, v in parts:
        p = jnp.exp2(s - m)
        l = l + jnp.sum(p, axis=-1, keepdims=True)
        pv = _dot(p.astype(BF16), v)
        o = pv if o is None else o + pv
    return o / l


def _na_kernel(q_ref, k_ref, v_ref, kc_ref, vc_ref, bias_ref, o_ref, *, rows):
    i = pl.program_id(2)
    r_start = jnp.clip(NA_QROWS * i - NA_WIN_ROWS // 2, 0, rows - NA_SLAB_ROWS)
    start = pl.multiple_of(r_start * GRID_W, GRID_W)
    kslab = k_ref[0, pl.ds(start, NA_TK), :]
    vslab = v_ref[0, pl.ds(start, NA_TK), :]
    q, kc, vc = q_ref[0], kc_ref[0], vc_ref[0]
    outs = []
    for hh in range(NA_STEP_HEADS):
        sl = slice(hh * HEAD_DIM, (hh + 1) * HEAD_DIM)
        qh = q[:, sl]
        s_nb = _dot_nt(qh, kslab[:, sl]) + bias_ref[0, hh]
        s_ctx = _dot_nt(qh, kc[:, sl])
        outs.append(_softmax_av([(s_nb, vslab[:, sl]), (s_ctx, vc[:, sl])]))
    o_ref[0] = jnp.concatenate(outs, axis=1).astype(BF16)


def _na_bias_tables(rpb, rows):
    nblk = rows // NA_QROWS
    assert nblk >= 3 and rows >= NA_SLAB_ROWS
    n_heads, n_dr, n_dc = rpb.shape
    half = NA_WIN_COLS - 1
    vec = jnp.zeros((n_heads, n_dr, LANES), F32)
    vec = vec.at[..., :n_dc - half].set(rpb[..., half:]).at[..., LANES - half:].set(rpb[..., :half])
    toep = jnp.tile(vec, (1, 1, GRID_W))[..., :GRID_W * (LANES - 1)]
    toep = toep.reshape(n_heads, n_dr, GRID_W, LANES - 1)[..., :GRID_W]
    col = np.arange(GRID_W)
    c0 = np.clip(col - NA_WIN_COLS // 2, 0, GRID_W - NA_WIN_COLS)
    col_ok = (col[None, :] >= c0[:, None]) & (col[None, :] < c0[:, None] + NA_WIN_COLS)
    toep = jnp.where(col_ok[None, None], toep, NEG_INF)
    masked = jnp.full((n_heads, GRID_W, GRID_W), NEG_INF, F32)
    cases = []
    for i in (0, 1, nblk - 1):
        rs = int(np.clip(NA_QROWS * i - NA_WIN_ROWS // 2, 0, rows - NA_SLAB_ROWS))
        q_rows = []
        for rl in range(NA_QROWS):
            r = NA_QROWS * i + rl
            r0 = int(np.clip(r - NA_WIN_ROWS // 2, 0, rows - NA_WIN_ROWS))
            tiles = []
            for kl in range(NA_SLAB_ROWS):
                kr = rs + kl
                tiles.append(toep[:, kr - r + NA_WIN_ROWS - 1] if r0 <= kr < r0 + NA_WIN_ROWS else masked)
            q_rows.append(jnp.concatenate(tiles, axis=2))
        cases.append(jnp.concatenate(q_rows, axis=1))
    return jnp.stack(cases)


def _na_call(p_lat, p_ctx, bias, kv_off_lat, kv_off_ctx):
    b, s, _ = p_lat.shape
    l = p_ctx.shape[1]
    rows = s // GRID_W
    nblk = rows // NA_QROWS
    w = NA_STEP_HEADS * HEAD_DIM
    hp = NA_DIM // w
    assert OFF_NA_Q % w == 0 and kv_off_lat % w == 0 and kv_off_ctx % w == 0
    qb, kb, vb = OFF_NA_Q // w, (kv_off_lat + KV_NA_K) // w, (kv_off_lat + KV_NA_V) // w
    kcb, vcb = (kv_off_ctx + KV_NA_K) // w, (kv_off_ctx + KV_NA_V) // w

    def case(i):
        return jnp.where(i == 0, 0, jnp.where(i == nblk - 1, 2, 1))

    vmem = 4 * s * w * 2 + 2 * NA_STEP_HEADS * NA_TQ * NA_TK * 4 + 8 * NA_TQ * (NA_TK + l) * 4
    return pl.pallas_call(
        functools.partial(_na_kernel, rows=rows),
        grid=(b, hp, nblk),
        in_specs=[pl.BlockSpec((1, NA_TQ, w), lambda bb, h, i: (bb, i, qb + h)),
                  pl.BlockSpec((1, s, w), lambda bb, h, i: (bb, 0, kb + h)),
                  pl.BlockSpec((1, s, w), lambda bb, h, i: (bb, 0, vb + h)),
                  pl.BlockSpec((1, l, w), lambda bb, h, i: (bb, 0, kcb + h)),
                  pl.BlockSpec((1, l, w), lambda bb, h, i: (bb, 0, vcb + h)),
                  pl.BlockSpec((1, NA_STEP_HEADS, NA_TQ, NA_TK), lambda bb, h, i: (case(i), h, 0, 0))],
        out_specs=pl.BlockSpec((1, NA_TQ, w), lambda bb, h, i: (bb, i, h)),
        out_shape=jax.ShapeDtypeStruct((b, s, NA_DIM), BF16),
        compiler_params=_params(("arbitrary", "arbitrary", "arbitrary"), vmem),
        name="natten",
    )(p_lat, p_lat, p_lat, p_ctx, p_ctx, bias)


def _wa_kernel(sink_ref, q_ref, k_ref, v_ref, kc_ref, vc_ref, o_ref, *, seq):
    i = pl.program_id(1)
    start = pl.multiple_of(jnp.clip(WA_TQ * i - WA_WINDOW, 0, seq - WA_TK), WA_WINDOW)
    kslab = k_ref[0, pl.ds(start, WA_TK), :]
    vslab = v_ref[0, pl.ds(start, WA_TK), :]
    qpos = WA_TQ * i + lax.broadcasted_iota(I32, (WA_TQ, WA_TK), 0)
    kpos = start + lax.broadcasted_iota(I32, (WA_TQ, WA_TK), 1)
    band = jnp.where(jnp.abs(kpos - qpos) <= WA_WINDOW, 0.0, NEG_INF).astype(F32)
    q, kc, vc = q_ref[0], kc_ref[0], vc_ref[0]
    outs = []
    for hq in range(WA_Q_HEADS):
        hk = hq // WA_GROUP
        sk = slice(hk * HEAD_DIM, (hk + 1) * HEAD_DIM)
        qh = q[:, hq * HEAD_DIM:(hq + 1) * HEAD_DIM]
        s_loc = _dot_nt(qh, kslab[:, sk]) + band
        s_ctx = _dot_nt(qh, kc[:, sk])
        outs.append(_softmax_av([(s_loc, vslab[:, sk]), (s_ctx, vc[:, sk])], extra=sink_ref[hq]))
    o_ref[0] = jnp.concatenate(outs, axis=1).astype(BF16)


def _wa_call(p_lat, p_ctx, sinks, kv_off_lat, kv_off_ctx):
    b, s, _ = p_lat.shape
    l = p_ctx.shape[1]
    assert s % WA_TQ == 0 and s >= WA_TK
    qb = OFF_WA_Q // WA_Q_DIM
    kb, vb = (kv_off_lat + KV_WA_K) // LANES, (kv_off_lat + KV_WA_V) // LANES
    kcb, vcb = (kv_off_ctx + KV_WA_K) // LANES, (kv_off_ctx + KV_WA_V) // LANES
    vmem = 4 * s * LANES * 2 + 4 * WA_TQ * WA_Q_DIM * 2 + 10 * WA_TQ * (WA_TK + l) * 4
    return pl.pallas_call(
        functools.partial(_wa_kernel, seq=s),
        grid=(b, s // WA_TQ),
        in_specs=[pl.BlockSpec(memory_space=pltpu.SMEM),
                  pl.BlockSpec((1, WA_TQ, WA_Q_DIM), lambda bb, i: (bb, i, qb)),
                  pl.BlockSpec((1, s, LANES), lambda bb, i: (bb, 0, kb)),
                  pl.BlockSpec((1, s, LANES), lambda bb, i: (bb, 0, vb)),
                  pl.BlockSpec((1, l, LANES), lambda bb, i: (bb, 0, kcb)),
                  pl.BlockSpec((1, l, LANES), lambda bb, i: (bb, 0, vcb))],
        out_specs=pl.BlockSpec((1, WA_TQ, WA_Q_DIM), lambda bb, i: (bb, i, 0)),
        out_shape=jax.ShapeDtypeStruct((b, s, WA_Q_DIM), BF16),
        compiler_params=_params(("arbitrary", "arbitrary"), vmem),
        name="winattn",
    )(sinks, p_lat, p_lat, p_lat, p_ctx, p_ctx)


def _ctx_attn_kernel(sink_ref, naq_ref, nak_ref, nav_ref, waq_ref, wak_ref, wav_ref, ona_ref, owa_ref):
    q, k, v = naq_ref[0], nak_ref[0], nav_ref[0]
    outs = []
    for h in range(NA_HEADS):
        sl = slice(h * HEAD_DIM, (h + 1) * HEAD_DIM)
        outs.append(_softmax_av([(_dot_nt(q[:, sl], k[:, sl]), v[:, sl])]))
    ona_ref[0] = jnp.concatenate(outs, axis=1).astype(BF16)
    q, k, v = waq_ref[0], wak_ref[0], wav_ref[0]
    outs = []
    for hq in range(WA_Q_HEADS):
        sk = slice((hq // WA_GROUP) * HEAD_DIM, (hq // WA_GROUP + 1) * HEAD_DIM)
        qh = q[:, hq * HEAD_DIM:(hq + 1) * HEAD_DIM]
        outs.append(_softmax_av([(_dot_nt(qh, k[:, sk]), v[:, sk])], extra=sink_ref[hq]))
    owa_ref[0] = jnp.concatenate(outs, axis=1).astype(BF16)


def _ctx_attn_call(p_ctx, sinks, kv_off):
    b, l, _ = p_ctx.shape

    def spec(width, off):
        return pl.BlockSpec((1, l, width), lambda bb: (bb, 0, off // width))

    return pl.pallas_call(
        _ctx_attn_kernel,
        grid=(b,),
        in_specs=[pl.BlockSpec(memory_space=pltpu.SMEM),
                  spec(NA_DIM, OFF_NA_Q), spec(NA_DIM, kv_off + KV_NA_K), spec(NA_DIM, kv_off + KV_NA_V),
                  spec(WA_Q_DIM, OFF_WA_Q), spec(WA_KV_DIM, kv_off + KV_WA_K), spec(WA_KV_DIM, kv_off + KV_WA_V)],
        out_specs=[pl.BlockSpec((1, l, NA_DIM), lambda bb: (bb, 0, 0)),
                   pl.BlockSpec((1, l, WA_Q_DIM), lambda bb: (bb, 0, 0))],
        out_shape=[jax.ShapeDtypeStruct((b, l, NA_DIM), BF16), jax.ShapeDtypeStruct((b, l, WA_Q_DIM), BF16)],
        compiler_params=_params(("arbitrary",), 16 * l * NA_DIM * 4),
        name="ctxattn",
    )(sinks, p_ctx, p_ctx, p_ctx, p_ctx, p_ctx, p_ctx)


def _pack_bf16_pairs(x):
    n = x.shape[1] // 2
    lo = pltpu.bitcast(x[:, :n].astype(BF16).astype(F32), U32)
    hi = pltpu.bitcast(x[:, n:].astype(BF16).astype(F32), U32)
    return (hi & jnp.uint32(0xFFFF0000)) | (lo >> 16)


def _store_token_tiles(ref, packed):
    m, n = packed.shape
    sub = n // LANES
    for s in range(sub):
        ref[pl.ds(s, m, stride=sub), :] = packed[:, s * LANES:(s + 1) * LANES]


def _load_token_tiles(ref, m, dtype):
    sub = ref.shape[0] // m
    chunks = [ref[pl.ds(s, m, stride=sub), :] for s in range(sub)]
    lo = [pltpu.bitcast(p << 16, F32).astype(dtype) for p in chunks]
    hi = [pltpu.bitcast(p & jnp.uint32(0xFFFF0000), F32).astype(dtype) for p in chunks]
    return jnp.concatenate(lo + hi, axis=1)


def _top_k_lanes(logits, n_exp):
    m_rows = logits.shape[0]
    col = lax.broadcasted_iota(I32, (m_rows, n_exp), 1).astype(F32)
    lane = lax.broadcasted_iota(I32, (m_rows, LANES), 1)
    work = logits
    vals = jnp.zeros((m_rows, LANES), F32)
    idxs = jnp.zeros((m_rows, LANES), F32)
    top = None
    denom = 0.0
    for k in range(TOP_K):
        mx = jnp.max(work, axis=-1, keepdims=True)
        ix = jnp.min(jnp.where(work == mx, col, float(n_exp)), axis=-1, keepdims=True)
        work = jnp.where(col == ix, -jnp.inf, work)
        top = mx if top is None else top
        e = jnp.exp(mx - top)
        denom = denom + e
        vals = jnp.where(lane == k, e, vals)
        idxs = jnp.where(lane == k, ix, idxs)
    return vals / denom, idxs.astype(I32)


def _merge_kernel(*refs, has_halo, tn):
    (ona_ref, owa_ref, scb_ref, scc_ref, sch_ref) = refs[:5]
    refs = refs[5:]
    if has_halo:
        cprev_ref, hprev_ref, cnext_ref, hnext_ref = refs[:4]
        refs = refs[4:]
    gna_ref, gsc_ref, gwa_ref, conv_ref, wna_ref, wsc_ref, wwa_ref, m_ref = refs
    i = pl.program_id(1)
    tm, d = m_ref.shape[1], m_ref.shape[2]

    u = scc_ref[0].astype(F32) * sch_ref[0].astype(F32)
    zero = jnp.zeros((1, SC_WIDTH), F32)
    if has_halo:
        u_prev = jnp.where(i == 0, zero, cprev_ref[0, 7:8, :].astype(F32) * hprev_ref[0, 7:8, :].astype(F32))
        u_next = jnp.where(i == pl.num_programs(1) - 1, zero,
                           cnext_ref[0, 0:1, :].astype(F32) * hnext_ref[0, 0:1, :].astype(F32))
    else:
        u_prev = u_next = zero
    row = lax.broadcasted_iota(I32, u.shape, 0)
    u_m1 = jnp.where(row == 0, u_prev, pltpu.roll(u, 1, 0))
    u_p1 = jnp.where(row == tm - 1, u_next, pltpu.roll(u, tm - 1, 0))
    y = u_m1 * conv_ref[0, 0:1, :] + u * conv_ref[0, 1:2, :] + u_p1 * conv_ref[0, 2:3, :]
    o_sc = (scb_ref[0].astype(F32) * y).astype(BF16)
    o_na, o_wa = ona_ref[0], owa_ref[0]

    def gate(ref, cs):
        return jax.nn.sigmoid(ref[0, :, cs].astype(F32))

    for n in range(d // tn):
        cs = slice(n * tn, (n + 1) * tn)
        merged = (gate(gna_ref, cs) * _dot(o_na, wna_ref[0, :, cs])
                  + gate(gsc_ref, cs) * _dot(o_sc, wsc_ref[0, :, cs])
                  + gate(gwa_ref, cs) * _dot(o_wa, wwa_ref[0, :, cs]))
        m_ref[0, :, cs] = merged.astype(BF16)


def _outproj_kernel(x_ref, m_ref, wout_ref, gt_ref, gffn_ref, shf_ref, scf_ref, wr_ref, br_ref,
                    xo_ref, f_ref, te_ref, tg_ref, *, n_exp):
    xn = x_ref[0] + gt_ref[0] * _dot(m_ref[0], wout_ref[0])
    xo_ref[0] = xn
    f = _rms_mod(xn, gffn_ref[0], shf_ref[0], scf_ref[0])
    _store_token_tiles(f_ref.at[0], _pack_bf16_pairs(f))
    f_hi = f.astype(BF16)
    f_lo = (f - f_hi.astype(F32)).astype(BF16)
    wr = wr_ref[0]
    hi_both = _dot(f_hi, wr)
    logits = (hi_both[:, :n_exp] + _dot(f_lo, wr[:, :n_exp]) + hi_both[:, n_exp:]) + br_ref[0]
    gates, idx = _top_k_lanes(logits, n_exp)
    tg_ref[0] = gates
    te_ref[0] = idx


def _merge_call(x, o_na, o_wa, p, layer, conv_w, w_na, w_sc, w_wa, w_out, gt, g_ffn, sh_f, sc_f, w_r2, b_r, tm, tn):
    b, n_tok, d = x.shape
    depth = w_out.shape[0]
    n_exp = b_r.shape[-1]
    sub = d // 2 // LANES
    nt = n_tok // tm
    has_halo = nt > 1
    assert OFF_GATES % d == 0 and d % tn == 0 and n_tok % tm == 0 and tm % 8 == 0
    gate_b = OFF_GATES // d
    rb = tm // 8

    def rows(width, off):
        return pl.BlockSpec((1, tm, width), lambda bb, i: (bb, i, off // width))

    def resident(shape):
        return pl.BlockSpec((1,) + shape, lambda bb, i: (layer, 0, 0), pipeline_mode=pl.Buffered(1))

    in_specs = [pl.BlockSpec((1, tm, NA_DIM), lambda bb, i: (bb, i, 0)),
                pl.BlockSpec((1, tm, WA_Q_DIM), lambda bb, i: (bb, i, 0)),
                rows(SC_WIDTH, OFF_SC_B), rows(SC_WIDTH, OFF_SC_C), rows(SC_WIDTH, OFF_SC_H)]
    args = [o_na, o_wa, p, p, p]
    if has_halo:
        last8 = n_tok // 8 - 1
        for off in (OFF_SC_C, OFF_SC_H):
            in_specs.append(pl.BlockSpec((1, 8, SC_WIDTH),
                                         lambda bb, i, off=off: (bb, jnp.maximum(i * rb - 1, 0), off // SC_WIDTH)))
        for off in (OFF_SC_C, OFF_SC_H):
            in_specs.append(pl.BlockSpec((1, 8, SC_WIDTH),
                                         lambda bb, i, off=off: (bb, jnp.minimum((i + 1) * rb, last8), off // SC_WIDTH)))
        args += [p, p, p, p]
    for br in range(N_BRANCHES):
        in_specs.append(pl.BlockSpec((1, tm, d), lambda bb, i, br=br: (bb, i, gate_b + br)))
        args.append(p)
    in_specs += [resident((SC_CONV_WIDTH, SC_WIDTH)),
                 resident((NA_DIM, d)), resident((SC_WIDTH, d)), resident((WA_Q_DIM, d))]
    args += [conv_w, w_na, w_sc, w_wa]
    vmem = (2 * tm * (NA_DIM + WA_Q_DIM + 3 * SC_WIDTH + N_BRANCHES * d + d) * 2
            + (NA_DIM + SC_WIDTH + WA_Q_DIM) * d * 2 + 4 * tm * SC_WIDTH * 4 + 3 * tm * tn * 4)
    merged = pl.pallas_call(
        functools.partial(_merge_kernel, has_halo=has_halo, tn=tn),
        grid=(b, nt),
        in_specs=in_specs,
        out_specs=pl.BlockSpec((1, tm, d), lambda bb, i: (bb, i, 0)),
        out_shape=jax.ShapeDtypeStruct((b, n_tok, d), BF16),
        compiler_params=_params(("arbitrary", "arbitrary"), vmem),
        name="merge",
    )(*args)

    def per_batch():
        return pl.BlockSpec((1, 1, d), lambda bb, i: (bb, 0, 0))

    out_specs = [pl.BlockSpec((1, tm, d), lambda bb, i: (bb, i, 0)),
                 pl.BlockSpec((1, tm * sub, LANES), lambda bb, i: (bb, i, 0)),
                 pl.BlockSpec((1, tm, LANES), lambda bb, i: (bb, i, 0)),
                 pl.BlockSpec((1, tm, LANES), lambda bb, i: (bb, i, 0))]
    out_shape = [jax.ShapeDtypeStruct((b, n_tok, d), F32), jax.ShapeDtypeStruct((b, n_tok * sub, LANES), U32),
                 jax.ShapeDtypeStruct((b, n_tok, LANES), I32), jax.ShapeDtypeStruct((b, n_tok, LANES), F32)]
    vmem = (4 * tm * d * 4 + 2 * tm * d * 2 + 2 * tm * (d // 2) * 4 + 4 * tm * LANES * 4
            + d * d * 2 + d * 2 * n_exp * 4 + 4 * tm * d * 4)
    return pl.pallas_call(
        functools.partial(_outproj_kernel, n_exp=n_exp),
        grid=(b, nt),
        in_specs=[pl.BlockSpec((1, tm, d), lambda bb, i: (bb, i, 0)),
                  pl.BlockSpec((1, tm, d), lambda bb, i: (bb, i, 0)),
                  resident((d, d)), per_batch(), resident((1, d)), per_batch(), per_batch(),
                  resident((d, 2 * n_exp)), resident((1, n_exp))],
        out_specs=out_specs,
        out_shape=out_shape,
        compiler_params=_params(("arbitrary", "arbitrary"), vmem),
        name="outproj",
    )(x, merged, w_out, gt, g_ffn.reshape(depth, 1, d), sh_f, sc_f, w_r2, b_r.reshape(depth, 1, n_exp))


def _moe_kernel(be_ref, nused_ref, tok_cur_ref, tok_nxt_ref, tok_far_ref, f_hbm, w1_ref, b1_ref, w2_ref, b2_ref,
                y_ref, xbuf, w1b, w2b, gsem, *, d_exp, sub):
    i = pl.program_id(0)
    n_used = nused_ref[0]
    slot = i % MOE_BUFS
    nxt = (i + 1) % MOE_BUFS
    far = (i + 2) % MOE_BUFS
    rows = MOE_BLOCK * sub

    def gather(tok_ref, r, dst_slot):
        src = pl.multiple_of(tok_ref[0, 0, r] * sub, sub)
        return pltpu.make_async_copy(f_hbm.at[pl.ds(src, sub), :], xbuf.at[dst_slot, pl.ds(r * sub, sub), :],
                                     gsem.at[dst_slot])

    def wait_gather(s):
        pltpu.make_async_copy(f_hbm.at[pl.ds(0, rows), :], xbuf.at[s], gsem.at[s]).wait()

    @pl.when(i < n_used)
    def _():
        @pl.when(i == 0)
        def _():
            def body(r, carry):
                gather(tok_cur_ref, r, 0).start()
                gather(tok_nxt_ref, r, 1).start()
                return carry
            lax.fori_loop(0, MOE_BLOCK, body, 0)

        @pl.when(jnp.logical_or(i == 0, be_ref[i] != be_ref[jnp.maximum(i - 1, 0)]))
        def _():
            w1b[...] = w1_ref[0, 0].astype(BF16)
            w2b[...] = w2_ref[0, 0].astype(BF16)

        wait_gather(slot)
        x = _load_token_tiles(xbuf.at[slot], MOE_BLOCK, BF16)
        n_chunks = 2
        cw = d_exp // n_chunks
        for r in range(MOE_BLOCK):
            gather(tok_far_ref, r, far).start(priority=r % DMA_QUEUES)
        y = None
        for c in range(n_chunks):
            glu = _dot(x, w1b[:, c * cw:(c + 1) * cw]) + b1_ref[0, 0, :, c * cw:(c + 1) * cw]
            lin = _dot(x, w1b[:, d_exp + c * cw:d_exp + (c + 1) * cw]) + b1_ref[0, 0, :, d_exp + c * cw:d_exp + (c + 1) * cw]
            glu = jnp.minimum(glu, SWIGLU_LIMIT)
            lin = jnp.clip(lin, -SWIGLU_LIMIT, SWIGLU_LIMIT)
            act = glu * jax.nn.sigmoid(SWIGLU_ALPHA * glu) * (lin + 1.0)
            yc = _dot(act.astype(BF16), w2b[c * cw:(c + 1) * cw, :])
            y = yc if y is None else y + yc
        _store_token_tiles(y_ref, _pack_bf16_pairs(y + b2_ref[0, 0]))

        @pl.when(i == n_used - 1)
        def _():
            wait_gather(nxt)
            wait_gather(far)

    @pl.when(i >= n_used)
    def _():
        y_ref[...] = jnp.zeros_like(y_ref)


def _moe_call(layer, block_e, n_used, slot_tok, f_all, w1, b1, w2, b2):
    n_blocks = block_e.shape[0]
    depth, n_exp, d, two_de = w1.shape
    d_exp = two_de // 2
    sub = d // 2 // LANES
    assert d_exp % (2 * LANES) == 0
    rows = MOE_BLOCK * sub

    def smem(index):
        return pl.BlockSpec((1, 1, MOE_BLOCK), index, memory_space=pltpu.SMEM)

    grid_spec = pltpu.PrefetchScalarGridSpec(
        num_scalar_prefetch=2,
        grid=(n_blocks,),
        in_specs=[smem(lambda i, be, nu: (i, 0, 0)),
                  smem(lambda i, be, nu: (jnp.minimum(i + 1, n_blocks - 1), 0, 0)),
                  smem(lambda i, be, nu: (jnp.minimum(i + 2, n_blocks - 1), 0, 0)),
                  pl.BlockSpec(memory_space=pl.ANY),
                  pl.BlockSpec((1, 1, d, two_de), lambda i, be, nu: (layer, be[i], 0, 0)),
                  pl.BlockSpec((1, 1, 1, two_de), lambda i, be, nu: (layer, be[i], 0, 0)),
                  pl.BlockSpec((1, 1, d_exp, d), lambda i, be, nu: (layer, be[i], 0, 0)),
                  pl.BlockSpec((1, 1, 1, d), lambda i, be, nu: (layer, be[i], 0, 0))],
        out_specs=pl.BlockSpec((rows, LANES), lambda i, be, nu: (i, 0)),
        scratch_shapes=[pltpu.VMEM((MOE_BUFS, rows, LANES), U32),
                        pltpu.VMEM((d, two_de), BF16), pltpu.VMEM((d_exp, d), BF16),
                        pltpu.SemaphoreType.DMA((MOE_BUFS,))],
    )
    vmem = (2 * (d * two_de + d_exp * d) * 4 + (d * two_de + d_exp * d) * 2 + 4 * rows * LANES * 4
            + 6 * MOE_BLOCK * (d + two_de) * 4)
    tok3 = slot_tok.reshape(n_blocks, 1, MOE_BLOCK)
    return pl.pallas_call(
        functools.partial(_moe_kernel, d_exp=d_exp, sub=sub),
        grid_spec=grid_spec,
        out_shape=jax.ShapeDtypeStruct((n_blocks * rows, LANES), U32),
        compiler_params=pltpu.CompilerParams(dimension_semantics=("arbitrary",),
                                             vmem_limit_bytes=int(min(vmem + (4 << 20), VMEM_BUDGET)),
                                             disable_bounds_checks=True),
        name="experts",
    )(block_e, n_used, tok3, tok3, tok3, f_all, w1, b1.reshape(depth, n_exp, 1, two_de), w2,
      b2.reshape(depth, n_exp, 1, d))


def _final_kernel(slot_cur_ref, slot_nxt_ref, x_ref, gate_ref, gt_ref, gfin_ref, y_hbm, o_ref, ybuf, sem):
    t = pl.program_id(0) * pl.num_programs(1) + pl.program_id(1)
    n_t = pl.num_programs(0) * pl.num_programs(1)
    tm, d = x_ref.shape[1], x_ref.shape[2]
    sub = d // 2 // LANES
    cur = t % 2

    def issue(slot_ref, buf):
        def body(r, carry):
            for q in range(DMA_QUEUES):
                _gather_rows(y_hbm, ybuf, sem, slot_ref, r * DMA_QUEUES + q, buf, sub).start(priority=q)
            return carry
        lax.fori_loop(0, TOP_K * tm // DMA_QUEUES, body, 0)

    @pl.when(t == 0)
    def _():
        issue(slot_cur_ref, 0)

    @pl.when(t + 1 < n_t)
    def _():
        issue(slot_nxt_ref, 1 - cur)

    _wait_rows(y_hbm, ybuf, sem, cur)
    xn = x_ref[0] + gt_ref[0] * _combine_rows(ybuf, cur, gate_ref[0], tm, sub)
    o_ref[0] = xn * lax.rsqrt(jnp.mean(xn * xn, axis=-1, keepdims=True) + RMS_EPS) * gfin_ref[...]


def _final_call(x, slots, gates, gt, g_final, y_sorted, tm):
    b, n_tok, d = x.shape
    nt = n_tok // tm
    sub = d // 2 // LANES
    n_rows = TOP_K * tm
    slots_t = _tile_slots(slots, tm, n_rows)
    n_t = b * nt

    def smem(index):
        return pl.BlockSpec((1, 1, n_rows), index, memory_space=pltpu.SMEM)

    vmem = 4 * tm * d * 4 + 2 * n_rows * sub * LANES * 4 + 2 * tm * LANES * 4 + 4 * tm * d * 4
    return pl.pallas_call(
        _final_kernel,
        grid=(b, nt),
        in_specs=[smem(lambda bb, i: (bb * nt + i, 0, 0)),
                  smem(lambda bb, i: (jnp.minimum(bb * nt + i + 1, n_t - 1), 0, 0)),
                  pl.BlockSpec((1, tm, d), lambda bb, i: (bb, i, 0)),
                  pl.BlockSpec((1, tm, LANES), lambda bb, i: (bb, i, 0)),
                  pl.BlockSpec((1, 1, d), lambda bb, i: (bb, 0, 0)),
                  pl.BlockSpec((1, d), lambda bb, i: (0, 0)),
                  pl.BlockSpec(memory_space=pl.ANY)],
        out_specs=pl.BlockSpec((1, tm, d), lambda bb, i: (bb, i, 0)),
        out_shape=jax.ShapeDtypeStruct((b, n_tok, d), F32),
        scratch_shapes=[pltpu.VMEM((2, n_rows * sub, LANES), U32), pltpu.SemaphoreType.DMA((2,))],
        compiler_params=pltpu.CompilerParams(dimension_semantics=("arbitrary", "arbitrary"),
                                             vmem_limit_bytes=int(min(vmem * 5 // 4 + (4 << 20), VMEM_BUDGET)),
                                             disable_bounds_checks=True),
        name="final",
    )(slots_t, slots_t, x, gates, gt, g_final.reshape(1, d), y_sorted)


def _cumsum_rows(onehot):
    n, e = onehot.shape
    chunk = _pick(n, (512, 256, 128))
    oh = onehot.reshape(n // chunk, chunk, e).astype(BF16)
    tri = jnp.tril(jnp.ones((chunk, chunk), BF16))
    within = jnp.einsum('ij,cjk->cik', tri, oh, preferred_element_type=F32)
    totals = within[:, -1, :]
    offsets = jnp.cumsum(totals, axis=0) - totals
    return (within + offsets[:, None, :]).astype(I32).reshape(n, e)


def _route(top_e, n_exp):
    n_tok = top_e.shape[0]
    n_asg = n_tok * TOP_K
    flat_e = top_e.reshape(n_asg)
    onehot = (flat_e[:, None] == jnp.arange(n_exp, dtype=I32)[None, :]).astype(I32)
    csum = _cumsum_rows(onehot)
    rank = jnp.sum(csum * onehot, axis=1) - 1
    counts = csum[-1]
    padded = (counts + MOE_BLOCK - 1) // MOE_BLOCK * MOE_BLOCK
    pad_ends = jnp.cumsum(padded)
    pad_starts = pad_ends - padded
    slot = jnp.sum(jnp.where(onehot > 0, pad_starts[None, :], 0), axis=1) + rank
    n_blocks = -(-(n_asg + n_exp * (MOE_BLOCK - 1)) // MOE_BLOCK)
    n_slots = n_blocks * MOE_BLOCK
    slot_tok = jnp.zeros((n_slots,), I32).at[slot].set(jnp.arange(n_asg, dtype=I32) // TOP_K, unique_indices=True,
                                                       mode='promise_in_bounds')
    block_start = jnp.arange(n_blocks, dtype=I32) * MOE_BLOCK
    block_e = jnp.minimum(jnp.sum((pad_ends[None, :] <= block_start[:, None]).astype(I32), axis=1), n_exp - 1)
    n_used = (pad_ends[-1] // MOE_BLOCK).astype(I32).reshape(1)
    return slot.reshape(n_tok, TOP_K), slot_tok, block_e, n_used


def _rope_tables(seq):
    t = np.arange(seq)
    quarter = HEAD_DIM // 4
    inv = jnp.asarray(ROPE_BASE, F32) ** (-jnp.arange(quarter, dtype=F32) / quarter)
    ang_r = jnp.asarray(t // GRID_W, F32)[:, None] * inv[None, :]
    ang_c = jnp.asarray(t % GRID_W, F32)[:, None] * inv[None, :]
    cos = jnp.concatenate([jnp.cos(ang_r)] * 2 + [jnp.cos(ang_c)] * 2, axis=-1)
    sin = jnp.concatenate([-jnp.sin(ang_r), jnp.sin(ang_r), -jnp.sin(ang_c), jnp.sin(ang_c)], axis=-1)
    reps = LANES // HEAD_DIM
    return jnp.tile(cos, (1, reps)), jnp.tile(sin, (1, reps))


def kernel(x, c, ctx, c_ctx, w_ada, b_ada, g_mix, w_in, na_rpb, sc_conv, wa_sinks, w_na_out, w_sc_out, w_wa_out,
           w_out, g_ffn, w_router, b_router, w_exp_in, b_exp_in, w_exp_out, b_exp_out, g_final):
    b, s, d = x.shape
    l = ctx.shape[1]
    depth = w_ada.shape[0]
    n_exp = w_router.shape[-1]
    rows = s // GRID_W
    off_kv = OFF_GATES + N_BRANCHES * d
    n_cols = off_kv + N_KV_COLS
    assert w_in.shape[-1] == n_cols and s % NA_TQ == 0

    cvec = jnp.zeros((8, d), F32).at[:b].set(c).at[b].set(c_ctx)
    ada = _ada_call(cvec, w_ada, b_ada).reshape(depth, 8, N_ADA, d)
    rope = _rope_tables(s)
    q_scale = HEAD_DIM ** -0.5 * LOG2E
    sinks2 = wa_sinks * LOG2E

    tm_lat = _pick(s, (1024, 512, 256))
    tm_lat_moe = _pick(s, (512, 256))
    tn_proj = _pick(n_cols, (1280, 768, 512, 256))
    tn_kv = _pick(np.gcd(N_KV_COLS, off_kv), (640, 256, 128))
    tm_merge = _pick(s, (512, 256))
    tn_merge = _pick(d, (512, 256))
    tm_comb = _pick(s, (256,))
    rope_cols = ((OFF_WA_Q, WA_Q_DIM), (off_kv + KV_WA_K, WA_KV_DIM))

    w_perm = jnp.concatenate(
        [w_in[..., REF_OFF_SC:REF_OFF_GATES], w_in[..., REF_OFF_WA_Q:REF_OFF_SC] * q_scale,
         w_in[..., REF_OFF_NA_Q:REF_OFF_WA_Q] * q_scale, w_in[..., REF_OFF_GATES:], w_in[..., :REF_OFF_NA_Q]],
        axis=-1).astype(BF16)
    wr_hi = w_router.astype(BF16)
    wr_lo = (w_router - wr_hi.astype(F32)).astype(BF16)
    w_r2 = jnp.concatenate([wr_hi, wr_lo], axis=-1)
    merge_w = (sc_conv, w_na_out.astype(BF16), w_sc_out.astype(BF16), w_wa_out.astype(BF16), w_out.astype(BF16))
    sub = d // 2 // LANES

    x_lat, x_ctx = x, ctx
    moe_lat = moe_ctx = None
    for layer in range(depth):
        last = layer == depth - 1
        mod_lat = [ada[layer, :b, k][:, None, :] for k in range(N_ADA)]
        mod_ctx = [jnp.broadcast_to(ada[layer, b, k][None, None, :], (b, 1, d)) for k in range(N_ADA)]

        p_lat = _proj_call(x_lat, g_mix, mod_lat[0], mod_lat[1], w_perm, layer, 0, n_cols, rope,
                           tm_lat if moe_lat is None else tm_lat_moe, tn_proj, rope_cols, moe_lat)
        if last:
            p_ctx = _proj_call(x_ctx, g_mix, mod_ctx[0], mod_ctx[1], w_perm, layer, off_kv, N_KV_COLS, None, l,
                               tn_kv, (), moe_ctx)
            ctx_kv = 0
        else:
            p_ctx = _proj_call(x_ctx, g_mix, mod_ctx[0], mod_ctx[1], w_perm, layer, 0, n_cols, None, l, tn_proj, (),
                               moe_ctx)
            ctx_kv = off_kv
        if moe_lat is not None:
            p_lat, x_lat = p_lat
            p_ctx, x_ctx = p_ctx

        bias = _na_bias_tables(na_rpb[layer] * LOG2E, rows)
        o_na = _na_call(p_lat, p_ctx, bias, off_kv, ctx_kv)
        o_wa = _wa_call(p_lat, p_ctx, sinks2[layer], off_kv, ctx_kv)

        x_lat, f_lat, te_lat, tg_lat = _merge_call(
            x_lat, o_na, o_wa, p_lat, layer, *merge_w, mod_lat[2], g_ffn, mod_lat[3], mod_lat[4],
            w_r2, b_router, tm_merge, tn_merge)
        f_all = f_lat.reshape(b * s * sub, LANES)
        te_all = te_lat.reshape(b * s, LANES)[:, :TOP_K]
        if not last:
            o_na_c, o_wa_c = _ctx_attn_call(p_ctx, sinks2[layer], off_kv)
            x_ctx, f_ctx, te_ctx, tg_ctx = _merge_call(
                x_ctx, o_na_c, o_wa_c, p_ctx, layer, *merge_w, mod_ctx[2], g_ffn, mod_ctx[3], mod_ctx[4],
                w_r2, b_router, l, tn_merge)
            f_all = jnp.concatenate([f_all, f_ctx.reshape(b * l * sub, LANES)], axis=0)
            te_all = jnp.concatenate([te_all, te_ctx.reshape(b * l, LANES)[:, :TOP_K]], axis=0)

        slots, slot_tok, block_e, n_used = _route(te_all, n_exp)
        y_sorted = _moe_call(layer, block_e, n_used, slot_tok, f_all, w_exp_in, b_exp_in, w_exp_out, b_exp_out)
        moe_lat = (slots[:b * s].reshape(b, s, TOP_K), tg_lat, mod_lat[5], y_sorted)
        if not last:
            moe_ctx = (slots[b * s:].reshape(b, l, TOP_K), tg_ctx, mod_ctx[5], y_sorted)
    return _final_call(x_lat, *moe_lat[:3], g_final, moe_lat[3], tm_comb)
```
